```python
import math
import jax, jax.numpy as jnp
from jax import lax
import numpy as np

D_MODEL = 1024
BATCH = 4
SEQ = 4096
DEPTH = 1

D_SSM = D_MODEL // 2
SSM_GROUP = 16
N_SSM_GROUPS = D_SSM // SSM_GROUP
SSM_STATE = 64
DT_MIN = 1e-3
DT_MAX = 1e-1
N_HEADS = 8
N_KV_HEADS = 2
HEAD_DIM = 64
D_ATTN = N_HEADS * HEAD_DIM
D_KV = N_KV_HEADS * HEAD_DIM
IDX_HEADS = 16
IDX_DIM = 64
TOPK = 256
Q_BLOCK = 128
ROPE_THETA = 10000.0
N_EXPERTS = 32
TOP_K_EXPERTS = 4
D_EXPERT = D_MODEL
SWIGLU_LIMIT = 7.0
SWIGLU_ALPHA = 1.702
RMS_EPS = 1e-5
NEG = -1e30
N_BRANCHES = 2
IN_SPLITS = (D_SSM, D_ATTN, D_KV, D_KV, IDX_HEADS * IDX_DIM, IDX_DIM, IDX_HEADS, D_MODEL, D_MODEL)
D_IN = sum(IN_SPLITS)
IN_OFFSETS = tuple(sum(IN_SPLITS[:i + 1]) for i in range(len(IN_SPLITS) - 1))

kernel_name = "hybrid_s5_dsa_moe_block"


def rms_norm(x, g):
    xf = x.astype(jnp.float32)
    out = xf * lax.rsqrt(jnp.mean(xf * xf, axis=-1, keepdims=True) + RMS_EPS) * g.astype(jnp.float32)
    return out.astype(x.dtype)


def rope(x, pos):
    d = x.shape[-1]
    half = d // 2
    inv = ROPE_THETA ** (-jnp.arange(half, dtype=jnp.float32) / half)
    ang = pos[:, None] * inv[None, :]
    cos = jnp.cos(ang)[:, None, :]
    sin = jnp.sin(ang)[:, None, :]
    xf = x.astype(jnp.float32)
    x1, x2 = xf[..., :half], xf[..., half:]
    out = jnp.concatenate([x1 * cos - x2 * sin, x2 * cos + x1 * sin], axis=-1)
    return out.astype(x.dtype)


def _complex_scan_op(e1, e2):
    a1r, a1i, b1r, b1i = e1
    a2r, a2i, b2r, b2i = e2
    ar = a2r * a1r - a2i * a1i
    ai = a2r * a1i + a2i * a1r
    br = a2r * b1r - a2i * b1i + b2r
    bi = a2r * b1i + a2i * b1r + b2i
    return (ar, ai, br, bi)


def s5_mixer(u, lam_re, lam_im, log_dt, b_re, b_im, c_re, c_im, d_skip, w_glu, b_glu):
    f32 = jnp.float32
    bsz, L, _ = u.shape
    uf = u.astype(f32).reshape(bsz, L, N_SSM_GROUPS, SSM_GROUP)
    lr, li = lam_re.astype(f32), lam_im.astype(f32)
    dt = jnp.exp(log_dt.astype(f32))[:, None]
    mag = jnp.exp(lr * dt)
    ar = mag * jnp.cos(li * dt)
    ai = mag * jnp.sin(li * dt)
    nr, ni = ar - 1.0, ai
    den = lr * lr + li * li
    zr = (nr * lr + ni * li) / den
    zi = (ni * lr - nr * li) / den
    br, bi = b_re.astype(f32), b_im.astype(f32)
    bbar_re = zr[:, :, None] * br - zi[:, :, None] * bi
    bbar_im = zr[:, :, None] * bi + zi[:, :, None] * br
    bu_re = jnp.einsum('blgh,gph->blgp', uf, bbar_re)
    bu_im = jnp.einsum('blgh,gph->blgp', uf, bbar_im)
    a_re = jnp.broadcast_to(ar, bu_re.shape)
    a_im = jnp.broadcast_to(ai, bu_re.shape)
    _, _, xr, xi = lax.associative_scan(_complex_scan_op, (a_re, a_im, bu_re, bu_im), axis=1)
    y = (jnp.einsum('blgp,ghp->blgh', xr, c_re.astype(f32))
         - jnp.einsum('blgp,ghp->blgh', xi, c_im.astype(f32))
         + d_skip.astype(f32) * uf)
    y = jax.nn.gelu(y.reshape(bsz, L, D_SSM))
    y = y * jax.nn.sigmoid(y @ w_glu.astype(f32) + b_glu.astype(f32))
    return y.astype(u.dtype)


def dsa_attention(q, k, v, q_idx, k_idx, w_idx):
    f32 = jnp.float32
    bsz, L = q.shape[0], q.shape[1]
    n_sel = min(TOPK, L // 4)
    n_blk = L // Q_BLOCK
    rep = N_HEADS // N_KV_HEADS
    q5 = q.reshape(bsz, L, N_KV_HEADS, rep, HEAD_DIM)
    kif = k_idx.astype(f32)
    key_pos = jnp.arange(L)
    bidx = jnp.arange(bsz)[:, None, None]

    def block(i):
        start = i * Q_BLOCK
        qb = lax.dynamic_slice_in_dim(q5, start, Q_BLOCK, axis=1).astype(f32)
        qib = lax.dynamic_slice_in_dim(q_idx, start, Q_BLOCK, axis=1).astype(f32)
        wib = lax.dynamic_slice_in_dim(w_idx, start, Q_BLOCK, axis=1).astype(f32) * (IDX_HEADS ** -0.5)
        qpos = start + jnp.arange(Q_BLOCK)
        s = jnp.einsum('bthd,bsd->bths', qib, kif) * (IDX_DIM ** -0.5)
        score = jnp.einsum('bths,bth->bts', jax.nn.relu(s), wib)
        causal = key_pos[None, :] <= qpos[:, None]
        score = jnp.where(causal[None], score, NEG)
        _, sel = lax.top_k(score, n_sel)
        valid = sel <= qpos[None, :, None]
        ks = k[bidx, sel].astype(f32)
        vs = v[bidx, sel].astype(f32)
        logits = jnp.einsum('btgrd,btkgd->btgrk', qb, ks) * (HEAD_DIM ** -0.5)
        logits = jnp.where(valid[:, :, None, None, :], logits, NEG)
        p = jax.nn.softmax(logits, axis=-1)
        o = jnp.einsum('btgrk,btkgd->btgrd', p, vs)
        return o.reshape(bsz, Q_BLOCK, D_ATTN).astype(q.dtype)

    out = lax.map(block, jnp.arange(n_blk))
    return out.transpose(1, 0, 2, 3).reshape(bsz, L, D_ATTN)


def moe(h, w_router, b_router, w_moe_in, b_moe_in, w_moe_out, b_moe_out):
    f32 = jnp.float32
    hf = h.reshape(-1, D_MODEL)
    logits = (hf @ w_router + b_router).astype(f32)
    vals, idx = lax.top_k(logits, TOP_K_EXPERTS)
    wts = jax.nn.softmax(vals, axis=-1)
    gates = jnp.sum(jax.nn.one_hot(idx, N_EXPERTS, dtype=f32) * wts[..., None], axis=1)
    out = jnp.zeros(hf.shape, f32)
    for e in range(N_EXPERTS):
        z = hf @ w_moe_in[e] + b_moe_in[e]
        zg = jnp.minimum(z[:, ::2], SWIGLU_LIMIT)
        zl = jnp.clip(z[:, 1::2], -SWIGLU_LIMIT, SWIGLU_LIMIT)
        act = zg * jax.nn.sigmoid(SWIGLU_ALPHA * zg) * (zl + 1.0)
        out = out + gates[:, e:e + 1] * (act @ w_moe_out[e] + b_moe_out[e]).astype(f32)
    return out.reshape(h.shape).astype(h.dtype)


def hybrid_layer(x, norm1_g, w_in, b_gate, ssm_lam_re, ssm_lam_im, ssm_log_dt, ssm_b_re, ssm_b_im,
                 ssm_c_re, ssm_c_im, ssm_d, w_glu, b_glu, w_proj_ssm, w_proj_attn, w_out,
                 norm2_g, w_router, b_router, w_moe_in, b_moe_in, w_moe_out, b_moe_out):
    bsz, L, _ = x.shape
    xn = rms_norm(x, norm1_g)
    proj = xn @ w_in
    u, q, k, v, qi, ki, wi, gs, ga = jnp.split(proj, IN_OFFSETS, axis=-1)
    pos = jnp.arange(L, dtype=jnp.float32)
    q = rope(q.reshape(bsz, L, N_HEADS, HEAD_DIM), pos)
    k = rope(k.reshape(bsz, L, N_KV_HEADS, HEAD_DIM), pos)
    v = v.reshape(bsz, L, N_KV_HEADS, HEAD_DIM)
    qi = rope(qi.reshape(bsz, L, IDX_HEADS, IDX_DIM), pos)
    ki = rope(ki.reshape(bsz, L, 1, IDX_DIM), pos)[:, :, 0]
    y_ssm = s5_mixer(u, ssm_lam_re, ssm_lam_im, ssm_log_dt, ssm_b_re, ssm_b_im,
                     ssm_c_re, ssm_c_im, ssm_d, w_glu, b_glu)
    y_att = dsa_attention(q, k, v, qi, ki, wi)
    g_ssm = jax.nn.sigmoid(gs + b_gate[0])
    g_att = jax.nn.sigmoid(ga + b_gate[1])
    mixed = g_ssm * (y_ssm @ w_proj_ssm) + g_att * (y_att @ w_proj_attn)
    x = x + mixed @ w_out
    x = x + moe(rms_norm(x, norm2_g), w_router, b_router, w_moe_in, b_moe_in, w_moe_out, b_moe_out)
    return x


def setup_inputs(seed: int = 0) -> dict:
    key = jax.random.key(seed)
    ks = jax.random.split(key, 32)
    f32 = jnp.float32

    def nrm(k, shape, scale):
        return jax.random.normal(k, shape, f32) * scale

    G, P, H = N_SSM_GROUPS, SSM_STATE, SSM_GROUP
    n = jnp.arange(P, dtype=f32)
    x = nrm(ks[0], (BATCH, SEQ, D_MODEL), 1.0)
    norm1_g = 1.0 + nrm(ks[1], (DEPTH, D_MODEL), 0.02)
    w_in = nrm(ks[2], (DEPTH, D_MODEL, D_IN), D_MODEL ** -0.5)
    b_gate = nrm(ks[3], (DEPTH, N_BRANCHES, D_MODEL), 0.02)
    ssm_lam_re = -0.5 + nrm(ks[4], (DEPTH, G, P), 0.01)
    ssm_lam_im = math.pi * n[None, None, :] + nrm(ks[5], (DEPTH, G, P), 0.01)
    ssm_log_dt = jax.random.uniform(ks[6], (DEPTH, G), f32, math.log(DT_MIN), math.log(DT_MAX))
    ssm_b_re = nrm(ks[7], (DEPTH, G, P, H), (2.0 * H) ** -0.5)
    ssm_b_im = nrm(ks[8], (DEPTH, G, P, H), (2.0 * H) ** -0.5)
    ssm_c_re = nrm(ks[9], (DEPTH, G, H, P), (2.0 * P) ** -0.5)
    ssm_c_im = nrm(ks[10], (DEPTH, G, H, P), (2.0 * P) ** -0.5)
    ssm_d = nrm(ks[11], (DEPTH, G, H), 1.0)
    w_glu = nrm(ks[12], (DEPTH, D_SSM, D_SSM), D_SSM ** -0.5)
    b_glu = nrm(ks[13], (DEPTH, D_SSM), 0.02)
    w_proj_ssm = nrm(ks[14], (DEPTH, D_SSM, D_MODEL), D_SSM ** -0.5)
    w_proj_attn = nrm(ks[15], (DEPTH, D_ATTN, D_MODEL), D_ATTN ** -0.5)
    w_out = nrm(ks[16], (DEPTH, D_MODEL, D_MODEL), D_MODEL ** -0.5)
    norm2_g = 1.0 + nrm(ks[17], (DEPTH, D_MODEL), 0.02)
    w_router = nrm(ks[18], (DEPTH, D_MODEL, N_EXPERTS), D_MODEL ** -0.5)
    b_router = nrm(ks[19], (DEPTH, N_EXPERTS), 0.01)
    w_moe_in = nrm(ks[20], (DEPTH, N_EXPERTS, D_MODEL, 2 * D_EXPERT), D_MODEL ** -0.5)
    b_moe_in = nrm(ks[21], (DEPTH, N_EXPERTS, 2 * D_EXPERT), 0.01)
    w_moe_out = nrm(ks[22], (DEPTH, N_EXPERTS, D_EXPERT, D_MODEL), D_EXPERT ** -0.5)
    b_moe_out = nrm(ks[23], (DEPTH, N_EXPERTS, D_MODEL), 0.01)
    norm_f_g = 1.0 + nrm(ks[24], (D_MODEL,), 0.02)
    return {"x": x, "norm1_g": norm1_g, "w_in": w_in, "b_gate": b_gate,
            "ssm_lam_re": ssm_lam_re, "ssm_lam_im": ssm_lam_im, "ssm_log_dt": ssm_log_dt,
            "ssm_b_re": ssm_b_re, "ssm_b_im": ssm_b_im, "ssm_c_re": ssm_c_re, "ssm_c_im": ssm_c_im,
            "ssm_d": ssm_d, "w_glu": w_glu, "b_glu": b_glu, "w_proj_ssm": w_proj_ssm,
            "w_proj_attn": w_proj_attn, "w_out": w_out, "norm2_g": norm2_g,
            "w_router": w_router, "b_router": b_router, "w_moe_in": w_moe_in, "b_moe_in": b_moe_in,
            "w_moe_out": w_moe_out, "b_moe_out": b_moe_out, "norm_f_g": norm_f_g}


def reference(x, norm1_g, w_in, b_gate, ssm_lam_re, ssm_lam_im, ssm_log_dt, ssm_b_re, ssm_b_im,
              ssm_c_re, ssm_c_im, ssm_d, w_glu, b_glu, w_proj_ssm, w_proj_attn, w_out,
              norm2_g, w_router, b_router, w_moe_in, b_moe_in, w_moe_out, b_moe_out, norm_f_g):
    h = x
    for l in range(DEPTH):
        h = hybrid_layer(h, norm1_g[l], w_in[l], b_gate[l], ssm_lam_re[l], ssm_lam_im[l], ssm_log_dt[l],
                         ssm_b_re[l], ssm_b_im[l], ssm_c_re[l], ssm_c_im[l], ssm_d[l], w_glu[l], b_glu[l],
                         w_proj_ssm[l], w_proj_attn[l], w_out[l], norm2_g[l], w_router[l], b_router[l],
                         w_moe_in[l], b_moe_in[l], w_moe_out[l], b_moe_out[l])
    return rms_norm(h, norm_f_g)
```

```python
import functools
import math

import numpy as np
import jax
import jax.numpy as jnp
from jax import lax
from jax.experimental import pallas as pl
from jax.experimental.pallas import tpu as pltpu

D_MODEL = 1024
D_SSM = 512
SSM_GROUP = 16
N_SSM_GROUPS = 32
SSM_STATE = 64
N_HEADS = 8
N_KV_HEADS = 2
HEAD_DIM = 64
D_ATTN = N_HEADS * HEAD_DIM
D_KV = N_KV_HEADS * HEAD_DIM
IDX_HEADS = 16
IDX_DIM = 64
TOPK = 256
ROPE_THETA = 10000.0
N_EXPERTS = 32
TOP_K_EXPERTS = 4
D_EXPERT = D_MODEL
SWIGLU_LIMIT = 7.0
SWIGLU_ALPHA = 1.702
RMS_EPS = 1e-5
NEG = -1e30

LANES = 128
SUBLANES = 8
VMEM_LIMIT = 56 * 1024 * 1024

F32 = jnp.float32
BF16 = jnp.bfloat16
I32 = jnp.int32


def _cparams(sem):
    return pltpu.CompilerParams(dimension_semantics=sem, vmem_limit_bytes=VMEM_LIMIT)


def _const_spec(shape):
    nd = len(shape)
    return pl.BlockSpec(shape, lambda *_: (0,) * nd)


IN_TM = 512


def _in_proj_kernel(x_ref, g_ref, cos_ref, sin_ref, bg_ref,
                    wu_ref, wq_ref, wqr_ref, wkv_ref, wqi_ref, wqir_ref, wkw_ref, wgs_ref, wga_ref,
                    u_ref, qt_ref, k_ref, vt_ref, qit_ref, ki_ref, wit_ref, gs_ref, ga_ref):
    x = x_ref[...]
    xn = x * lax.rsqrt(jnp.mean(x * x, axis=-1, keepdims=True) + RMS_EPS) * g_ref[...]
    xb = xn.astype(BF16)

    def mm(w_ref):
        return jnp.dot(xb, w_ref[...], preferred_element_type=F32)

    cos = cos_ref[...]
    sin = sin_ref[...]

    def rope(a, ar):
        reps = a.shape[1] // LANES
        return a * jnp.tile(cos, (1, reps)) + ar * jnp.tile(sin, (1, reps))

    u_ref[...] = mm(wu_ref)
    q = rope(mm(wq_ref), mm(wqr_ref)) * (HEAD_DIM ** -0.5)
    qt_ref[...] = q.T.astype(BF16)
    kv = mm(wkv_ref)
    k = rope(kv[:, :D_KV], kv[:, D_KV:2 * D_KV])
    for g in range(N_KV_HEADS):
        k_ref[g] = k[:, g * HEAD_DIM:(g + 1) * HEAD_DIM].astype(BF16)
    vt_ref[...] = kv[:, 2 * D_KV:].T.astype(BF16)
    qi = rope(mm(wqi_ref), mm(wqir_ref)) * (IDX_DIM ** -0.5)
    qit_ref[...] = qi.T.astype(BF16)
    kw = mm(wkw_ref)
    ki = kw[:, :IDX_DIM] * cos[:, :IDX_DIM] + kw[:, IDX_DIM:2 * IDX_DIM] * sin[:, :IDX_DIM]
    ki_ref[...] = ki.astype(BF16)
    wit_ref[...] = (kw[:, LANES:2 * LANES].T)[:IDX_HEADS] * (IDX_HEADS ** -0.5)
    gs_ref[...] = jax.nn.sigmoid(mm(wgs_ref) + bg_ref[0:1, :]).astype(BF16)
    ga_ref[...] = jax.nn.sigmoid(mm(wga_ref) + bg_ref[1:2, :]).astype(BF16)


def _rot_half_cols(w, n_heads, d):
    k = w.shape[0]
    w4 = w.reshape(k, n_heads, 2, d // 2)
    return w4[:, :, ::-1, :].reshape(k, n_heads * d)


def _in_proj(x2, norm1_g, w_in, b_gate, seq_len):
    n = x2.shape[0]
    tm = min(IN_TM, seq_len)
    o = 0
    parts = []
    for width in (D_SSM, D_ATTN, D_KV, D_KV, IDX_HEADS * IDX_DIM, IDX_DIM, IDX_HEADS, D_MODEL, D_MODEL):
        parts.append(w_in[:, o:o + width])
        o += width
    wu, wq, wk, wv, wqi, wki, wwi, wgs, wga = parts
    wqr = _rot_half_cols(wq, N_HEADS, HEAD_DIM)
    wkr = _rot_half_cols(wk, N_KV_HEADS, HEAD_DIM)
    wqir = _rot_half_cols(wqi, IDX_HEADS, IDX_DIM)
    wkir = _rot_half_cols(wki, 1, IDX_DIM)
    wkv = jnp.concatenate([wk, wkr, wv], axis=1)
    wkw = jnp.concatenate([wki, wkir, wwi, jnp.zeros((D_MODEL, LANES - IDX_HEADS), F32)], axis=1)
    weights = [w.astype(BF16) for w in (wu, wq, wqr, wkv, wqi, wqir, wkw, wgs, wga)]

    half = HEAD_DIM // 2
    inv = ROPE_THETA ** (-jnp.arange(half, dtype=F32) / half)
    ang = jnp.arange(seq_len, dtype=F32)[:, None] * inv[None, :]
    cos = jnp.tile(jnp.cos(ang), (1, 4))
    sin = jnp.tile(jnp.concatenate([-jnp.sin(ang), jnp.sin(ang)], axis=1), (1, 2))

    tiles_per_seq = seq_len // tm
    row = lambda i: (i, 0)
    col = lambda i: (0, i)
    out_shapes = (
        jax.ShapeDtypeStruct((n, D_SSM), F32),
        jax.ShapeDtypeStruct((D_ATTN, n), BF16),
        jax.ShapeDtypeStruct((N_KV_HEADS, n, HEAD_DIM), BF16),
        jax.ShapeDtypeStruct((D_KV, n), BF16),
        jax.ShapeDtypeStruct((IDX_HEADS * IDX_DIM, n), BF16),
        jax.ShapeDtypeStruct((n, IDX_DIM), BF16),
        jax.ShapeDtypeStruct((IDX_HEADS, n), F32),
        jax.ShapeDtypeStruct((n, D_MODEL), BF16),
        jax.ShapeDtypeStruct((n, D_MODEL), BF16),
    )
    out_specs = (
        pl.BlockSpec((tm, D_SSM), row),
        pl.BlockSpec((D_ATTN, tm), col),
        pl.BlockSpec((N_KV_HEADS, tm, HEAD_DIM), lambda i: (0, i, 0)),
        pl.BlockSpec((D_KV, tm), col),
        pl.BlockSpec((IDX_HEADS * IDX_DIM, tm), col),
        pl.BlockSpec((tm, IDX_DIM), row),
        pl.BlockSpec((IDX_HEADS, tm), col),
        pl.BlockSpec((tm, D_MODEL), row),
        pl.BlockSpec((tm, D_MODEL), row),
    )
    in_specs = [
        pl.BlockSpec((tm, D_MODEL), row),
        _const_spec((1, D_MODEL)),
        pl.BlockSpec((tm, LANES), lambda i: (i % tiles_per_seq, 0)),
        pl.BlockSpec((tm, LANES), lambda i: (i % tiles_per_seq, 0)),
        _const_spec((2, D_MODEL)),
    ] + [_const_spec(w.shape) for w in weights]
    return pl.pallas_call(
        _in_proj_kernel,
        grid=(n // tm,),
        in_specs=in_specs,
        out_specs=out_specs,
        out_shape=out_shapes,
        compiler_params=_cparams(("arbitrary",)),
        name="in_proj",
    )(x2, norm1_g.reshape(1, D_MODEL), cos, sin, b_gate, *weights)


S5_T = 512
S5_CHUNK = 512
N_STATE = N_SSM_GROUPS * SSM_STATE
S5_HALF_IN = D_SSM // 2
S5_HALF_ST = N_STATE // 2


def _cmul(ar, ai, br, bi):
    return ar * br - ai * bi, ar * bi + ai * br


def _s5_kernel(u_ref, lr_ref, li_ref, ldt_ref, bre_ref, bim_ref, cre_ref, cim_ref, d_ref,
               wglu_ref, bglu_ref, y_ref,
               bb_ref, cc_ref, pw_re_ref, pw_im_ref, st_re_ref, st_im_ref, carry_re_ref, carry_im_ref):
    first = jnp.logical_and(pl.program_id(0) == 0, pl.program_id(1) == 0)

    @pl.when(first)
    def _():
        lr, li = lr_ref[...], li_ref[...]
        dt = jnp.exp(ldt_ref[...])
        mag = jnp.exp(lr * dt)
        ar = mag * jnp.cos(li * dt)
        ai = mag * jnp.sin(li * dt)
        den = lr * lr + li * li
        zr = ((ar - 1.0) * lr + ai * li) / den
        zi = (ai * lr - (ar - 1.0) * li) / den
        for h in range(2):
            rows = slice(h * S5_HALF_IN, (h + 1) * S5_HALF_IN)
            cols = slice(h * S5_HALF_ST, (h + 1) * S5_HALF_ST)
            bre, bim = bre_ref[rows, cols], bim_ref[rows, cols]
            bb_ref[h, :, :S5_HALF_ST] = (zr[:, cols] * bre - zi[:, cols] * bim).astype(BF16)
            bb_ref[h, :, S5_HALF_ST:] = (zr[:, cols] * bim + zi[:, cols] * bre).astype(BF16)
            cc_ref[h, :S5_HALF_ST, :] = cre_ref[cols, rows].astype(BF16)
            cc_ref[h, S5_HALF_ST:, :] = (-cim_ref[cols, rows]).astype(BF16)
        pr, pi = ar, ai
        pw_re_ref[0:1, :] = pr
        pw_im_ref[0:1, :] = pi
        for n in range(1, SUBLANES):
            pr, pi = _cmul(pr, pi, ar, ai)
            pw_re_ref[n:n + 1, :] = pr
            pw_im_ref[n:n + 1, :] = pi

    @pl.when(pl.program_id(1) == 0)
    def _():
        carry_re_ref[...] = jnp.zeros_like(carry_re_ref)
        carry_im_ref[...] = jnp.zeros_like(carry_im_ref)

    u = u_ref[...]
    ub = u.astype(BF16)
    for h in range(2):
        bu = jnp.dot(ub[:, h * S5_HALF_IN:(h + 1) * S5_HALF_IN], bb_ref[h], preferred_element_type=F32)
        st_re_ref[:, h * S5_HALF_ST:(h + 1) * S5_HALF_ST] = bu[:, :S5_HALF_ST]
        st_im_ref[:, h * S5_HALF_ST:(h + 1) * S5_HALF_ST] = bu[:, S5_HALF_ST:]

    t_len = u.shape[0]
    row = lax.broadcasted_iota(I32, (SUBLANES, S5_CHUNK), 0)
    for c in range(N_STATE // S5_CHUNK):
        lanes = slice(c * S5_CHUNK, (c + 1) * S5_CHUNK)
        p_re, p_im = pw_re_ref[:, lanes], pw_im_ref[:, lanes]
        steps = []
        for d in (1, 2, 4):
            a_re = jnp.where(row >= d, jnp.broadcast_to(p_re[d - 1:d, :], row.shape), 0.0)
            a_im = jnp.where(row >= d, jnp.broadcast_to(p_im[d - 1:d, :], row.shape), 0.0)
            steps.append((d, a_re, a_im))

        def block(i, carry):
            c_re, c_im = carry
            r0 = pl.multiple_of(i * SUBLANES, SUBLANES)
            xr = st_re_ref[pl.ds(r0, SUBLANES), lanes]
            xi = st_im_ref[pl.ds(r0, SUBLANES), lanes]
            for d, a_re, a_im in steps:
                sr = pltpu.roll(xr, d, 0)
                si = pltpu.roll(xi, d, 0)
                tr, ti = _cmul(a_re, a_im, sr, si)
                xr, xi = xr + tr, xi + ti
            tr, ti = _cmul(p_re, p_im, jnp.broadcast_to(c_re, xr.shape), jnp.broadcast_to(c_im, xi.shape))
            xr, xi = xr + tr, xi + ti
            st_re_ref[pl.ds(r0, SUBLANES), lanes] = xr
            st_im_ref[pl.ds(r0, SUBLANES), lanes] = xi
            return xr[SUBLANES - 1:, :], xi[SUBLANES - 1:, :]

        def two_blocks(j, carry):
            return block(2 * j + 1, block(2 * j, carry))

        c_re, c_im = lax.fori_loop(0, t_len // (2 * SUBLANES), two_blocks,
                                   (carry_re_ref[:, lanes], carry_im_ref[:, lanes]))
        carry_re_ref[:, lanes] = c_re
        carry_im_ref[:, lanes] = c_im

    ys = []
    for h in range(2):
        cols = slice(h * S5_HALF_ST, (h + 1) * S5_HALF_ST)
        xs = jnp.concatenate([st_re_ref[:, cols], st_im_ref[:, cols]], axis=1).astype(BF16)
        ys.append(jnp.dot(xs, cc_ref[h], preferred_element_type=F32))
    y = jnp.concatenate(ys, axis=1) + d_ref[...] * u
    y = jax.nn.gelu(y)
    gate = jnp.dot(y.astype(BF16), wglu_ref[...], preferred_element_type=F32) + bglu_ref[...]
    y_ref[...] = (y * jax.nn.sigmoid(gate)).astype(BF16)


def _s5(u, batch, seq_len, lam_re, lam_im, log_dt, b_re, b_im, c_re, c_im, d_skip, w_glu, b_glu):
    n = u.shape[0]
    t = min(S5_T, seq_len)
    g, p, hh = N_SSM_GROUPS, SSM_STATE, SSM_GROUP
    eye = jnp.eye(g, dtype=F32)
    bre_bd = jnp.einsum("gph,gk->ghkp", b_re, eye).reshape(D_SSM, N_STATE)
    bim_bd = jnp.einsum("gph,gk->ghkp", b_im, eye).reshape(D_SSM, N_STATE)
    cre_bd = jnp.einsum("ghp,gk->gpkh", c_re, eye).reshape(N_STATE, D_SSM)
    cim_bd = jnp.einsum("ghp,gk->gpkh", c_im, eye).reshape(N_STATE, D_SSM)
    lr = lam_re.reshape(1, N_STATE)
    li = lam_im.reshape(1, N_STATE)
    ldt = jnp.repeat(log_dt, p).reshape(1, N_STATE)
    tiles = seq_len // t
    consts = [lr, li, ldt, bre_bd, bim_bd, cre_bd, cim_bd, d_skip.reshape(1, D_SSM),
              w_glu.astype(BF16), b_glu.reshape(1, D_SSM)]
    return pl.pallas_call(
        _s5_kernel,
        grid=(batch, tiles),
        in_specs=[pl.BlockSpec((t, D_SSM), lambda b, i: (b * tiles + i, 0))]
        + [_const_spec(c.shape) for c in consts],
        out_specs=pl.BlockSpec((t, D_SSM), lambda b, i: (b * tiles + i, 0)),
        out_shape=jax.ShapeDtypeStruct((n, D_SSM), BF16),
        scratch_shapes=[
            pltpu.VMEM((2, S5_HALF_IN, 2 * S5_HALF_ST), BF16),
            pltpu.VMEM((2, 2 * S5_HALF_ST, S5_HALF_IN), BF16),
            pltpu.VMEM((SUBLANES, N_STATE), F32),
            pltpu.VMEM((SUBLANES, N_STATE), F32),
            pltpu.VMEM((t, N_STATE), F32),
            pltpu.VMEM((t, N_STATE), F32),
            pltpu.VMEM((1, N_STATE), F32),
            pltpu.VMEM((1, N_STATE), F32),
        ],
        compiler_params=_cparams(("arbitrary", "arbitrary")),
        name="s5",
    )(u, *consts)


AT_TQ = 256
AT_TK = 128
INT_MIN = -2 ** 31
KEY_NEG = int(np.float32(NEG).view(np.int32)) ^ 0x7FFFFFFF


def _attn_kernel(qt_ref, qit_ref, wit_ref, k_ref, ki_ref, vt_ref, y_ref, key_ref, *, seq_len, n_sel):
    j = pl.program_id(1)
    tq, tk = AT_TQ, AT_TK
    n_diag = tq // tk
    n_full = j * n_diag
    q_pos = j * tq + lax.broadcasted_iota(I32, (tk, tq), 1)
    k_iota = lax.broadcasted_iota(I32, (tk, tq), 0)

    def tile_start(kt):
        return pl.multiple_of(kt * tk, tk)

    def score_tile(kt, diag):
        r0 = tile_start(kt)
        ki_t = ki_ref[pl.ds(r0, tk), :]
        acc = jnp.zeros((tk, tq), F32)
        for h in range(IDX_HEADS):
            s = jnp.dot(ki_t, qit_ref[h * IDX_DIM:(h + 1) * IDX_DIM, :], preferred_element_type=F32)
            acc = acc + wit_ref[h:h + 1, :] * jnp.maximum(s, 0.0)
        if diag:
            acc = jnp.where(r0 + k_iota <= q_pos, acc, NEG)
        bits = lax.bitcast_convert_type(acc, I32)
        key_ref[pl.ds(r0, tk), :] = jnp.where(bits < 0, bits ^ 0x7FFFFFFF, bits)

    def score_body(kt, c):
        score_tile(kt, False)
        return c

    lax.fori_loop(0, n_full, score_body, 0)
    for dd in range(n_diag):
        score_tile(n_full + dd, True)

    n_beyond = seq_len - (j + 1) * tq

    def count_tile(c, cnt, cand):
        r0 = pl.multiple_of(c * tq, tq)
        ge = jnp.where(key_ref[pl.ds(r0, tq), :] >= cand, 1, 0)
        rows = tq // 4
        parts = [ge[i * rows:(i + 1) * rows].reshape(rows // SUBLANES, SUBLANES, tq).sum(axis=0)
                 for i in range(4)]
        return cnt + ((parts[0] + parts[1]) + (parts[2] + parts[3]))

    def bit_step(b, thr_u):
        cand_u = thr_u | lax.shift_left(jnp.int32(1), 31 - b)
        cand = cand_u ^ INT_MIN
        cnt = lax.fori_loop(0, j + 1, functools.partial(count_tile, cand=cand),
                            jnp.zeros((SUBLANES, tq), I32))
        total = jnp.sum(cnt, axis=0, keepdims=True) + jnp.where(cand <= KEY_NEG, n_beyond, 0)
        return jnp.where(total >= n_sel, cand_u, thr_u)

    thr = lax.fori_loop(0, 32, bit_step, jnp.zeros((1, tq), I32)) ^ INT_MIN

    outs = []
    for h in range(N_HEADS):
        g = h // (N_HEADS // N_KV_HEADS)
        q_h = qt_ref[h * HEAD_DIM:(h + 1) * HEAD_DIM, :]

        def att_tile(kt, carry, diag):
            m, l, acc = carry
            r0 = tile_start(kt)
            lg = jnp.dot(k_ref[g, pl.ds(r0, tk), :], q_h, preferred_element_type=F32)
            valid = key_ref[pl.ds(r0, tk), :] >= thr
            if diag:
                valid = jnp.logical_and(valid, r0 + k_iota <= q_pos)
            lg = jnp.where(valid, lg, NEG)
            m_new = jnp.maximum(m, jnp.max(lg, axis=0, keepdims=True))
            p = jnp.exp(lg - m_new)
            alpha = jnp.exp(m - m_new)
            l = alpha * l + jnp.sum(p, axis=0, keepdims=True)
            v_t = vt_ref[g * HEAD_DIM:(g + 1) * HEAD_DIM, pl.ds(r0, tk)]
            acc = alpha * acc + jnp.dot(v_t, p.astype(BF16), preferred_element_type=F32)
            return m_new, l, acc

        carry = (jnp.full((1, tq), NEG, F32), jnp.zeros((1, tq), F32), jnp.zeros((HEAD_DIM, tq), F32))
        carry = lax.fori_loop(0, n_full, functools.partial(att_tile, diag=False), carry)
        for dd in range(n_diag):
            carry = att_tile(n_full + dd, carry, True)
        _, l, acc = carry
        outs.append(acc / l)
    y_ref[...] = jnp.concatenate(outs, axis=0).T.astype(BF16)


def _attention(qt, qit, wit, k, ki, vt, batch, seq_len):
    n = ki.shape[0]
    tq = AT_TQ
    nq = seq_len // tq
    n_sel = min(TOPK, seq_len // 4)
    qcol = lambda b, j: (0, b * nq + j)
    return pl.pallas_call(
        functools.partial(_attn_kernel, seq_len=seq_len, n_sel=n_sel),
        grid=(batch, nq),
        in_specs=[
            pl.BlockSpec((D_ATTN, tq), qcol),
            pl.BlockSpec((IDX_HEADS * IDX_DIM, tq), qcol),
            pl.BlockSpec((IDX_HEADS, tq), qcol),
            pl.BlockSpec((N_KV_HEADS, seq_len, HEAD_DIM), lambda b, j: (0, b, 0)),
            pl.BlockSpec((seq_len, IDX_DIM), lambda b, j: (b, 0)),
            pl.BlockSpec((D_KV, seq_len), lambda b, j: (0, b)),
        ],
        out_specs=pl.BlockSpec((tq, D_ATTN), lambda b, j: (b * nq + j, 0)),
        out_shape=jax.ShapeDtypeStruct((n, D_ATTN), BF16),
        scratch_shapes=[pltpu.VMEM((seq_len, tq), I32)],
        compiler_params=_cparams(("arbitrary", "arbitrary")),
        name="attn",
    )(qt, qit, wit, k, ki, vt)


MIX_TM = 512


def _split_bf16(a):
    hi = a.astype(BF16)
    return hi, (a - hi.astype(F32)).astype(BF16)


def _mix_kernel(x_ref, ys_ref, ya_ref, gs_ref, ga_ref, wps_ref, wpa_ref, wo_ref, g2_ref,
                wr_hi_ref, wr_lo_ref, br_ref, x1_ref, h_ref, gates_ref):
    ps = jnp.dot(ys_ref[...], wps_ref[...], preferred_element_type=F32)
    pa = jnp.dot(ya_ref[...], wpa_ref[...], preferred_element_type=F32)
    mixed = gs_ref[...].astype(F32) * ps + ga_ref[...].astype(F32) * pa
    x1 = x_ref[...] + jnp.dot(mixed.astype(BF16), wo_ref[...], preferred_element_type=F32)
    x1_ref[...] = x1
    h = x1 * lax.rsqrt(jnp.mean(x1 * x1, axis=-1, keepdims=True) + RMS_EPS) * g2_ref[...]
    h_ref[...] = h
    h_hi, h_lo = _split_bf16(h)
    logits = (jnp.dot(h_hi, wr_hi_ref[...], preferred_element_type=F32)
              + jnp.dot(h_hi, wr_lo_ref[...], preferred_element_type=F32)
              + jnp.dot(h_lo, wr_hi_ref[...], preferred_element_type=F32)) + br_ref[...]
    lane = lax.broadcasted_iota(I32, logits.shape, 1)
    rest = logits
    picks, vals = [], []
    for _ in range(TOP_K_EXPERTS):
        m = jnp.max(rest, axis=-1, keepdims=True)
        first = jnp.min(jnp.where(rest == m, lane, N_EXPERTS), axis=-1, keepdims=True)
        pick = lane == first
        picks.append(pick)
        vals.append(m)
        rest = jnp.where(pick, -jnp.inf, rest)
    es = [jnp.exp(v - vals[0]) for v in vals]
    denom = es[0] + es[1] + es[2] + es[3]
    gates = jnp.zeros_like(logits)
    for pick, e in zip(picks, es):
        gates = gates + jnp.where(pick, e / denom, 0.0)
    gates_ref[...] = gates


def _mix(x2, y_ssm, y_att, gs, ga, w_proj_ssm, w_proj_attn, w_out, norm2_g, w_router, b_router):
    n = x2.shape[0]
    tm = min(MIX_TM, n)
    row = lambda i: (i, 0)
    wr_hi, wr_lo = _split_bf16(w_router)
    consts = [w_proj_ssm.astype(BF16), w_proj_attn.astype(BF16), w_out.astype(BF16),
              norm2_g.reshape(1, D_MODEL), wr_hi, wr_lo, b_router.reshape(1, N_EXPERTS)]
    return pl.pallas_call(
        _mix_kernel,
        grid=(n // tm,),
        in_specs=[pl.BlockSpec((tm, D_MODEL), row), pl.BlockSpec((tm, D_SSM), row),
                  pl.BlockSpec((tm, D_ATTN), row), pl.BlockSpec((tm, D_MODEL), row),
                  pl.BlockSpec((tm, D_MODEL), row)] + [_const_spec(c.shape) for c in consts],
        out_specs=(pl.BlockSpec((tm, D_MODEL), row), pl.BlockSpec((tm, D_MODEL), row),
                   pl.BlockSpec((tm, N_EXPERTS), row)),
        out_shape=(jax.ShapeDtypeStruct((n, D_MODEL), F32), jax.ShapeDtypeStruct((n, D_MODEL), F32),
                   jax.ShapeDtypeStruct((n, N_EXPERTS), F32)),
        compiler_params=_cparams(("arbitrary",)),
        name="mix",
    )(x2, y_ssm, y_att, gs, ga, *consts)


MOE_TT = 2048
MOE_RB = 256
ROW_CHUNKS = D_MODEL // LANES
assert ROW_CHUNKS == SUBLANES
MOE_PITCH = MOE_RB + SUBLANES


def _moe_kernel(cnt_ref, off_ref, tok_ref, gate_ref, h_ref, x1_ref, win_ref, bin_ref, wout_ref, bout_ref,
                gf_ref, out_ref, xt_ref, yt_ref):
    i, e = pl.program_id(0), pl.program_id(1)
    n = cnt_ref[i * N_EXPERTS + e]
    start = off_ref[i * N_EXPERTS + e]

    @pl.when(e == 0)
    def _():
        out_ref[...] = x1_ref[...]

    def slab(t):
        return pl.ds(pl.multiple_of(t * SUBLANES, SUBLANES), SUBLANES)

    def block(b, c):
        base = start + b * MOE_RB

        def gather8(r8, c2):
            for rr in range(SUBLANES):
                r = r8 * SUBLANES + rr
                t = tok_ref[0, 0, base + r]
                xt_ref[pl.ds(r, SUBLANES, stride=MOE_PITCH), :] = h_ref[slab(t), :]
            return c2

        lax.fori_loop(0, MOE_RB // SUBLANES, gather8, 0)
        xg = jnp.concatenate([xt_ref[j * MOE_PITCH:j * MOE_PITCH + MOE_RB, :] for j in range(ROW_CHUNKS)],
                             axis=1).astype(BF16)
        z = jnp.dot(xg, win_ref[0], preferred_element_type=F32) + bin_ref[0]
        zg = jnp.minimum(z[:, :D_EXPERT], SWIGLU_LIMIT)
        zl = jnp.clip(z[:, D_EXPERT:], -SWIGLU_LIMIT, SWIGLU_LIMIT)
        act = zg * jax.nn.sigmoid(SWIGLU_ALPHA * zg) * (zl + 1.0)
        y = jnp.dot(act.astype(BF16), wout_ref[0], preferred_element_type=F32) + bout_ref[0]
        for j in range(ROW_CHUNKS):
            yt_ref[j * MOE_PITCH:j * MOE_PITCH + MOE_RB, :] = y[:, j * LANES:(j + 1) * LANES]

        def scatter(r, c2):
            t = tok_ref[0, 0, base + r]
            g = gate_ref[0, 0, base + r]
            out_ref[slab(t), :] = out_ref[slab(t), :] + g * yt_ref[pl.ds(r, SUBLANES, stride=MOE_PITCH), :]
            return c2

        lax.fori_loop(0, jnp.minimum(MOE_RB, n - b * MOE_RB), scatter, 0)
        return c

    lax.fori_loop(0, (n + MOE_RB - 1) // MOE_RB, block, 0)

    @pl.when(e == N_EXPERTS - 1)
    def _():
        group = 256
        gf = gf_ref[...]

        def norm(c, c2):
            rows = pl.ds(pl.multiple_of(c * group * SUBLANES, group * SUBLANES), group * SUBLANES)
            v = out_ref[rows, :].reshape(group, SUBLANES, LANES)
            ss = jnp.sum(jnp.sum(v * v, axis=2, keepdims=True), axis=1, keepdims=True)
            o = v * lax.rsqrt(ss * (1.0 / D_MODEL) + RMS_EPS) * gf
            out_ref[rows, :] = o.reshape(group * SUBLANES, LANES)
            return c2

        lax.fori_loop(0, out_ref.shape[0] // (group * SUBLANES), norm, 0)


def _moe(h, x1, gates, w_moe_in, b_moe_in, w_moe_out, b_moe_out, norm_f_g):
    n = h.shape[0]
    tt = min(MOE_TT, n)
    n_tiles = n // tt
    cap = TOP_K_EXPERTS * tt + MOE_RB

    gt = jnp.transpose(gates.reshape(n_tiles, tt, N_EXPERTS), (0, 2, 1))
    sel = gt > 0.0
    cnt = jnp.sum(sel, axis=2).astype(I32)
    off = jnp.cumsum(cnt, axis=1) - cnt
    flat_sel = sel.reshape(n_tiles, N_EXPERTS * tt)
    order = jnp.argsort(jnp.logical_not(flat_sel), axis=1, stable=True)[:, :cap]
    tok = (order % tt).astype(I32).reshape(n_tiles, 1, cap)
    gate = jnp.take_along_axis(gt.reshape(n_tiles, N_EXPERTS * tt), order, axis=1).reshape(n_tiles, 1, cap)

    win = jnp.concatenate([w_moe_in[..., 0::2], w_moe_in[..., 1::2]], axis=-1).astype(BF16)
    bin_ = jnp.concatenate([b_moe_in[..., 0::2], b_moe_in[..., 1::2]], axis=-1).reshape(N_EXPERTS, 1, 2 * D_EXPERT)
    wout = w_moe_out.astype(BF16)
    bout = b_moe_out.reshape(N_EXPERTS, 1, D_MODEL)

    slabs = lambda a: a.reshape(n * ROW_CHUNKS, LANES)
    tile = pl.BlockSpec((tt * ROW_CHUNKS, LANES), lambda i, e, *_: (i, 0))
    tile1 = pl.BlockSpec((tt * ROW_CHUNKS, LANES), lambda i, e, *_: (i, 0), pipeline_mode=pl.Buffered(1))
    smem_list = pl.BlockSpec((1, 1, cap), lambda i, e, *_: (i, 0, 0), memory_space=pltpu.SMEM)
    per_expert = lambda shape: pl.BlockSpec((1,) + shape, lambda i, e, *_: (e, 0, 0))
    grid_spec = pltpu.PrefetchScalarGridSpec(
        num_scalar_prefetch=2,
        grid=(n_tiles, N_EXPERTS),
        in_specs=[smem_list, smem_list, tile1, tile1,
                  per_expert((D_MODEL, 2 * D_EXPERT)), per_expert((1, 2 * D_EXPERT)),
                  per_expert((D_EXPERT, D_MODEL)), per_expert((1, D_MODEL)),
                  pl.BlockSpec((SUBLANES, LANES), lambda i, e, *_: (0, 0))],
        out_specs=tile,
        scratch_shapes=[pltpu.VMEM((ROW_CHUNKS * MOE_PITCH, LANES), F32),
                        pltpu.VMEM((ROW_CHUNKS * MOE_PITCH, LANES), F32)],
    )
    out = pl.pallas_call(
        _moe_kernel,
        grid_spec=grid_spec,
        out_shape=jax.ShapeDtypeStruct((n * ROW_CHUNKS, LANES), F32),
        compiler_params=_cparams(("arbitrary", "arbitrary")),
        name="moe",
    )(cnt.reshape(-1), off.reshape(-1).astype(I32), tok, gate, slabs(h), slabs(x1), win, bin_, wout, bout,
      norm_f_g.reshape(SUBLANES, LANES))
    return out.reshape(n, D_MODEL)


def kernel(x, norm1_g, w_in, b_gate, ssm_lam_re, ssm_lam_im, ssm_log_dt, ssm_b_re, ssm_b_im, ssm_c_re, ssm_c_im, ssm_d, w_glu, b_glu, w_proj_ssm, w_proj_attn, w_out, norm2_g, w_router, b_router, w_moe_in, b_moe_in, w_moe_out, b_moe_out, norm_f_g):
    bsz, seq_len, _ = x.shape
    n = bsz * seq_len
    x2 = x.reshape(n, D_MODEL)
    u, qt, k, vt, qit, ki, wit, gs, ga = _in_proj(x2, norm1_g[0], w_in[0], b_gate[0], seq_len)
    y_ssm = _s5(u, bsz, seq_len, ssm_lam_re[0], ssm_lam_im[0], ssm_log_dt[0], ssm_b_re[0], ssm_b_im[0],
                ssm_c_re[0], ssm_c_im[0], ssm_d[0], w_glu[0], b_glu[0])
    y_att = _attention(qt, qit, wit, k, ki, vt, bsz, seq_len)
    x1, h, gates = _mix(x2, y_ssm, y_att, gs, ga, w_proj_ssm[0], w_proj_attn[0], w_out[0], norm2_g[0],
                        w_router[0], b_router[0])
    out = _moe(h, x1, gates, w_moe_in[0], b_moe_in[0], w_moe_out[0], b_moe_out[0], norm_f_g)
    return out.reshape(x.shape)
```

```python
import functools
import math

import numpy as np
import jax
import jax.numpy as jnp
from jax import lax
from jax.experimental import pallas as pl
from jax.experimental.pallas import tpu as pltpu

D_MODEL = 1024
D_SSM = 512
SSM_GROUP = 16
N_SSM_GROUPS = 32
SSM_STATE = 64
N_HEADS = 8
N_KV_HEADS = 2
HEAD_DIM = 64
D_ATTN = N_HEADS * HEAD_DIM
D_KV = N_KV_HEADS * HEAD_DIM
IDX_HEADS = 16
IDX_DIM = 64
TOPK = 256
ROPE_THETA = 10000.0
N_EXPERTS = 32
TOP_K_EXPERTS = 4
D_EXPERT = D_MODEL
SWIGLU_LIMIT = 7.0
SWIGLU_ALPHA = 1.702
RMS_EPS = 1e-5
NEG = -1e30

LANES = 128
SUBLANES = 8
VMEM_LIMIT = 56 * 1024 * 1024

F32 = jnp.float32
BF16 = jnp.bfloat16
I32 = jnp.int32


def _cparams(sem):
    return pltpu.CompilerParams(dimension_semantics=sem, vmem_limit_bytes=VMEM_LIMIT)


def _const_spec(shape):
    nd = len(shape)
    return pl.BlockSpec(shape, lambda *_: (0,) * nd)


IN_TM = 512


def _in_proj_kernel(x_ref, g_ref, cos_ref, sin_ref, bg_ref,
                    wu_ref, wq_ref, wqr_ref, wkv_ref, wqi_ref, wqir_ref, wkw_ref, wgs_ref, wga_ref,
                    u_ref, qt_ref, k_ref, vt_ref, qit_ref, ki_ref, wit_ref, gs_ref, ga_ref):
    x = x_ref[...]
    xn = x * lax.rsqrt(jnp.mean(x * x, axis=-1, keepdims=True) + RMS_EPS) * g_ref[...]
    xb = xn.astype(BF16)

    def mm(w_ref):
        return jnp.dot(xb, w_ref[...], preferred_element_type=F32)

    cos = cos_ref[...]
    sin = sin_ref[...]

    def rope(a, ar):
        reps = a.shape[1] // LANES
        return a * jnp.tile(cos, (1, reps)) + ar * jnp.tile(sin, (1, reps))

    u_ref[...] = mm(wu_ref)
    q = rope(mm(wq_ref), mm(wqr_ref)) * (HEAD_DIM ** -0.5)
    qt_ref[...] = q.T.astype(BF16)
    kv = mm(wkv_ref)
    k = rope(kv[:, :D_KV], kv[:, D_KV:2 * D_KV])
    for g in range(N_KV_HEADS):
        k_ref[g] = k[:, g * HEAD_DIM:(g + 1) * HEAD_DIM].astype(BF16)
    vt_ref[...] = kv[:, 2 * D_KV:].T.astype(BF16)
    qi = rope(mm(wqi_ref), mm(wqir_ref)) * (IDX_DIM ** -0.5)
    qit_ref[...] = qi.T.astype(BF16)
    kw = mm(wkw_ref)
    ki = kw[:, :IDX_DIM] * cos[:, :IDX_DIM] + kw[:, IDX_DIM:2 * IDX_DIM] * sin[:, :IDX_DIM]
    ki_ref[...] = ki.astype(BF16)
    wit_ref[...] = (kw[:, LANES:2 * LANES].T)[:IDX_HEADS] * (IDX_HEADS ** -0.5)
    gs_ref[...] = jax.nn.sigmoid(mm(wgs_ref) + bg_ref[0:1, :]).astype(BF16)
    ga_ref[...] = jax.nn.sigmoid(mm(wga_ref) + bg_ref[1:2, :]).astype(BF16)


def _rot_half_cols(w, n_heads, d):
    k = w.shape[0]
    w4 = w.reshape(k, n_heads, 2, d // 2)
    return w4[:, :, ::-1, :].reshape(k, n_heads * d)


def _in_proj(x2, norm1_g, w_in, b_gate, seq_len):
    n = x2.shape[0]
    tm = min(IN_TM, seq_len)
    o = 0
    parts = []
    for width in (D_SSM, D_ATTN, D_KV, D_KV, IDX_HEADS * IDX_DIM, IDX_DIM, IDX_HEADS, D_MODEL, D_MODEL):
        parts.append(w_in[:, o:o + width])
        o += width
    wu, wq, wk, wv, wqi, wki, wwi, wgs, wga = parts
    wqr = _rot_half_cols(wq, N_HEADS, HEAD_DIM)
    wkr = _rot_half_cols(wk, N_KV_HEADS, HEAD_DIM)
    wqir = _rot_half_cols(wqi, IDX_HEADS, IDX_DIM)
    wkir = _rot_half_cols(wki, 1, IDX_DIM)
    wkv = jnp.concatenate([wk, wkr, wv], axis=1)
    wkw = jnp.concatenate([wki, wkir, wwi, jnp.zeros((D_MODEL, LANES - IDX_HEADS), F32)], axis=1)
    weights = [w.astype(BF16) for w in (wu, wq, wqr, wkv, wqi, wqir, wkw, wgs, wga)]

    half = HEAD_DIM // 2
    inv = ROPE_THETA ** (-jnp.arange(half, dtype=F32) / half)
    ang = jnp.arange(seq_len, dtype=F32)[:, None] * inv[None, :]
    cos = jnp.tile(jnp.cos(ang), (1, 4))
    sin = jnp.tile(jnp.concatenate([-jnp.sin(ang), jnp.sin(ang)], axis=1), (1, 2))

    tiles_per_seq = seq_len // tm
    row = lambda i: (i, 0)
    col = lambda i: (0, i)
    out_shapes = (
        jax.ShapeDtypeStruct((n, D_SSM), F32),
        jax.ShapeDtypeStruct((D_ATTN, n), BF16),
        jax.ShapeDtypeStruct((N_KV_HEADS, n, HEAD_DIM), BF16),
        jax.ShapeDtypeStruct((D_KV, n), BF16),
        jax.ShapeDtypeStruct((IDX_HEADS * IDX_DIM, n), BF16),
        jax.ShapeDtypeStruct((n, IDX_DIM), BF16),
        jax.ShapeDtypeStruct((IDX_HEADS, n), F32),
        jax.ShapeDtypeStruct((n, D_MODEL), BF16),
        jax.ShapeDtypeStruct((n, D_MODEL), BF16),
    )
    out_specs = (
        pl.BlockSpec((tm, D_SSM), row),
        pl.BlockSpec((D_ATTN, tm), col),
        pl.BlockSpec((N_KV_HEADS, tm, HEAD_DIM), lambda i: (0, i, 0)),
        pl.BlockSpec((D_KV, tm), col),
        pl.BlockSpec((IDX_HEADS * IDX_DIM, tm), col),
        pl.BlockSpec((tm, IDX_DIM), row),
        pl.BlockSpec((IDX_HEADS, tm), col),
        pl.BlockSpec((tm, D_MODEL), row),
        pl.BlockSpec((tm, D_MODEL), row),
    )
    in_specs = [
        pl.BlockSpec((tm, D_MODEL), row),
        _const_spec((1, D_MODEL)),
        pl.BlockSpec((tm, LANES), lambda i: (i % tiles_per_seq, 0)),
        pl.BlockSpec((tm, LANES), lambda i: (i % tiles_per_seq, 0)),
        _const_spec((2, D_MODEL)),
    ] + [_const_spec(w.shape) for w in weights]
    return pl.pallas_call(
        _in_proj_kernel,
        grid=(n // tm,),
        in_specs=in_specs,
        out_specs=out_specs,
        out_shape=out_shapes,
        compiler_params=_cparams(("arbitrary",)),
        name="in_proj",
    )(x2, norm1_g.reshape(1, D_MODEL), cos, sin, b_gate, *weights)


S5_T = 512
S5_CHUNK = 512
N_STATE = N_SSM_GROUPS * SSM_STATE
S5_HALF_IN = D_SSM // 2
S5_HALF_ST = N_STATE // 2


def _cmul(ar, ai, br, bi):
    return ar * br - ai * bi, ar * bi + ai * br


def _s5_kernel(u_ref, lr_ref, li_ref, ldt_ref, bre_ref, bim_ref, cre_ref, cim_ref, d_ref,
               wglu_ref, bglu_ref, y_ref,
               bb_ref, cc_ref, pw_re_ref, pw_im_ref, st_re_ref, st_im_ref, carry_re_ref, carry_im_ref):
    first = jnp.logical_and(pl.program_id(0) == 0, pl.program_id(1) == 0)

    @pl.when(first)
    def _():
        lr, li = lr_ref[...], li_ref[...]
        dt = jnp.exp(ldt_ref[...])
        mag = jnp.exp(lr * dt)
        ar = mag * jnp.cos(li * dt)
        ai = mag * jnp.sin(li * dt)
        den = lr * lr + li * li
        zr = ((ar - 1.0) * lr + ai * li) / den
        zi = (ai * lr - (ar - 1.0) * li) / den
        for h in range(2):
            rows = slice(h * S5_HALF_IN, (h + 1) * S5_HALF_IN)
            cols = slice(h * S5_HALF_ST, (h + 1) * S5_HALF_ST)
            bre, bim = bre_ref[rows, cols], bim_ref[rows, cols]
            bb_ref[h, :, :S5_HALF_ST] = (zr[:, cols] * bre - zi[:, cols] * bim).astype(BF16)
            bb_ref[h, :, S5_HALF_ST:] = (zr[:, cols] * bim + zi[:, cols] * bre).astype(BF16)
            cc_ref[h, :S5_HALF_ST, :] = cre_ref[cols, rows].astype(BF16)
            cc_ref[h, S5_HALF_ST:, :] = (-cim_ref[cols, rows]).astype(BF16)
        pr, pi = ar, ai
        pw_re_ref[0:1, :] = pr
        pw_im_ref[0:1, :] = pi
        for n in range(1, SUBLANES):
            pr, pi = _cmul(pr, pi, ar, ai)
            pw_re_ref[n:n + 1, :] = pr
            pw_im_ref[n:n + 1, :] = pi

    @pl.when(pl.program_id(1) == 0)
    def _():
        carry_re_ref[...] = jnp.zeros_like(carry_re_ref)
        carry_im_ref[...] = jnp.zeros_like(carry_im_ref)

    u = u_ref[...]
    ub = u.astype(BF16)
    for h in range(2):
        bu = jnp.dot(ub[:, h * S5_HALF_IN:(h + 1) * S5_HALF_IN], bb_ref[h], preferred_element_type=F32)
        st_re_ref[:, h * S5_HALF_ST:(h + 1) * S5_HALF_ST] = bu[:, :S5_HALF_ST]
        st_im_ref[:, h * S5_HALF_ST:(h + 1) * S5_HALF_ST] = bu[:, S5_HALF_ST:]

    t_len = u.shape[0]
    row = lax.broadcasted_iota(I32, (SUBLANES, S5_CHUNK), 0)
    for c in range(N_STATE // S5_CHUNK):
        lanes = slice(c * S5_CHUNK, (c + 1) * S5_CHUNK)
        p_re, p_im = pw_re_ref[:, lanes], pw_im_ref[:, lanes]
        steps = []
        for d in (1, 2, 4):
            a_re = jnp.where(row >= d, jnp.broadcast_to(p_re[d - 1:d, :], row.shape), 0.0)
            a_im = jnp.where(row >= d, jnp.broadcast_to(p_im[d - 1:d, :], row.shape), 0.0)
            steps.append((d, a_re, a_im))

        def block(i, carry):
            c_re, c_im = carry
            r0 = pl.multiple_of(i * SUBLANES, SUBLANES)
            xr = st_re_ref[pl.ds(r0, SUBLANES), lanes]
            xi = st_im_ref[pl.ds(r0, SUBLANES), lanes]
            for d, a_re, a_im in steps:
                sr = pltpu.roll(xr, d, 0)
                si = pltpu.roll(xi, d, 0)
                tr, ti = _cmul(a_re, a_im, sr, si)
                xr, xi = xr + tr, xi + ti
            tr, ti = _cmul(p_re, p_im, jnp.broadcast_to(c_re, xr.shape), jnp.broadcast_to(c_im, xi.shape))
            xr, xi = xr + tr, xi + ti
            st_re_ref[pl.ds(r0, SUBLANES), lanes] = xr
            st_im_ref[pl.ds(r0, SUBLANES), lanes] = xi
            return xr[SUBLANES - 1:, :], xi[SUBLANES - 1:, :]

        def two_blocks(j, carry):
            return block(2 * j + 1, block(2 * j, carry))

        c_re, c_im = lax.fori_loop(0, t_len // (2 * SUBLANES), two_blocks,
                                   (carry_re_ref[:, lanes], carry_im_ref[:, lanes]))
        carry_re_ref[:, lanes] = c_re
        carry_im_ref[:, lanes] = c_im

    ys = []
    for h in range(2):
        cols = slice(h * S5_HALF_ST, (h + 1) * S5_HALF_ST)
        xs = jnp.concatenate([st_re_ref[:, cols], st_im_ref[:, cols]], axis=1).astype(BF16)
        ys.append(jnp.dot(xs, cc_ref[h], preferred_element_type=F32))
    y = jnp.concatenate(ys, axis=1) + d_ref[...] * u
    y = jax.nn.gelu(y)
    gate = jnp.dot(y.astype(BF16), wglu_ref[...], preferred_element_type=F32) + bglu_ref[...]
    y_ref[...] = (y * jax.nn.sigmoid(gate)).astype(BF16)


def _s5(u, batch, seq_len, lam_re, lam_im, log_dt, b_re, b_im, c_re, c_im, d_skip, w_glu, b_glu):
    n = u.shape[0]
    t = min(S5_T, seq_len)
    g, p, hh = N_SSM_GROUPS, SSM_STATE, SSM_GROUP
    eye = jnp.eye(g, dtype=F32)
    bre_bd = jnp.einsum("gph,gk->ghkp", b_re, eye).reshape(D_SSM, N_STATE)
    bim_bd = jnp.einsum("gph,gk->ghkp", b_im, eye).reshape(D_SSM, N_STATE)
    cre_bd = jnp.einsum("ghp,gk->gpkh", c_re, eye).reshape(N_STATE, D_SSM)
    cim_bd = jnp.einsum("ghp,gk->gpkh", c_im, eye).reshape(N_STATE, D_SSM)
    lr = lam_re.reshape(1, N_STATE)
    li = lam_im.reshape(1, N_STATE)
    ldt = jnp.repeat(log_dt, p).reshape(1, N_STATE)
    tiles = seq_len // t
    consts = [lr, li, ldt, bre_bd, bim_bd, cre_bd, cim_bd, d_skip.reshape(1, D_SSM),
              w_glu.astype(BF16), b_glu.reshape(1, D_SSM)]
    return pl.pallas_call(
        _s5_kernel,
        grid=(batch, tiles),
        in_specs=[pl.BlockSpec((t, D_SSM), lambda b, i: (b * tiles + i, 0))]
        + [_const_spec(c.shape) for c in consts],
        out_specs=pl.BlockSpec((t, D_SSM), lambda b, i: (b * tiles + i, 0)),
        out_shape=jax.ShapeDtypeStruct((n, D_SSM), BF16),
        scratch_shapes=[
            pltpu.VMEM((2, S5_HALF_IN, 2 * S5_HALF_ST), BF16),
            pltpu.VMEM((2, 2 * S5_HALF_ST, S5_HALF_IN), BF16),
            pltpu.VMEM((SUBLANES, N_STATE), F32),
            pltpu.VMEM((SUBLANES, N_STATE), F32),
            pltpu.VMEM((t, N_STATE), F32),
            pltpu.VMEM((t, N_STATE), F32),
            pltpu.VMEM((1, N_STATE), F32),
            pltpu.VMEM((1, N_STATE), F32),
        ],
        compiler_params=_cparams(("arbitrary", "arbitrary")),
        name="s5",
    )(u, *consts)


AT_TQ = 256
AT_TK = 128
INT_MIN = -2 ** 31
KEY_NEG = int(np.float32(NEG).view(np.int32)) ^ 0x7FFFFFFF


def _attn_kernel(qt_ref, qit_ref, wit_ref, k_ref, ki_ref, vt_ref, y_ref, key_ref, *, seq_len, n_sel):
    j = pl.program_id(1)
    tq, tk = AT_TQ, AT_TK
    n_diag = tq // tk
    n_full = j * n_diag
    q_pos = j * tq + lax.broadcasted_iota(I32, (tk, tq), 1)
    k_iota = lax.broadcasted_iota(I32, (tk, tq), 0)

    def tile_start(kt):
        return pl.multiple_of(kt * tk, tk)

    def score_tile(kt, diag):
        r0 = tile_start(kt)
        ki_t = ki_ref[pl.ds(r0, tk), :]
        acc = jnp.zeros((tk, tq), F32)
        for h in range(IDX_HEADS):
            s = jnp.dot(ki_t, qit_ref[h * IDX_DIM:(h + 1) * IDX_DIM, :], preferred_element_type=F32)
            acc = acc + wit_ref[h:h + 1, :] * jnp.maximum(s, 0.0)
        if diag:
            acc = jnp.where(r0 + k_iota <= q_pos, acc, NEG)
        bits = lax.bitcast_convert_type(acc, I32)
        key_ref[pl.ds(r0, tk), :] = jnp.where(bits < 0, bits ^ 0x7FFFFFFF, bits)

    def score_body(kt, c):
        score_tile(kt, False)
        return c

    lax.fori_loop(0, n_full, score_body, 0)
    for dd in range(n_diag):
        score_tile(n_full + dd, True)

    n_beyond = seq_len - (j + 1) * tq

    def count_tile(c, cnt, cand):
        r0 = pl.multiple_of(c * tq, tq)
        ge = jnp.where(key_ref[pl.ds(r0, tq), :] >= cand, 1, 0)
        rows = tq // 4
        parts = [ge[i * rows:(i + 1) * rows].reshape(rows // SUBLANES, SUBLANES, tq).sum(axis=0)
                 for i in range(4)]
        return cnt + ((parts[0] + parts[1]) + (parts[2] + parts[3]))

    def bit_step(b, thr_u):
        cand_u = thr_u | lax.shift_left(jnp.int32(1), 31 - b)
        cand = cand_u ^ INT_MIN
        cnt = lax.fori_loop(0, j + 1, functools.partial(count_tile, cand=cand),
                            jnp.zeros((SUBLANES, tq), I32))
        total = jnp.sum(cnt, axis=0, keepdims=True) + jnp.where(cand <= KEY_NEG, n_beyond, 0)
        return jnp.where(total >= n_sel, cand_u, thr_u)

    thr = lax.fori_loop(0, 32, bit_step, jnp.zeros((1, tq), I32)) ^ INT_MIN

    outs = []
    for h in range(N_HEADS):
        g = h // (N_HEADS // N_KV_HEADS)
        q_h = qt_ref[h * HEAD_DIM:(h + 1) * HEAD_DIM, :]

        def att_tile(kt, carry, diag):
            m, l, acc = carry
            r0 = tile_start(kt)
            lg = jnp.dot(k_ref[g, pl.ds(r0, tk), :], q_h, preferred_element_type=F32)
            valid = key_ref[pl.ds(r0, tk), :] >= thr
            if diag:
                valid = jnp.logical_and(valid, r0 + k_iota <= q_pos)
            lg = jnp.where(valid, lg, NEG)
            m_new = jnp.maximum(m, jnp.max(lg, axis=0, keepdims=True))
            p = jnp.exp(lg - m_new)
            alpha = jnp.exp(m - m_new)
            l = alpha * l + jnp.sum(p, axis=0, keepdims=True)
            v_t = vt_ref[g * HEAD_DIM:(g + 1) * HEAD_DIM, pl.ds(r0, tk)]
            acc = alpha * acc + jnp.dot(v_t, p.astype(BF16), preferred_element_type=F32)
            return m_new, l, acc

        carry = (jnp.full((1, tq), NEG, F32), jnp.zeros((1, tq), F32), jnp.zeros((HEAD_DIM, tq), F32))
        carry = lax.fori_loop(0, n_full, functools.partial(att_tile, diag=False), carry)
        for dd in range(n_diag):
            carry = att_tile(n_full + dd, carry, True)
        _, l, acc = carry
        outs.append(acc / l)
    y_ref[...] = jnp.concatenate(outs, axis=0).T.astype(BF16)


def _attention(qt, qit, wit, k, ki, vt, batch, seq_len):
    n = ki.shape[0]
    tq = AT_TQ
    nq = seq_len // tq
    n_sel = min(TOPK, seq_len // 4)
    qcol = lambda b, j: (0, b * nq + j)
    return pl.pallas_call(
        functools.partial(_attn_kernel, seq_len=seq_len, n_sel=n_sel),
        grid=(batch, nq),
        in_specs=[
            pl.BlockSpec((D_ATTN, tq), qcol),
            pl.BlockSpec((IDX_HEADS * IDX_DIM, tq), qcol),
            pl.BlockSpec((IDX_HEADS, tq), qcol),
            pl.BlockSpec((N_KV_HEADS, seq_len, HEAD_DIM), lambda b, j: (0, b, 0)),
            pl.BlockSpec((seq_len, IDX_DIM), lambda b, j: (b, 0)),
            pl.BlockSpec((D_KV, seq_len), lambda b, j: (0, b)),
        ],
        out_specs=pl.BlockSpec((tq, D_ATTN), lambda b, j: (b * nq + j, 0)),
        out_shape=jax.ShapeDtypeStruct((n, D_ATTN), BF16),
        scratch_shapes=[pltpu.VMEM((seq_len, tq), I32)],
        compiler_params=_cparams(("arbitrary", "arbitrary")),
        name="attn",
    )(qt, qit, wit, k, ki, vt)


MIX_TM = 512


def _split_bf16(a):
    hi = a.astype(BF16)
    return hi, (a - hi.astype(F32)).astype(BF16)


def _mix_kernel(x_ref, ys_ref, ya_ref, gs_ref, ga_ref, wps_ref, wpa_ref, wo_ref, g2_ref,
                wr_hi_ref, wr_lo_ref, br_ref, x1_ref, h_ref, gates_ref):
    ps = jnp.dot(ys_ref[...], wps_ref[...], preferred_element_type=F32)
    pa = jnp.dot(ya_ref[...], wpa_ref[...], preferred_element_type=F32)
    mixed = gs_ref[...].astype(F32) * ps + ga_ref[...].astype(F32) * pa
    x1 = x_ref[...] + jnp.dot(mixed.astype(BF16), wo_ref[...], preferred_element_type=F32)
    x1_ref[...] = x1
    h = x1 * lax.rsqrt(jnp.mean(x1 * x1, axis=-1, keepdims=True) + RMS_EPS) * g2_ref[...]
    h_ref[...] = h
    h_hi, h_lo = _split_bf16(h)
    logits = (jnp.dot(h_hi, wr_hi_ref[...], preferred_element_type=F32)
              + jnp.dot(h_hi, wr_lo_ref[...], preferred_element_type=F32)
              + jnp.dot(h_lo, wr_hi_ref[...], preferred_element_type=F32)) + br_ref[...]
    lane = lax.broadcasted_iota(I32, logits.shape, 1)
    rest = logits
    picks, vals = [], []
    for _ in range(TOP_K_EXPERTS):
        m = jnp.max(rest, axis=-1, keepdims=True)
        first = jnp.min(jnp.where(rest == m, lane, N_EXPERTS), axis=-1, keepdims=True)
        pick = lane == first
        picks.append(pick)
        vals.append(m)
        rest = jnp.where(pick, -jnp.inf, rest)
    es = [jnp.exp(v - vals[0]) for v in vals]
    denom = es[0] + es[1] + es[2] + es[3]
    gates = jnp.zeros_like(logits)
    for pick, e in zip(picks, es):
        gates = gates + jnp.where(pick, e / denom, 0.0)
    gates_ref[...] = gates


def _mix(x2, y_ssm, y_att, gs, ga, w_proj_ssm, w_proj_attn, w_out, norm2_g, w_router, b_router):
    n = x2.shape[0]
    tm = min(MIX_TM, n)
    row = lambda i: (i, 0)
    wr_hi, wr_lo = _split_bf16(w_router)
    consts = [w_proj_ssm.astype(BF16), w_proj_attn.astype(BF16), w_out.astype(BF16),
              norm2_g.reshape(1, D_MODEL), wr_hi, wr_lo, b_router.reshape(1, N_EXPERTS)]
    return pl.pallas_call(
        _mix_kernel,
        grid=(n // tm,),
        in_specs=[pl.BlockSpec((tm, D_MODEL), row), pl.BlockSpec((tm, D_SSM), row),
                  pl.BlockSpec((tm, D_ATTN), row), pl.BlockSpec((tm, D_MODEL), row),
                  pl.BlockSpec((tm, D_MODEL), row)] + [_const_spec(c.shape) for c in consts],
        out_specs=(pl.BlockSpec((tm, D_MODEL), row), pl.BlockSpec((tm, D_MODEL), row),
                   pl.BlockSpec((tm, N_EXPERTS), row)),
        out_shape=(jax.ShapeDtypeStruct((n, D_MODEL), F32), jax.ShapeDtypeStruct((n, D_MODEL), F32),
                   jax.ShapeDtypeStruct((n, N_EXPERTS), F32)),
        compiler_params=_cparams(("arbitrary",)),
        name="mix",
    )(x2, y_ssm, y_att, gs, ga, *consts)


MOE_TT = 2048
MOE_RB = 256
ROW_CHUNKS = D_MODEL // LANES
assert ROW_CHUNKS == SUBLANES
MOE_PITCH = MOE_RB + SUBLANES


def _moe_kernel(cnt_ref, off_ref, tok_ref, gate_ref, h_ref, x1_ref, win_ref, bin_ref, wout_ref, bout_ref,
                gf_ref, out_ref, xt_ref, yt_ref):
    i, e = pl.program_id(0), pl.program_id(1)
    n = cnt_ref[i * N_EXPERTS + e]
    start = off_ref[i * N_EXPERTS + e]

    @pl.when(e == 0)
    def _():
        out_ref[...] = x1_ref[...]

    def slab(t):
        return pl.ds(pl.multiple_of(t * SUBLANES, SUBLANES), SUBLANES)

    def block(b, c):
        base = start + b * MOE_RB

        def gather8(r8, c2):
            for rr in range(SUBLANES):
                r = r8 * SUBLANES + rr
                t = tok_ref[0, 0, base + r]
                xt_ref[pl.ds(r, SUBLANES, stride=MOE_PITCH), :] = h_ref[slab(t), :]
            return c2

        lax.fori_loop(0, MOE_RB // SUBLANES, gather8, 0)
        xg = jnp.concatenate([xt_ref[j * MOE_PITCH:j * MOE_PITCH + MOE_RB, :] for j in range(ROW_CHUNKS)],
                             axis=1).astype(BF16)
        z = jnp.dot(xg, win_ref[0], preferred_element_type=F32) + bin_ref[0]
        half = MXU_COLS // 2
        groups = range(2 * D_EXPERT // MXU_COLS)
        zg = jnp.concatenate([z[:, c * MXU_COLS:c * MXU_COLS + half] for c in groups], axis=1)
        zl = jnp.concatenate([z[:, c * MXU_COLS + half:(c + 1) * MXU_COLS] for c in groups], axis=1)
        zg = jnp.minimum(zg, SWIGLU_LIMIT)
        zl = jnp.clip(zl, -SWIGLU_LIMIT, SWIGLU_LIMIT)
        act = zg * jax.nn.sigmoid(SWIGLU_ALPHA * zg) * (zl + 1.0)
        y = jnp.dot(act.astype(BF16), wout_ref[0], preferred_element_type=F32) + bout_ref[0]
        for j in range(ROW_CHUNKS):
            yt_ref[j * MOE_PITCH:j * MOE_PITCH + MOE_RB, :] = y[:, j * LANES:(j + 1) * LANES]

        def scatter(r, c2):
            t = tok_ref[0, 0, base + r]
            g = gate_ref[0, 0, base + r]
            out_ref[slab(t), :] = out_ref[slab(t), :] + g * yt_ref[pl.ds(r, SUBLANES, stride=MOE_PITCH), :]
            return c2

        lax.fori_loop(0, jnp.minimum(MOE_RB, n - b * MOE_RB), scatter, 0)
        return c

    lax.fori_loop(0, (n + MOE_RB - 1) // MOE_RB, block, 0)

    @pl.when(e == N_EXPERTS - 1)
    def _():
        group = 256
        gf = gf_ref[...]

        def norm(c, c2):
            rows = pl.ds(pl.multiple_of(c * group * SUBLANES, group * SUBLANES), group * SUBLANES)
            v = out_ref[rows, :].reshape(group, SUBLANES, LANES)
            ss = jnp.sum(jnp.sum(v * v, axis=2, keepdims=True), axis=1, keepdims=True)
            o = v * lax.rsqrt(ss * (1.0 / D_MODEL) + RMS_EPS) * gf
            out_ref[rows, :] = o.reshape(group * SUBLANES, LANES)
            return c2

        lax.fori_loop(0, out_ref.shape[0] // (group * SUBLANES), norm, 0)


MXU_COLS = 256


def _expert_weights_kernel(win_ref, wout_ref, perm_ref, winp_ref, woutb_ref):
    w = win_ref[0].astype(BF16)
    for c in range(w.shape[1] // MXU_COLS):
        cols = slice(c * MXU_COLS, (c + 1) * MXU_COLS)
        winp_ref[0, :, cols] = jnp.dot(w[:, cols], perm_ref[...], preferred_element_type=F32).astype(BF16)
    woutb_ref[0] = wout_ref[0].astype(BF16)


def _expert_weights(w_moe_in, w_moe_out):
    half = MXU_COLS // 2
    src = jnp.concatenate([jnp.arange(half) * 2, jnp.arange(half) * 2 + 1])
    perm = (jnp.arange(MXU_COLS)[:, None] == src[None, :]).astype(BF16)
    per_expert = lambda a: pl.BlockSpec((1,) + a.shape[1:], lambda e: (e, 0, 0))
    return pl.pallas_call(
        _expert_weights_kernel,
        grid=(N_EXPERTS,),
        in_specs=[per_expert(w_moe_in), per_expert(w_moe_out), _const_spec(perm.shape)],
        out_specs=(per_expert(w_moe_in), per_expert(w_moe_out)),
        out_shape=(jax.ShapeDtypeStruct(w_moe_in.shape, BF16), jax.ShapeDtypeStruct(w_moe_out.shape, BF16)),
        compiler_params=_cparams(("arbitrary",)),
        name="expert_weights",
    )(w_moe_in, w_moe_out, perm)


def _moe(h, x1, gates, w_moe_in, b_moe_in, w_moe_out, b_moe_out, norm_f_g):
    n = h.shape[0]
    tt = min(MOE_TT, n)
    n_tiles = n // tt
    cap = TOP_K_EXPERTS * tt + MOE_RB

    gt = jnp.transpose(gates.reshape(n_tiles, tt, N_EXPERTS), (0, 2, 1))
    sel = gt > 0.0
    cnt = jnp.sum(sel, axis=2).astype(I32)
    off = jnp.cumsum(cnt, axis=1) - cnt
    flat_sel = sel.reshape(n_tiles, N_EXPERTS * tt)
    order = jnp.argsort(jnp.logical_not(flat_sel), axis=1, stable=True)[:, :cap]
    tok = (order % tt).astype(I32).reshape(n_tiles, 1, cap)
    gate = jnp.take_along_axis(gt.reshape(n_tiles, N_EXPERTS * tt), order, axis=1).reshape(n_tiles, 1, cap)

    win, wout = _expert_weights(w_moe_in, w_moe_out)
    half = MXU_COLS // 2
    bin_ = jnp.swapaxes(b_moe_in.reshape(N_EXPERTS, -1, half, 2), 2, 3).reshape(N_EXPERTS, 1, 2 * D_EXPERT)
    bout = b_moe_out.reshape(N_EXPERTS, 1, D_MODEL)

    slabs = lambda a: a.reshape(n * ROW_CHUNKS, LANES)
    tile = pl.BlockSpec((tt * ROW_CHUNKS, LANES), lambda i, e, *_: (i, 0))
    tile1 = pl.BlockSpec((tt * ROW_CHUNKS, LANES), lambda i, e, *_: (i, 0), pipeline_mode=pl.Buffered(1))
    smem_list = pl.BlockSpec((1, 1, cap), lambda i, e, *_: (i, 0, 0), memory_space=pltpu.SMEM)
    per_expert = lambda shape: pl.BlockSpec((1,) + shape, lambda i, e, *_: (e, 0, 0))
    grid_spec = pltpu.PrefetchScalarGridSpec(
        num_scalar_prefetch=2,
        grid=(n_tiles, N_EXPERTS),
        in_specs=[smem_list, smem_list, tile1, tile1,
                  per_expert((D_MODEL, 2 * D_EXPERT)), per_expert((1, 2 * D_EXPERT)),
                  per_expert((D_EXPERT, D_MODEL)), per_expert((1, D_MODEL)),
                  pl.BlockSpec((SUBLANES, LANES), lambda i, e, *_: (0, 0))],
        out_specs=tile,
        scratch_shapes=[pltpu.VMEM((ROW_CHUNKS * MOE_PITCH, LANES), F32),
                        pltpu.VMEM((ROW_CHUNKS * MOE_PITCH, LANES), F32)],
    )
    out = pl.pallas_call(
        _moe_kernel,
        grid_spec=grid_spec,
        out_shape=jax.ShapeDtypeStruct((n * ROW_CHUNKS, LANES), F32),
        compiler_params=_cparams(("arbitrary", "arbitrary")),
        name="moe",
    )(cnt.reshape(-1), off.reshape(-1).astype(I32), tok, gate, slabs(h), slabs(x1), win, bin_, wout, bout,
      norm_f_g.reshape(SUBLANES, LANES))
    return out.reshape(n, D_MODEL)


def kernel(x, norm1_g, w_in, b_gate, ssm_lam_re, ssm_lam_im, ssm_log_dt, ssm_b_re, ssm_b_im, ssm_c_re, ssm_c_im, ssm_d, w_glu, b_glu, w_proj_ssm, w_proj_attn, w_out, norm2_g, w_router, b_router, w_moe_in, b_moe_in, w_moe_out, b_moe_out, norm_f_g):
    bsz, seq_len, _ = x.shape
    n = bsz * seq_len
    x2 = x.reshape(n, D_MODEL)
    u, qt, k, vt, qit, ki, wit, gs, ga = _in_proj(x2, norm1_g[0], w_in[0], b_gate[0], seq_len)
    y_ssm = _s5(u, bsz, seq_len, ssm_lam_re[0], ssm_lam_im[0], ssm_log_dt[0], ssm_b_re[0], ssm_b_im[0],
                ssm_c_re[0], ssm_c_im[0], ssm_d[0], w_glu[0], b_glu[0])
    y_att = _attention(qt, qit, wit, k, ki, vt, bsz, seq_len)
    x1, h, gates = _mix(x2, y_ssm, y_att, gs, ga, w_proj_ssm[0], w_proj_attn[0], w_out[0], norm2_g[0],
                        w_router[0], b_router[0])
    out = _moe(h, x1, gates, w_moe_in[0], b_moe_in[0], w_moe_out[0], b_moe_out[0], norm_f_g)
    return out.reshape(x.shape)
```

```python
import functools
import math

import numpy as np
import jax
import jax.numpy as jnp
from jax import lax
from jax.experimental import pallas as pl
from jax.experimental.pallas import tpu as pltpu

D_MODEL = 1024
D_SSM = 512
SSM_GROUP = 16
N_SSM_GROUPS = 32
SSM_STATE = 64
N_HEADS = 8
N_KV_HEADS = 2
HEAD_DIM = 64
D_ATTN = N_HEADS * HEAD_DIM
D_KV = N_KV_HEADS * HEAD_DIM
IDX_HEADS = 16
IDX_DIM = 64
TOPK = 256
ROPE_THETA = 10000.0
N_EXPERTS = 32
TOP_K_EXPERTS = 4
D_EXPERT = D_MODEL
SWIGLU_LIMIT = 7.0
SWIGLU_ALPHA = 1.702
RMS_EPS = 1e-5
NEG = -1e30

LANES = 128
SUBLANES = 8
VMEM_LIMIT = 56 * 1024 * 1024

F32 = jnp.float32
BF16 = jnp.bfloat16
I32 = jnp.int32


def _cparams(sem):
    return pltpu.CompilerParams(dimension_semantics=sem, vmem_limit_bytes=VMEM_LIMIT)


def _const_spec(shape):
    nd = len(shape)
    return pl.BlockSpec(shape, lambda *_: (0,) * nd)


IN_TM = 512


def _in_proj_kernel(x_ref, g_ref, cos_ref, sin_ref, bg_ref,
                    wu_ref, wq_ref, wqr_ref, wkv_ref, wqi_ref, wqir_ref, wkw_ref, wgs_ref, wga_ref,
                    u_ref, qt_ref, k_ref, vt_ref, qit_ref, ki_ref, wit_ref, gs_ref, ga_ref):
    x = x_ref[...]
    xn = x * lax.rsqrt(jnp.mean(x * x, axis=-1, keepdims=True) + RMS_EPS) * g_ref[...]
    xb = xn.astype(BF16)

    def mm(w_ref):
        return jnp.dot(xb, w_ref[...], preferred_element_type=F32)

    cos = cos_ref[...]
    sin = sin_ref[...]

    def rope(a, ar):
        reps = a.shape[1] // LANES
        return a * jnp.tile(cos, (1, reps)) + ar * jnp.tile(sin, (1, reps))

    u_ref[...] = mm(wu_ref)
    q = rope(mm(wq_ref), mm(wqr_ref)) * (HEAD_DIM ** -0.5)
    qt_ref[...] = q.T.astype(BF16)
    kv = mm(wkv_ref)
    k = rope(kv[:, :D_KV], kv[:, D_KV:2 * D_KV])
    for g in range(N_KV_HEADS):
        k_ref[g] = k[:, g * HEAD_DIM:(g + 1) * HEAD_DIM].astype(BF16)
    vt_ref[...] = kv[:, 2 * D_KV:].T.astype(BF16)
    qi = rope(mm(wqi_ref), mm(wqir_ref)) * (IDX_DIM ** -0.5)
    qit_ref[...] = qi.T.astype(BF16)
    kw = mm(wkw_ref)
    ki = kw[:, :IDX_DIM] * cos[:, :IDX_DIM] + kw[:, IDX_DIM:2 * IDX_DIM] * sin[:, :IDX_DIM]
    ki_ref[...] = ki.astype(BF16)
    wit_ref[...] = (kw[:, LANES:2 * LANES].T)[:IDX_HEADS] * (IDX_HEADS ** -0.5)
    gs_ref[...] = jax.nn.sigmoid(mm(wgs_ref) + bg_ref[0:1, :]).astype(BF16)
    ga_ref[...] = jax.nn.sigmoid(mm(wga_ref) + bg_ref[1:2, :]).astype(BF16)


def _rot_half_cols(w, n_heads, d):
    k = w.shape[0]
    w4 = w.reshape(k, n_heads, 2, d // 2)
    return w4[:, :, ::-1, :].reshape(k, n_heads * d)


def _in_proj(x2, norm1_g, w_in, b_gate, seq_len):
    n = x2.shape[0]
    tm = min(IN_TM, seq_len)
    o = 0
    parts = []
    for width in (D_SSM, D_ATTN, D_KV, D_KV, IDX_HEADS * IDX_DIM, IDX_DIM, IDX_HEADS, D_MODEL, D_MODEL):
        parts.append(w_in[:, o:o + width])
        o += width
    wu, wq, wk, wv, wqi, wki, wwi, wgs, wga = parts
    wqr = _rot_half_cols(wq, N_HEADS, HEAD_DIM)
    wkr = _rot_half_cols(wk, N_KV_HEADS, HEAD_DIM)
    wqir = _rot_half_cols(wqi, IDX_HEADS, IDX_DIM)
    wkir = _rot_half_cols(wki, 1, IDX_DIM)
    wkv = jnp.concatenate([wk, wkr, wv], axis=1)
    wkw = jnp.concatenate([wki, wkir, wwi, jnp.zeros((D_MODEL, LANES - IDX_HEADS), F32)], axis=1)
    weights = [w.astype(BF16) for w in (wu, wq, wqr, wkv, wqi, wqir, wkw, wgs, wga)]

    half = HEAD_DIM // 2
    inv = ROPE_THETA ** (-jnp.arange(half, dtype=F32) / half)
    ang = jnp.arange(seq_len, dtype=F32)[:, None] * inv[None, :]
    cos = jnp.tile(jnp.cos(ang), (1, 4))
    sin = jnp.tile(jnp.concatenate([-jnp.sin(ang), jnp.sin(ang)], axis=1), (1, 2))

    tiles_per_seq = seq_len // tm
    row = lambda i: (i, 0)
    col = lambda i: (0, i)
    out_shapes = (
        jax.ShapeDtypeStruct((n, D_SSM), F32),
        jax.ShapeDtypeStruct((D_ATTN, n), BF16),
        jax.ShapeDtypeStruct((N_KV_HEADS, n, HEAD_DIM), BF16),
        jax.ShapeDtypeStruct((D_KV, n), BF16),
        jax.ShapeDtypeStruct((IDX_HEADS * IDX_DIM, n), BF16),
        jax.ShapeDtypeStruct((n, IDX_DIM), BF16),
        jax.ShapeDtypeStruct((IDX_HEADS, n), F32),
        jax.ShapeDtypeStruct((n, D_MODEL), BF16),
        jax.ShapeDtypeStruct((n, D_MODEL), BF16),
    )
    out_specs = (
        pl.BlockSpec((tm, D_SSM), row),
        pl.BlockSpec((D_ATTN, tm), col),
        pl.BlockSpec((N_KV_HEADS, tm, HEAD_DIM), lambda i: (0, i, 0)),
        pl.BlockSpec((D_KV, tm), col),
        pl.BlockSpec((IDX_HEADS * IDX_DIM, tm), col),
        pl.BlockSpec((tm, IDX_DIM), row),
        pl.BlockSpec((IDX_HEADS, tm), col),
        pl.BlockSpec((tm, D_MODEL), row),
        pl.BlockSpec((tm, D_MODEL), row),
    )
    in_specs = [
        pl.BlockSpec((tm, D_MODEL), row),
        _const_spec((1, D_MODEL)),
        pl.BlockSpec((tm, LANES), lambda i: (i % tiles_per_seq, 0)),
        pl.BlockSpec((tm, LANES), lambda i: (i % tiles_per_seq, 0)),
        _const_spec((2, D_MODEL)),
    ] + [_const_spec(w.shape) for w in weights]
    return pl.pallas_call(
        _in_proj_kernel,
        grid=(n // tm,),
        in_specs=in_specs,
        out_specs=out_specs,
        out_shape=out_shapes,
        compiler_params=_cparams(("arbitrary",)),
        name="in_proj",
    )(x2, norm1_g.reshape(1, D_MODEL), cos, sin, b_gate, *weights)


S5_T = 512
S5_CHUNK = 512
N_STATE = N_SSM_GROUPS * SSM_STATE
S5_HALF_IN = D_SSM // 2
S5_HALF_ST = N_STATE // 2


def _cmul(ar, ai, br, bi):
    return ar * br - ai * bi, ar * bi + ai * br


def _s5_kernel(u_ref, lr_ref, li_ref, ldt_ref, bre_ref, bim_ref, cre_ref, cim_ref, d_ref,
               wglu_ref, bglu_ref, y_ref,
               bb_ref, cc_ref, pw_re_ref, pw_im_ref, st_re_ref, st_im_ref, carry_re_ref, carry_im_ref):
    first = jnp.logical_and(pl.program_id(0) == 0, pl.program_id(1) == 0)

    @pl.when(first)
    def _():
        lr, li = lr_ref[...], li_ref[...]
        dt = jnp.exp(ldt_ref[...])
        mag = jnp.exp(lr * dt)
        ar = mag * jnp.cos(li * dt)
        ai = mag * jnp.sin(li * dt)
        den = lr * lr + li * li
        zr = ((ar - 1.0) * lr + ai * li) / den
        zi = (ai * lr - (ar - 1.0) * li) / den
        for h in range(2):
            rows = slice(h * S5_HALF_IN, (h + 1) * S5_HALF_IN)
            cols = slice(h * S5_HALF_ST, (h + 1) * S5_HALF_ST)
            bre, bim = bre_ref[rows, cols], bim_ref[rows, cols]
            bb_ref[h, :, :S5_HALF_ST] = (zr[:, cols] * bre - zi[:, cols] * bim).astype(BF16)
            bb_ref[h, :, S5_HALF_ST:] = (zr[:, cols] * bim + zi[:, cols] * bre).astype(BF16)
            cc_ref[h, :S5_HALF_ST, :] = cre_ref[cols, rows].astype(BF16)
            cc_ref[h, S5_HALF_ST:, :] = (-cim_ref[cols, rows]).astype(BF16)
        pr, pi = ar, ai
        pw_re_ref[0:1, :] = pr
        pw_im_ref[0:1, :] = pi
        for n in range(1, SUBLANES):
            pr, pi = _cmul(pr, pi, ar, ai)
            pw_re_ref[n:n + 1, :] = pr
            pw_im_ref[n:n + 1, :] = pi

    @pl.when(pl.program_id(1) == 0)
    def _():
        carry_re_ref[...] = jnp.zeros_like(carry_re_ref)
        carry_im_ref[...] = jnp.zeros_like(carry_im_ref)

    u = u_ref[...]
    ub = u.astype(BF16)
    for h in range(2):
        bu = jnp.dot(ub[:, h * S5_HALF_IN:(h + 1) * S5_HALF_IN], bb_ref[h], preferred_element_type=F32)
        st_re_ref[:, h * S5_HALF_ST:(h + 1) * S5_HALF_ST] = bu[:, :S5_HALF_ST]
        st_im_ref[:, h * S5_HALF_ST:(h + 1) * S5_HALF_ST] = bu[:, S5_HALF_ST:]

    t_len = u.shape[0]
    row = lax.broadcasted_iota(I32, (SUBLANES, S5_CHUNK), 0)
    for c in range(N_STATE // S5_CHUNK):
        lanes = slice(c * S5_CHUNK, (c + 1) * S5_CHUNK)
        p_re, p_im = pw_re_ref[:, lanes], pw_im_ref[:, lanes]
        steps = []
        for d in (1, 2, 4):
            a_re = jnp.where(row >= d, jnp.broadcast_to(p_re[d - 1:d, :], row.shape), 0.0)
            a_im = jnp.where(row >= d, jnp.broadcast_to(p_im[d - 1:d, :], row.shape), 0.0)
            steps.append((d, a_re, a_im))

        def block(i, carry):
            c_re, c_im = carry
            r0 = pl.multiple_of(i * SUBLANES, SUBLANES)
            xr = st_re_ref[pl.ds(r0, SUBLANES), lanes]
            xi = st_im_ref[pl.ds(r0, SUBLANES), lanes]
            for d, a_re, a_im in steps:
                sr = pltpu.roll(xr, d, 0)
                si = pltpu.roll(xi, d, 0)
                tr, ti = _cmul(a_re, a_im, sr, si)
                xr, xi = xr + tr, xi + ti
            tr, ti = _cmul(p_re, p_im, jnp.broadcast_to(c_re, xr.shape), jnp.broadcast_to(c_im, xi.shape))
            xr, xi = xr + tr, xi + ti
            st_re_ref[pl.ds(r0, SUBLANES), lanes] = xr
            st_im_ref[pl.ds(r0, SUBLANES), lanes] = xi
            return xr[SUBLANES - 1:, :], xi[SUBLANES - 1:, :]

        def two_blocks(j, carry):
            return block(2 * j + 1, block(2 * j, carry))

        c_re, c_im = lax.fori_loop(0, t_len // (2 * SUBLANES), two_blocks,
                                   (carry_re_ref[:, lanes], carry_im_ref[:, lanes]))
        carry_re_ref[:, lanes] = c_re
        carry_im_ref[:, lanes] = c_im

    ys = []
    for h in range(2):
        cols = slice(h * S5_HALF_ST, (h + 1) * S5_HALF_ST)
        xs = jnp.concatenate([st_re_ref[:, cols], st_im_ref[:, cols]], axis=1).astype(BF16)
        ys.append(jnp.dot(xs, cc_ref[h], preferred_element_type=F32))
    y = jnp.concatenate(ys, axis=1) + d_ref[...] * u
    y = jax.nn.gelu(y)
    gate = jnp.dot(y.astype(BF16), wglu_ref[...], preferred_element_type=F32) + bglu_ref[...]
    y_ref[...] = (y * jax.nn.sigmoid(gate)).astype(BF16)


def _s5(u, batch, seq_len, lam_re, lam_im, log_dt, b_re, b_im, c_re, c_im, d_skip, w_glu, b_glu):
    n = u.shape[0]
    t = min(S5_T, seq_len)
    g, p, hh = N_SSM_GROUPS, SSM_STATE, SSM_GROUP
    eye = jnp.eye(g, dtype=F32)
    bre_bd = jnp.einsum("gph,gk->ghkp", b_re, eye).reshape(D_SSM, N_STATE)
    bim_bd = jnp.einsum("gph,gk->ghkp", b_im, eye).reshape(D_SSM, N_STATE)
    cre_bd = jnp.einsum("ghp,gk->gpkh", c_re, eye).reshape(N_STATE, D_SSM)
    cim_bd = jnp.einsum("ghp,gk->gpkh", c_im, eye).reshape(N_STATE, D_SSM)
    lr = lam_re.reshape(1, N_STATE)
    li = lam_im.reshape(1, N_STATE)
    ldt = jnp.repeat(log_dt, p).reshape(1, N_STATE)
    tiles = seq_len // t
    consts = [lr, li, ldt, bre_bd, bim_bd, cre_bd, cim_bd, d_skip.reshape(1, D_SSM),
              w_glu.astype(BF16), b_glu.reshape(1, D_SSM)]
    return pl.pallas_call(
        _s5_kernel,
        grid=(batch, tiles),
        in_specs=[pl.BlockSpec((t, D_SSM), lambda b, i: (b * tiles + i, 0))]
        + [_const_spec(c.shape) for c in consts],
        out_specs=pl.BlockSpec((t, D_SSM), lambda b, i: (b * tiles + i, 0)),
        out_shape=jax.ShapeDtypeStruct((n, D_SSM), BF16),
        scratch_shapes=[
            pltpu.VMEM((2, S5_HALF_IN, 2 * S5_HALF_ST), BF16),
            pltpu.VMEM((2, 2 * S5_HALF_ST, S5_HALF_IN), BF16),
            pltpu.VMEM((SUBLANES, N_STATE), F32),
            pltpu.VMEM((SUBLANES, N_STATE), F32),
            pltpu.VMEM((t, N_STATE), F32),
            pltpu.VMEM((t, N_STATE), F32),
            pltpu.VMEM((1, N_STATE), F32),
            pltpu.VMEM((1, N_STATE), F32),
        ],
        compiler_params=_cparams(("arbitrary", "arbitrary")),
        name="s5",
    )(u, *consts)


AT_TQ = 256
AT_TK = 128
INT_MIN = -2 ** 31


def _attn_kernel(qt_ref, qit_ref, wit_ref, k_ref, ki_ref, vt_ref, y_ref, score_ref, m_ref, l_ref, acc_ref,
                 lg_ref, *, seq_len, n_sel):
    j = pl.program_id(1)
    tq, tk = AT_TQ, AT_TK
    n_diag = tq // tk
    n_full = j * n_diag
    q_pos = j * tq + lax.broadcasted_iota(I32, (tk, tq), 1)
    k_iota = lax.broadcasted_iota(I32, (tk, tq), 0)

    def tile_start(kt):
        return pl.multiple_of(kt * tk, tk)

    def score_tile(kt):
        r0 = tile_start(kt)
        ki_t = ki_ref[pl.ds(r0, tk), :]
        acc = jnp.zeros((tk, tq), F32)
        for h in range(IDX_HEADS):
            s = jnp.dot(ki_t, qit_ref[h * IDX_DIM:(h + 1) * IDX_DIM, :], preferred_element_type=F32)
            acc = acc + wit_ref[h:h + 1, :] * jnp.maximum(s, 0.0)
        score_ref[pl.ds(r0, tk), :] = jnp.where(r0 + k_iota <= q_pos, acc, NEG)

    def score_pair(i, c):
        score_tile(2 * i)
        score_tile(2 * i + 1)
        return c

    n_tot = n_full + n_diag
    lax.fori_loop(0, n_tot // 2, score_pair, 0)

    n_beyond = seq_len - (j + 1) * tq

    def candidate(u):
        s = u ^ INT_MIN
        return lax.bitcast_convert_type(jnp.where(s < 0, s ^ 0x7FFFFFFF, s), F32)

    def count_tile(c, cnt, cand):
        r0 = pl.multiple_of(c * tq, tq)
        ge = jnp.where(score_ref[pl.ds(r0, tq), :] >= cand, 1, 0)
        rows = tq // 4
        parts = [ge[i * rows:(i + 1) * rows].reshape(rows // SUBLANES, SUBLANES, tq).sum(axis=0)
                 for i in range(4)]
        return cnt + ((parts[0] + parts[1]) + (parts[2] + parts[3]))

    def bit_step(b, thr_u):
        cand_u = thr_u | lax.shift_left(jnp.int32(1), 31 - b)
        cand = candidate(cand_u)
        cnt = lax.fori_loop(0, j + 1, functools.partial(count_tile, cand=cand),
                            jnp.zeros((SUBLANES, tq), I32))
        total = jnp.sum(cnt, axis=0, keepdims=True) + jnp.where(cand <= NEG, n_beyond, 0)
        return jnp.where(total >= n_sel, cand_u, thr_u)

    thr = candidate(lax.fori_loop(0, 32, bit_step, jnp.zeros((1, tq), I32)))

    m_ref[...] = jnp.full(m_ref.shape, NEG, F32)
    l_ref[...] = jnp.zeros(l_ref.shape, F32)
    acc_ref[...] = jnp.zeros(acc_ref.shape, F32)
    rep = N_HEADS // N_KV_HEADS

    def logits(kt, slot):
        r0 = tile_start(kt)
        valid = jnp.logical_and(score_ref[pl.ds(r0, tk), :] >= thr, r0 + k_iota <= q_pos)
        bias = jnp.where(valid, 0.0, NEG)
        for g in range(N_KV_HEADS):
            k_t = k_ref[g, pl.ds(r0, tk), :]
            for h in range(g * rep, (g + 1) * rep):
                q_h = qt_ref[h * HEAD_DIM:(h + 1) * HEAD_DIM, :]
                lg_ref[slot, h] = jnp.dot(k_t, q_h, preferred_element_type=F32) + bias

    def softmax_values(kt, slot):
        r0 = tile_start(kt)
        for g in range(N_KV_HEADS):
            v_t = vt_ref[g * HEAD_DIM:(g + 1) * HEAD_DIM, pl.ds(r0, tk)]
            for h in range(g * rep, (g + 1) * rep):
                lg = lg_ref[slot, h]
                m_old = m_ref[h]
                m_new = jnp.maximum(m_old, jnp.max(lg, axis=0, keepdims=True))
                p = jnp.exp(lg - m_new)
                alpha = jnp.exp(m_old - m_new)
                l_ref[h] = alpha * l_ref[h] + jnp.sum(p, axis=0, keepdims=True)
                acc_ref[h] = alpha * acc_ref[h] + jnp.dot(v_t, p.astype(BF16), preferred_element_type=F32)
                m_ref[h] = m_new

    logits(0, 0)

    def tile_pair(i, c):
        logits(2 * i + 1, 1)
        softmax_values(2 * i, 0)
        logits(jnp.minimum(2 * i + 2, n_tot - 1), 0)
        softmax_values(2 * i + 1, 1)
        return c

    lax.fori_loop(0, n_tot // 2, tile_pair, 0)
    outs = [acc_ref[h] / l_ref[h] for h in range(N_HEADS)]
    y_ref[...] = jnp.concatenate(outs, axis=0).T.astype(BF16)


def _attention(qt, qit, wit, k, ki, vt, batch, seq_len):
    n = ki.shape[0]
    tq = AT_TQ
    nq = seq_len // tq
    n_sel = min(TOPK, seq_len // 4)
    qcol = lambda b, j: (0, b * nq + j)
    return pl.pallas_call(
        functools.partial(_attn_kernel, seq_len=seq_len, n_sel=n_sel),
        grid=(batch, nq),
        in_specs=[
            pl.BlockSpec((D_ATTN, tq), qcol),
            pl.BlockSpec((IDX_HEADS * IDX_DIM, tq), qcol),
            pl.BlockSpec((IDX_HEADS, tq), qcol),
            pl.BlockSpec((N_KV_HEADS, seq_len, HEAD_DIM), lambda b, j: (0, b, 0)),
            pl.BlockSpec((seq_len, IDX_DIM), lambda b, j: (b, 0)),
            pl.BlockSpec((D_KV, seq_len), lambda b, j: (0, b)),
        ],
        out_specs=pl.BlockSpec((tq, D_ATTN), lambda b, j: (b * nq + j, 0)),
        out_shape=jax.ShapeDtypeStruct((n, D_ATTN), BF16),
        scratch_shapes=[pltpu.VMEM((seq_len, tq), F32),
                        pltpu.VMEM((N_HEADS, 1, tq), F32),
                        pltpu.VMEM((N_HEADS, 1, tq), F32),
                        pltpu.VMEM((N_HEADS, HEAD_DIM, tq), F32),
                        pltpu.VMEM((2, N_HEADS, AT_TK, tq), F32)],
        compiler_params=_cparams(("arbitrary", "arbitrary")),
        name="attn",
    )(qt, qit, wit, k, ki, vt)


MIX_TM = 512


def _split_bf16(a):
    hi = a.astype(BF16)
    return hi, (a - hi.astype(F32)).astype(BF16)


def _mix_kernel(x_ref, ys_ref, ya_ref, gs_ref, ga_ref, wps_ref, wpa_ref, wo_ref, g2_ref,
                wr_hi_ref, wr_lo_ref, br_ref, x1_ref, h_ref, gates_ref):
    ps = jnp.dot(ys_ref[...], wps_ref[...], preferred_element_type=F32)
    pa = jnp.dot(ya_ref[...], wpa_ref[...], preferred_element_type=F32)
    mixed = gs_ref[...].astype(F32) * ps + ga_ref[...].astype(F32) * pa
    x1 = x_ref[...] + jnp.dot(mixed.astype(BF16), wo_ref[...], preferred_element_type=F32)
    h = x1 * lax.rsqrt(jnp.mean(x1 * x1, axis=-1, keepdims=True) + RMS_EPS) * g2_ref[...]
    tm = x1.shape[0]
    for j in range(D_MODEL // LANES):
        x1_ref[pl.ds(j, tm, stride=D_MODEL // LANES), :] = x1[:, j * LANES:(j + 1) * LANES]
        h_ref[pl.ds(j, tm, stride=D_MODEL // LANES), :] = h[:, j * LANES:(j + 1) * LANES]
    h_hi, h_lo = _split_bf16(h)
    logits = (jnp.dot(h_hi, wr_hi_ref[...], preferred_element_type=F32)
              + jnp.dot(h_hi, wr_lo_ref[...], preferred_element_type=F32)
              + jnp.dot(h_lo, wr_hi_ref[...], preferred_element_type=F32)) + br_ref[...]
    lane = lax.broadcasted_iota(I32, logits.shape, 1)
    rest = logits
    picks, vals = [], []
    for _ in range(TOP_K_EXPERTS):
        m = jnp.max(rest, axis=-1, keepdims=True)
        first = jnp.min(jnp.where(rest == m, lane, N_EXPERTS), axis=-1, keepdims=True)
        pick = lane == first
        picks.append(pick)
        vals.append(m)
        rest = jnp.where(pick, -jnp.inf, rest)
    es = [jnp.exp(v - vals[0]) for v in vals]
    denom = es[0] + es[1] + es[2] + es[3]
    gates = jnp.zeros_like(logits)
    for pick, e in zip(picks, es):
        gates = gates + jnp.where(pick, e / denom, 0.0)
    gates_ref[...] = gates


def _mix(x2, y_ssm, y_att, gs, ga, w_proj_ssm, w_proj_attn, w_out, norm2_g, w_router, b_router):
    n = x2.shape[0]
    tm = min(MIX_TM, n)
    chunks = D_MODEL // LANES
    row = lambda i: (i, 0)
    wr_hi, wr_lo = _split_bf16(w_router)
    consts = [w_proj_ssm.astype(BF16), w_proj_attn.astype(BF16), w_out.astype(BF16),
              norm2_g.reshape(1, D_MODEL), wr_hi, wr_lo, b_router.reshape(1, N_EXPERTS)]
    return pl.pallas_call(
        _mix_kernel,
        grid=(n // tm,),
        in_specs=[pl.BlockSpec((tm, D_MODEL), row), pl.BlockSpec((tm, D_SSM), row),
                  pl.BlockSpec((tm, D_ATTN), row), pl.BlockSpec((tm, D_MODEL), row),
                  pl.BlockSpec((tm, D_MODEL), row)] + [_const_spec(c.shape) for c in consts],
        out_specs=(pl.BlockSpec((tm * chunks, LANES), row), pl.BlockSpec((tm * chunks, LANES), row),
                   pl.BlockSpec((tm, N_EXPERTS), row)),
        out_shape=(jax.ShapeDtypeStruct((n * chunks, LANES), F32), jax.ShapeDtypeStruct((n * chunks, LANES), F32),
                   jax.ShapeDtypeStruct((n, N_EXPERTS), F32)),
        compiler_params=_cparams(("arbitrary",)),
        name="mix",
    )(x2, y_ssm, y_att, gs, ga, *consts)


MOE_TT = 2048
MOE_RB = 288
ROW_CHUNKS = D_MODEL // LANES
assert ROW_CHUNKS == SUBLANES
MOE_PITCH = MOE_RB + SUBLANES


def _moe_kernel(cnt_ref, off_ref, tok_ref, gate_ref, h_ref, x1_ref, win_ref, bin_ref, wout_ref, bout_ref,
                gf_ref, out_ref, xt_ref, yt_ref):
    i, e = pl.program_id(0), pl.program_id(1)
    n = cnt_ref[i * N_EXPERTS + e]
    start = off_ref[i * N_EXPERTS + e]

    @pl.when(e == 0)
    def _():
        out_ref[...] = x1_ref[...]

    def slab(t):
        return pl.ds(pl.multiple_of(t * SUBLANES, SUBLANES), SUBLANES)

    def block(b, c):
        base = start + b * MOE_RB

        def gather8(r8, c2):
            for rr in range(SUBLANES):
                r = r8 * SUBLANES + rr
                t = tok_ref[0, 0, base + r]
                xt_ref[pl.ds(r, SUBLANES, stride=MOE_PITCH), :] = h_ref[slab(t), :]
            return c2

        lax.fori_loop(0, MOE_RB // SUBLANES, gather8, 0)
        xg = jnp.concatenate([xt_ref[j * MOE_PITCH:j * MOE_PITCH + MOE_RB, :] for j in range(ROW_CHUNKS)],
                             axis=1).astype(BF16)
        z = jnp.dot(xg, win_ref[0], preferred_element_type=F32) + bin_ref[0]
        half = MXU_COLS // 2
        groups = range(2 * D_EXPERT // MXU_COLS)
        zg = jnp.concatenate([z[:, c * MXU_COLS:c * MXU_COLS + half] for c in groups], axis=1)
        zl = jnp.concatenate([z[:, c * MXU_COLS + half:(c + 1) * MXU_COLS] for c in groups], axis=1)
        zg = jnp.minimum(zg, SWIGLU_LIMIT)
        zl = jnp.clip(zl, -SWIGLU_LIMIT, SWIGLU_LIMIT)
        act = zg * jax.nn.sigmoid(SWIGLU_ALPHA * zg) * (zl + 1.0)
        y = jnp.dot(act.astype(BF16), wout_ref[0], preferred_element_type=F32) + bout_ref[0]
        for j in range(ROW_CHUNKS):
            yt_ref[j * MOE_PITCH:j * MOE_PITCH + MOE_RB, :] = y[:, j * LANES:(j + 1) * LANES]

        def updated(r):
            t = tok_ref[0, 0, base + r]
            g = gate_ref[0, 0, base + r]
            return t, out_ref[slab(t), :] + g * yt_ref[pl.ds(r, SUBLANES, stride=MOE_PITCH), :]

        def scatter8(r8, c2):
            rows = [updated(r8 * SUBLANES + rr) for rr in range(SUBLANES)]
            for t, v in rows:
                out_ref[slab(t), :] = v
            return c2

        def scatter1(r, c2):
            t, v = updated(r)
            out_ref[slab(t), :] = v
            return c2

        n_rows = jnp.minimum(MOE_RB, n - b * MOE_RB)
        n_groups = n_rows // SUBLANES
        lax.fori_loop(0, n_groups, scatter8, 0)
        lax.fori_loop(n_groups * SUBLANES, n_rows, scatter1, 0)
        return c

    lax.fori_loop(0, (n + MOE_RB - 1) // MOE_RB, block, 0)

    @pl.when(e == N_EXPERTS - 1)
    def _():
        group = 256
        gf = gf_ref[...]

        def norm(c, c2):
            rows = pl.ds(pl.multiple_of(c * group * SUBLANES, group * SUBLANES), group * SUBLANES)
            v = out_ref[rows, :].reshape(group, SUBLANES, LANES)
            ss = jnp.sum(jnp.sum(v * v, axis=2, keepdims=True), axis=1, keepdims=True)
            o = v * lax.rsqrt(ss * (1.0 / D_MODEL) + RMS_EPS) * gf
            out_ref[rows, :] = o.reshape(group * SUBLANES, LANES)
            return c2

        lax.fori_loop(0, out_ref.shape[0] // (group * SUBLANES), norm, 0)


MXU_COLS = 256


def _expert_weights_kernel(win_ref, wout_ref, perm_ref, winp_ref, woutb_ref):
    w = win_ref[0].astype(BF16)
    for c in range(w.shape[1] // MXU_COLS):
        cols = slice(c * MXU_COLS, (c + 1) * MXU_COLS)
        winp_ref[0, :, cols] = jnp.dot(w[:, cols], perm_ref[...], preferred_element_type=F32).astype(BF16)
    woutb_ref[0] = wout_ref[0].astype(BF16)


def _expert_weights(w_moe_in, w_moe_out):
    half = MXU_COLS // 2
    src = jnp.concatenate([jnp.arange(half) * 2, jnp.arange(half) * 2 + 1])
    perm = (jnp.arange(MXU_COLS)[:, None] == src[None, :]).astype(BF16)
    per_expert = lambda a: pl.BlockSpec((1,) + a.shape[1:], lambda e: (e, 0, 0))
    return pl.pallas_call(
        _expert_weights_kernel,
        grid=(N_EXPERTS,),
        in_specs=[per_expert(w_moe_in), per_expert(w_moe_out), _const_spec(perm.shape)],
        out_specs=(per_expert(w_moe_in), per_expert(w_moe_out)),
        out_shape=(jax.ShapeDtypeStruct(w_moe_in.shape, BF16), jax.ShapeDtypeStruct(w_moe_out.shape, BF16)),
        compiler_params=_cparams(("arbitrary",)),
        name="expert_weights",
    )(w_moe_in, w_moe_out, perm)


def _moe(h, x1, gates, w_moe_in, b_moe_in, w_moe_out, b_moe_out, norm_f_g):
    n = h.shape[0] // ROW_CHUNKS
    tt = min(MOE_TT, n)
    n_tiles = n // tt
    cap = TOP_K_EXPERTS * tt + MOE_RB

    gt = jnp.transpose(gates.reshape(n_tiles, tt, N_EXPERTS), (0, 2, 1))
    sel = gt > 0.0
    cnt = jnp.sum(sel, axis=2).astype(I32)
    off = jnp.cumsum(cnt, axis=1) - cnt
    flat_sel = sel.reshape(n_tiles, N_EXPERTS * tt)
    order = jnp.argsort(jnp.logical_not(flat_sel), axis=1, stable=True)[:, :cap]
    tok = (order % tt).astype(I32).reshape(n_tiles, 1, cap)
    gate = jnp.take_along_axis(gt.reshape(n_tiles, N_EXPERTS * tt), order, axis=1).reshape(n_tiles, 1, cap)

    win, wout = _expert_weights(w_moe_in, w_moe_out)
    half = MXU_COLS // 2
    bin_ = jnp.swapaxes(b_moe_in.reshape(N_EXPERTS, -1, half, 2), 2, 3).reshape(N_EXPERTS, 1, 2 * D_EXPERT)
    bout = b_moe_out.reshape(N_EXPERTS, 1, D_MODEL)

    tile =pl.BlockSpec((tt * ROW_CHUNKS, LANES), lambda i, e, *_: (i, 0))
    tile1 = pl.BlockSpec((tt * ROW_CHUNKS, LANES), lambda i, e, *_: (i, 0), pipeline_mode=pl.Buffered(1))
    smem_list = pl.BlockSpec((1, 1, cap), lambda i, e, *_: (i, 0, 0), memory_space=pltpu.SMEM)
    per_expert = lambda shape: pl.BlockSpec((1,) + shape, lambda i, e, *_: (e, 0, 0))
    grid_spec = pltpu.PrefetchScalarGridSpec(
        num_scalar_prefetch=2,
        grid=(n_tiles, N_EXPERTS),
        in_specs=[smem_list, smem_list, tile1, tile1,
                  per_expert((D_MODEL, 2 * D_EXPERT)), per_expert((1, 2 * D_EXPERT)),
                  per_expert((D_EXPERT, D_MODEL)), per_expert((1, D_MODEL)),
                  pl.BlockSpec((SUBLANES, LANES), lambda i, e, *_: (0, 0))],
        out_specs=tile,
        scratch_shapes=[pltpu.VMEM((ROW_CHUNKS * MOE_PITCH, LANES), F32),
                        pltpu.VMEM((ROW_CHUNKS * MOE_PITCH, LANES), F32)],
    )
    out = pl.pallas_call(
        _moe_kernel,
        grid_spec=grid_spec,
        out_shape=jax.ShapeDtypeStruct((n * ROW_CHUNKS, LANES), F32),
        compiler_params=_cparams(("arbitrary", "arbitrary")),
        name="moe",
    )(cnt.reshape(-1), off.reshape(-1).astype(I32), tok, gate, h, x1, win, bin_, wout, bout,
      norm_f_g.reshape(SUBLANES, LANES))
    return out.reshape(n, D_MODEL)


def kernel(x, norm1_g, w_in, b_gate, ssm_lam_re, ssm_lam_im, ssm_log_dt, ssm_b_re, ssm_b_im, ssm_c_re, ssm_c_im, ssm_d, w_glu, b_glu, w_proj_ssm, w_proj_attn, w_out, norm2_g, w_router, b_router, w_moe_in, b_moe_in, w_moe_out, b_moe_out, norm_f_g):
    bsz, seq_len, _ = x.shape
    n = bsz * seq_len
    x2 = x.reshape(n, D_MODEL)
    u, qt, k, vt, qit, ki, wit, gs, ga = _in_proj(x2, norm1_g[0], w_in[0], b_gate[0], seq_len)
    y_ssm = _s5(u, bsz, seq_len, ssm_lam_re[0], ssm_lam_im[0], ssm_log_dt[0], ssm_b_re[0], ssm_b_im[0],
                ssm_c_re[0], ssm_c_im[0], ssm_d[0], w_glu[0], b_glu[0])
    y_att = _attention(qt, qit, wit, k, ki, vt, bsz, seq_len)
    x1, h, gates = _mix(x2, y_ssm, y_att, gs, ga, w_proj_ssm[0], w_proj_attn[0], w_out[0], norm2_g[0],
                        w_router[0], b_router[0])
    out = _moe(h, x1, gates, w_moe_in[0], b_moe_in[0], w_moe_out[0], b_moe_out[0], norm_f_g)
    return out.reshape(x.shape)
```

```python
import functools
import math

import numpy as np
import jax
import jax.numpy as jnp
from jax import lax
from jax.experimental import pallas as pl
from jax.experimental.pallas import tpu as pltpu

D_MODEL = 1024
D_SSM = 512
SSM_GROUP = 16
N_SSM_GROUPS = 32
SSM_STATE = 64
N_HEADS = 8
N_KV_HEADS = 2
HEAD_DIM = 64
D_ATTN = N_HEADS * HEAD_DIM
D_KV = N_KV_HEADS * HEAD_DIM
IDX_HEADS = 16
IDX_DIM = 64
TOPK = 256
ROPE_THETA = 10000.0
N_EXPERTS = 32
TOP_K_EXPERTS = 4
D_EXPERT = D_MODEL
SWIGLU_LIMIT = 7.0
SWIGLU_ALPHA = 1.702
RMS_EPS = 1e-5
NEG = -1e30

LANES = 128
SUBLANES = 8
VMEM_LIMIT = 56 * 1024 * 1024

F32 = jnp.float32
BF16 = jnp.bfloat16
I32 = jnp.int32


def _cparams(sem):
    return pltpu.CompilerParams(dimension_semantics=sem, vmem_limit_bytes=VMEM_LIMIT)


def _const_spec(shape):
    nd = len(shape)
    return pl.BlockSpec(shape, lambda *_: (0,) * nd)


IN_TM = 512


def _in_proj_kernel(x_ref, g_ref, cos_ref, sin_ref, bg_ref,
                    wu_ref, wq_ref, wqr_ref, wkv_ref, wqi_ref, wqir_ref, wkw_ref, wgs_ref, wga_ref,
                    u_ref, qt_ref, k_ref, vt_ref, qit_ref, ki_ref, wit_ref, gs_ref, ga_ref):
    x = x_ref[...]
    xn = x * lax.rsqrt(jnp.mean(x * x, axis=-1, keepdims=True) + RMS_EPS) * g_ref[...]
    xb = xn.astype(BF16)

    def mm(w_ref):
        return jnp.dot(xb, w_ref[...], preferred_element_type=F32)

    cos = cos_ref[...]
    sin = sin_ref[...]

    def rope(a, ar):
        reps = a.shape[1] // LANES
        return a * jnp.tile(cos, (1, reps)) + ar * jnp.tile(sin, (1, reps))

    u_ref[...] = mm(wu_ref)
    q = rope(mm(wq_ref), mm(wqr_ref)) * (HEAD_DIM ** -0.5)
    qt_ref[...] = q.T.astype(BF16)
    kv = mm(wkv_ref)
    k = rope(kv[:, :D_KV], kv[:, D_KV:2 * D_KV])
    for g in range(N_KV_HEADS):
        k_ref[g] = k[:, g * HEAD_DIM:(g + 1) * HEAD_DIM].astype(BF16)
    vt_ref[...] = kv[:, 2 * D_KV:].T.astype(BF16)
    qi = rope(mm(wqi_ref), mm(wqir_ref)) * (IDX_DIM ** -0.5)
    qit_ref[...] = qi.T.astype(BF16)
    kw = mm(wkw_ref)
    ki = kw[:, :IDX_DIM] * cos[:, :IDX_DIM] + kw[:, IDX_DIM:2 * IDX_DIM] * sin[:, :IDX_DIM]
    ki_ref[...] = ki.astype(BF16)
    wit_ref[...] = (kw[:, LANES:2 * LANES].T)[:IDX_HEADS] * (IDX_HEADS ** -0.5)
    gs_ref[...] = jax.nn.sigmoid(mm(wgs_ref) + bg_ref[0:1, :]).astype(BF16)
    ga_ref[...] = jax.nn.sigmoid(mm(wga_ref) + bg_ref[1:2, :]).astype(BF16)


def _rot_half_cols(w, n_heads, d):
    k = w.shape[0]
    w4 = w.reshape(k, n_heads, 2, d // 2)
    return w4[:, :, ::-1, :].reshape(k, n_heads * d)


def _in_proj(x2, norm1_g, w_in, b_gate, seq_len):
    n = x2.shape[0]
    tm = min(IN_TM, seq_len)
    o = 0
    parts = []
    for width in (D_SSM, D_ATTN, D_KV, D_KV, IDX_HEADS * IDX_DIM, IDX_DIM, IDX_HEADS, D_MODEL, D_MODEL):
        parts.append(w_in[:, o:o + width])
        o += width
    wu, wq, wk, wv, wqi, wki, wwi, wgs, wga = parts
    wqr = _rot_half_cols(wq, N_HEADS, HEAD_DIM)
    wkr = _rot_half_cols(wk, N_KV_HEADS, HEAD_DIM)
    wqir = _rot_half_cols(wqi, IDX_HEADS, IDX_DIM)
    wkir = _rot_half_cols(wki, 1, IDX_DIM)
    wkv = jnp.concatenate([wk, wkr, wv], axis=1)
    wkw = jnp.concatenate([wki, wkir, wwi, jnp.zeros((D_MODEL, LANES - IDX_HEADS), F32)], axis=1)
    weights = [w.astype(BF16) for w in (wu, wq, wqr, wkv, wqi, wqir, wkw, wgs, wga)]

    half = HEAD_DIM // 2
    inv = ROPE_THETA ** (-jnp.arange(half, dtype=F32) / half)
    ang = jnp.arange(seq_len, dtype=F32)[:, None] * inv[None, :]
    cos = jnp.tile(jnp.cos(ang), (1, 4))
    sin = jnp.tile(jnp.concatenate([-jnp.sin(ang), jnp.sin(ang)], axis=1), (1, 2))

    tiles_per_seq = seq_len // tm
    row = lambda i: (i, 0)
    col = lambda i: (0, i)
    out_shapes = (
        jax.ShapeDtypeStruct((n, D_SSM), F32),
        jax.ShapeDtypeStruct((D_ATTN, n), BF16),
        jax.ShapeDtypeStruct((N_KV_HEADS, n, HEAD_DIM), BF16),
        jax.ShapeDtypeStruct((D_KV, n), BF16),
        jax.ShapeDtypeStruct((IDX_HEADS * IDX_DIM, n), BF16),
        jax.ShapeDtypeStruct((n, IDX_DIM), BF16),
        jax.ShapeDtypeStruct((IDX_HEADS, n), F32),
        jax.ShapeDtypeStruct((n, D_MODEL), BF16),
        jax.ShapeDtypeStruct((n, D_MODEL), BF16),
    )
    out_specs = (
        pl.BlockSpec((tm, D_SSM), row),
        pl.BlockSpec((D_ATTN, tm), col),
        pl.BlockSpec((N_KV_HEADS, tm, HEAD_DIM), lambda i: (0, i, 0)),
        pl.BlockSpec((D_KV, tm), col),
        pl.BlockSpec((IDX_HEADS * IDX_DIM, tm), col),
        pl.BlockSpec((tm, IDX_DIM), row),
        pl.BlockSpec((IDX_HEADS, tm), col),
        pl.BlockSpec((tm, D_MODEL), row),
        pl.BlockSpec((tm, D_MODEL), row),
    )
    in_specs = [
        pl.BlockSpec((tm, D_MODEL), row),
        _const_spec((1, D_MODEL)),
        pl.BlockSpec((tm, LANES), lambda i: (i % tiles_per_seq, 0)),
        pl.BlockSpec((tm, LANES), lambda i: (i % tiles_per_seq, 0)),
        _const_spec((2, D_MODEL)),
    ] + [_const_spec(w.shape) for w in weights]
    return pl.pallas_call(
        _in_proj_kernel,
        grid=(n // tm,),
        in_specs=in_specs,
        out_specs=out_specs,
        out_shape=out_shapes,
        compiler_params=_cparams(("arbitrary",)),
        name="in_proj",
    )(x2, norm1_g.reshape(1, D_MODEL), cos, sin, b_gate, *weights)


S5_T = 512
S5_CHUNK = 512
N_STATE = N_SSM_GROUPS * SSM_STATE
S5_HALF_IN = D_SSM // 2
S5_HALF_ST = N_STATE // 2


def _cmul(ar, ai, br, bi):
    return ar * br - ai * bi, ar * bi + ai * br


def _s5_kernel(u_ref, lr_ref, li_ref, ldt_ref, bre_ref, bim_ref, cre_ref, cim_ref, d_ref,
               wglu_ref, bglu_ref, y_ref,
               bb_ref, cc_ref, pw_re_ref, pw_im_ref, st_re_ref, st_im_ref, carry_re_ref, carry_im_ref):
    first = jnp.logical_and(pl.program_id(0) == 0, pl.program_id(1) == 0)

    @pl.when(first)
    def _():
        lr, li = lr_ref[...], li_ref[...]
        dt = jnp.exp(ldt_ref[...])
        mag = jnp.exp(lr * dt)
        ar = mag * jnp.cos(li * dt)
        ai = mag * jnp.sin(li * dt)
        den = lr * lr + li * li
        zr = ((ar - 1.0) * lr + ai * li) / den
        zi = (ai * lr - (ar - 1.0) * li) / den
        for h in range(2):
            rows = slice(h * S5_HALF_IN, (h + 1) * S5_HALF_IN)
            cols = slice(h * S5_HALF_ST, (h + 1) * S5_HALF_ST)
            bre, bim = bre_ref[rows, cols], bim_ref[rows, cols]
            bb_ref[h, :, :S5_HALF_ST] = (zr[:, cols] * bre - zi[:, cols] * bim).astype(BF16)
            bb_ref[h, :, S5_HALF_ST:] = (zr[:, cols] * bim + zi[:, cols] * bre).astype(BF16)
            cc_ref[h, :S5_HALF_ST, :] = cre_ref[cols, rows].astype(BF16)
            cc_ref[h, S5_HALF_ST:, :] = (-cim_ref[cols, rows]).astype(BF16)
        pr, pi = ar, ai
        pw_re_ref[0:1, :] = pr
        pw_im_ref[0:1, :] = pi
        for n in range(1, SUBLANES):
            pr, pi = _cmul(pr, pi, ar, ai)
            pw_re_ref[n:n + 1, :] = pr
            pw_im_ref[n:n + 1, :] = pi

    @pl.when(pl.program_id(1) == 0)
    def _():
        carry_re_ref[...] = jnp.zeros_like(carry_re_ref)
        carry_im_ref[...] = jnp.zeros_like(carry_im_ref)

    u = u_ref[...]
    ub = u.astype(BF16)
    for h in range(2):
        bu = jnp.dot(ub[:, h * S5_HALF_IN:(h + 1) * S5_HALF_IN], bb_ref[h], preferred_element_type=F32)
        st_re_ref[:, h * S5_HALF_ST:(h + 1) * S5_HALF_ST] = bu[:, :S5_HALF_ST]
        st_im_ref[:, h * S5_HALF_ST:(h + 1) * S5_HALF_ST] = bu[:, S5_HALF_ST:]

    t_len = u.shape[0]
    row = lax.broadcasted_iota(I32, (SUBLANES, S5_CHUNK), 0)
    for c in range(N_STATE // S5_CHUNK):
        lanes = slice(c * S5_CHUNK, (c + 1) * S5_CHUNK)
        p_re, p_im = pw_re_ref[:, lanes], pw_im_ref[:, lanes]
        steps = []
        for d in (1, 2, 4):
            a_re = jnp.where(row >= d, jnp.broadcast_to(p_re[d - 1:d, :], row.shape), 0.0)
            a_im = jnp.where(row >= d, jnp.broadcast_to(p_im[d - 1:d, :], row.shape), 0.0)
            steps.append((d, a_re, a_im))

        def block(i, carry):
            c_re, c_im = carry
            r0 = pl.multiple_of(i * SUBLANES, SUBLANES)
            xr = st_re_ref[pl.ds(r0, SUBLANES), lanes]
            xi = st_im_ref[pl.ds(r0, SUBLANES), lanes]
            for d, a_re, a_im in steps:
                sr = pltpu.roll(xr, d, 0)
                si = pltpu.roll(xi, d, 0)
                tr, ti = _cmul(a_re, a_im, sr, si)
                xr, xi = xr + tr, xi + ti
            tr, ti = _cmul(p_re, p_im, jnp.broadcast_to(c_re, xr.shape), jnp.broadcast_to(c_im, xi.shape))
            xr, xi = xr + tr, xi + ti
            st_re_ref[pl.ds(r0, SUBLANES), lanes] = xr
            st_im_ref[pl.ds(r0, SUBLANES), lanes] = xi
            return xr[SUBLANES - 1:, :], xi[SUBLANES - 1:, :]

        def two_blocks(j, carry):
            return block(2 * j + 1, block(2 * j, carry))

        c_re, c_im = lax.fori_loop(0, t_len // (2 * SUBLANES), two_blocks,
                                   (carry_re_ref[:, lanes], carry_im_ref[:, lanes]))
        carry_re_ref[:, lanes] = c_re
        carry_im_ref[:, lanes] = c_im

    ys = []
    for h in range(2):
        cols = slice(h * S5_HALF_ST, (h + 1) * S5_HALF_ST)
        xs = jnp.concatenate([st_re_ref[:, cols], st_im_ref[:, cols]], axis=1).astype(BF16)
        ys.append(jnp.dot(xs, cc_ref[h], preferred_element_type=F32))
    y = jnp.concatenate(ys, axis=1) + d_ref[...] * u
    y = jax.nn.gelu(y)
    gate = jnp.dot(y.astype(BF16), wglu_ref[...], preferred_element_type=F32) + bglu_ref[...]
    y_ref[...] = (y * jax.nn.sigmoid(gate)).astype(BF16)


def _s5(u, batch, seq_len, lam_re, lam_im, log_dt, b_re, b_im, c_re, c_im, d_skip, w_glu, b_glu):
    n = u.shape[0]
    t = min(S5_T, seq_len)
    g, p, hh = N_SSM_GROUPS, SSM_STATE, SSM_GROUP
    eye = jnp.eye(g, dtype=F32)
    bre_bd = jnp.einsum("gph,gk->ghkp", b_re, eye).reshape(D_SSM, N_STATE)
    bim_bd = jnp.einsum("gph,gk->ghkp", b_im, eye).reshape(D_SSM, N_STATE)
    cre_bd = jnp.einsum("ghp,gk->gpkh", c_re, eye).reshape(N_STATE, D_SSM)
    cim_bd = jnp.einsum("ghp,gk->gpkh", c_im, eye).reshape(N_STATE, D_SSM)
    lr = lam_re.reshape(1, N_STATE)
    li = lam_im.reshape(1, N_STATE)
    ldt = jnp.repeat(log_dt, p).reshape(1, N_STATE)
    tiles = seq_len // t
    consts = [lr, li, ldt, bre_bd, bim_bd, cre_bd, cim_bd, d_skip.reshape(1, D_SSM),
              w_glu.astype(BF16), b_glu.reshape(1, D_SSM)]
    return pl.pallas_call(
        _s5_kernel,
        grid=(batch, tiles),
        in_specs=[pl.BlockSpec((t, D_SSM), lambda b, i: (b * tiles + i, 0))]
        + [_const_spec(c.shape) for c in consts],
        out_specs=pl.BlockSpec((t, D_SSM), lambda b, i: (b * tiles + i, 0)),
        out_shape=jax.ShapeDtypeStruct((n, D_SSM), BF16),
        scratch_shapes=[
            pltpu.VMEM((2, S5_HALF_IN, 2 * S5_HALF_ST), BF16),
            pltpu.VMEM((2, 2 * S5_HALF_ST, S5_HALF_IN), BF16),
            pltpu.VMEM((SUBLANES, N_STATE), F32),
            pltpu.VMEM((SUBLANES, N_STATE), F32),
            pltpu.VMEM((t, N_STATE), F32),
            pltpu.VMEM((t, N_STATE), F32),
            pltpu.VMEM((1, N_STATE), F32),
            pltpu.VMEM((1, N_STATE), F32),
        ],
        compiler_params=_cparams(("arbitrary", "arbitrary")),
        name="s5",
    )(u, *consts)


AT_TQ = 256
AT_TK = 128
INT_MIN = -2 ** 31


def _attn_kernel(qt_ref, qit_ref, wit_ref, k_ref, ki_ref, vt_ref, y_ref, score_ref, m_ref, l_ref, acc_ref,
                 lg_ref, *, seq_len, n_sel):
    j = pl.program_id(1)
    tq, tk = AT_TQ, AT_TK
    n_diag = tq // tk
    n_full = j * n_diag
    q_pos = j * tq + lax.broadcasted_iota(I32, (tk, tq), 1)
    k_iota = lax.broadcasted_iota(I32, (tk, tq), 0)

    def tile_start(kt):
        return pl.multiple_of(kt * tk, tk)

    def score_tile(kt):
        r0 = tile_start(kt)
        ki_t = ki_ref[pl.ds(r0, tk), :]
        acc = jnp.zeros((tk, tq), F32)
        for h in range(IDX_HEADS):
            s = jnp.dot(ki_t, qit_ref[h * IDX_DIM:(h + 1) * IDX_DIM, :], preferred_element_type=F32)
            acc = acc + wit_ref[h:h + 1, :] * jnp.maximum(s, 0.0)
        score_ref[pl.ds(r0, tk), :] = jnp.where(r0 + k_iota <= q_pos, acc, NEG)

    def score_pair(i, c):
        score_tile(2 * i)
        score_tile(2 * i + 1)
        return c

    n_tot = n_full + n_diag
    lax.fori_loop(0, n_tot // 2, score_pair, 0)

    n_beyond = seq_len - (j + 1) * tq

    def candidate(u):
        s = u ^ INT_MIN
        return lax.bitcast_convert_type(jnp.where(s < 0, s ^ 0x7FFFFFFF, s), F32)

    n_acc = 4

    def count_tile(c, cnts, cand):
        r0 = pl.multiple_of(c * tq, tq)
        sc = score_ref[pl.ds(r0, tq), :]
        cnts = list(cnts)
        for i in range(tq // SUBLANES):
            rows = sc[i * SUBLANES:(i + 1) * SUBLANES]
            cnts[i % n_acc] = jnp.where(rows >= cand, cnts[i % n_acc] + 1, cnts[i % n_acc])
        return tuple(cnts)

    def bit_step(b, thr_u):
        cand_u = thr_u | lax.shift_left(jnp.int32(1), 31 - b)
        cand = candidate(cand_u)
        zero = jnp.zeros((SUBLANES, tq), I32)
        cnts = lax.fori_loop(0, j + 1, functools.partial(count_tile, cand=cand), (zero,) * n_acc)
        cnt = (cnts[0] + cnts[1]) + (cnts[2] + cnts[3])
        total = jnp.sum(cnt, axis=0, keepdims=True) + jnp.where(cand <= NEG, n_beyond, 0)
        return jnp.where(total >= n_sel, cand_u, thr_u)

    thr = candidate(lax.fori_loop(0, 32, bit_step, jnp.zeros((1, tq), I32)))

    m_ref[...] = jnp.full(m_ref.shape, NEG, F32)
    l_ref[...] = jnp.zeros(l_ref.shape, F32)
    acc_ref[...] = jnp.zeros(acc_ref.shape, F32)
    rep = N_HEADS // N_KV_HEADS

    def logits(kt, slot):
        r0 = tile_start(kt)
        valid = jnp.logical_and(score_ref[pl.ds(r0, tk), :] >= thr, r0 + k_iota <= q_pos)
        bias = jnp.where(valid, 0.0, NEG)
        for g in range(N_KV_HEADS):
            k_t = k_ref[g, pl.ds(r0, tk), :]
            for h in range(g * rep, (g + 1) * rep):
                q_h = qt_ref[h * HEAD_DIM:(h + 1) * HEAD_DIM, :]
                lg_ref[slot, h] = jnp.dot(k_t, q_h, preferred_element_type=F32) + bias

    def softmax_values(kt, slot):
        r0 = tile_start(kt)
        for g in range(N_KV_HEADS):
            v_t = vt_ref[g * HEAD_DIM:(g + 1) * HEAD_DIM, pl.ds(r0, tk)]
            for h in range(g * rep, (g + 1) * rep):
                lg = lg_ref[slot, h]
                m_old = m_ref[h]
                m_new = jnp.maximum(m_old, jnp.max(lg, axis=0, keepdims=True))
                p = jnp.exp(lg - m_new)
                alpha = jnp.exp(m_old - m_new)
                l_ref[h] = alpha * l_ref[h] + jnp.sum(p, axis=0, keepdims=True)
                acc_ref[h] = alpha * acc_ref[h] + jnp.dot(v_t, p.astype(BF16), preferred_element_type=F32)
                m_ref[h] = m_new

    logits(0, 0)

    def tile_pair(i, c):
        logits(2 * i + 1, 1)
        softmax_values(2 * i, 0)
        logits(jnp.minimum(2 * i + 2, n_tot - 1), 0)
        softmax_values(2 * i + 1, 1)
        return c

    lax.fori_loop(0, n_tot // 2, tile_pair, 0)
    outs = [acc_ref[h] / l_ref[h] for h in range(N_HEADS)]
    y_ref[...] = jnp.concatenate(outs, axis=0).T.astype(BF16)


def _attention(qt, qit, wit, k, ki, vt, batch, seq_len):
    n = ki.shape[0]
    tq = AT_TQ
    nq = seq_len // tq
    n_sel = min(TOPK, seq_len // 4)
    qcol = lambda b, j: (0, b * nq + j)
    return pl.pallas_call(
        functools.partial(_attn_kernel, seq_len=seq_len, n_sel=n_sel),
        grid=(batch, nq),
        in_specs=[
            pl.BlockSpec((D_ATTN, tq), qcol),
            pl.BlockSpec((IDX_HEADS * IDX_DIM, tq), qcol),
            pl.BlockSpec((IDX_HEADS, tq), qcol),
            pl.BlockSpec((N_KV_HEADS, seq_len, HEAD_DIM), lambda b, j: (0, b, 0)),
            pl.BlockSpec((seq_len, IDX_DIM), lambda b, j: (b, 0)),
            pl.BlockSpec((D_KV, seq_len), lambda b, j: (0, b)),
        ],
        out_specs=pl.BlockSpec((tq, D_ATTN), lambda b, j: (b * nq + j, 0)),
        out_shape=jax.ShapeDtypeStruct((n, D_ATTN), BF16),
        scratch_shapes=[pltpu.VMEM((seq_len, tq), F32),
                        pltpu.VMEM((N_HEADS, 1, tq), F32),
                        pltpu.VMEM((N_HEADS, 1, tq), F32),
                        pltpu.VMEM((N_HEADS, HEAD_DIM, tq), F32),
                        pltpu.VMEM((2, N_HEADS, AT_TK, tq), F32)],
        compiler_params=_cparams(("arbitrary", "arbitrary")),
        name="attn",
    )(qt, qit, wit, k, ki, vt)


MIX_TM = 512


def _split_bf16(a):
    hi = a.astype(BF16)
    return hi, (a - hi.astype(F32)).astype(BF16)


def _mix_kernel(x_ref, ys_ref, ya_ref, gs_ref, ga_ref, wps_ref, wpa_ref, wo_ref, g2_ref,
                wr_hi_ref, wr_lo_ref, br_ref, x1_ref, h_ref, idx_ref, gate_ref):
    ps = jnp.dot(ys_ref[...], wps_ref[...], preferred_element_type=F32)
    pa = jnp.dot(ya_ref[...], wpa_ref[...], preferred_element_type=F32)
    mixed = gs_ref[...].astype(F32) * ps + ga_ref[...].astype(F32) * pa
    x1 = x_ref[...] + jnp.dot(mixed.astype(BF16), wo_ref[...], preferred_element_type=F32)
    h = x1 * lax.rsqrt(jnp.mean(x1 * x1, axis=-1, keepdims=True) + RMS_EPS) * g2_ref[...]
    tm = x1.shape[0]
    for j in range(D_MODEL // LANES):
        x1_ref[pl.ds(j, tm, stride=D_MODEL // LANES), :] = x1[:, j * LANES:(j + 1) * LANES]
        h_ref[pl.ds(j, tm, stride=D_MODEL // LANES), :] = h[:, j * LANES:(j + 1) * LANES]
    h_hi, h_lo = _split_bf16(h)
    logits = (jnp.dot(h_hi, wr_hi_ref[...], preferred_element_type=F32)
              + jnp.dot(h_hi, wr_lo_ref[...], preferred_element_type=F32)
              + jnp.dot(h_lo, wr_hi_ref[...], preferred_element_type=F32)) + br_ref[...]
    lane = lax.broadcasted_iota(I32, logits.shape, 1)
    rest = logits
    firsts, vals = [], []
    for _ in range(TOP_K_EXPERTS):
        m = jnp.max(rest, axis=-1, keepdims=True)
        first = jnp.minimum(jnp.min(jnp.where(rest == m, lane, N_EXPERTS), axis=-1, keepdims=True),
                            N_EXPERTS - 1)
        firsts.append(first)
        vals.append(m)
        rest = jnp.where(lane == first, -jnp.inf, rest)
    es = [jnp.exp(v - vals[0]) for v in vals]
    denom = es[0] + es[1] + es[2] + es[3]
    idx_ref[...] = jnp.concatenate(firsts, axis=1)
    gate_ref[...] = jnp.concatenate([e / denom for e in es], axis=1)


def _mix(x2, y_ssm, y_att, gs, ga, w_proj_ssm, w_proj_attn, w_out, norm2_g, w_router, b_router):
    n = x2.shape[0]
    tm = min(MIX_TM, n)
    chunks = D_MODEL // LANES
    row = lambda i: (i, 0)
    wr_hi, wr_lo = _split_bf16(w_router)
    consts = [w_proj_ssm.astype(BF16), w_proj_attn.astype(BF16), w_out.astype(BF16),
              norm2_g.reshape(1, D_MODEL), wr_hi, wr_lo, b_router.reshape(1, N_EXPERTS)]
    return pl.pallas_call(
        _mix_kernel,
        grid=(n // tm,),
        in_specs=[pl.BlockSpec((tm, D_MODEL), row), pl.BlockSpec((tm, D_SSM), row),
                  pl.BlockSpec((tm, D_ATTN), row), pl.BlockSpec((tm, D_MODEL), row),
                  pl.BlockSpec((tm, D_MODEL), row)] + [_const_spec(c.shape) for c in consts],
        out_specs=(pl.BlockSpec((tm * chunks, LANES), row), pl.BlockSpec((tm * chunks, LANES), row),
                   pl.BlockSpec((tm, TOP_K_EXPERTS), row), pl.BlockSpec((tm, TOP_K_EXPERTS), row)),
        out_shape=(jax.ShapeDtypeStruct((n * chunks, LANES), F32), jax.ShapeDtypeStruct((n * chunks, LANES), F32),
                   jax.ShapeDtypeStruct((n, TOP_K_EXPERTS), I32), jax.ShapeDtypeStruct((n, TOP_K_EXPERTS), F32)),
        compiler_params=_cparams(("arbitrary",)),
        name="mix",
    )(x2, y_ssm, y_att, gs, ga, *consts)


MOE_TT = 2048
MOE_RB = 288
ROW_CHUNKS = D_MODEL // LANES
assert ROW_CHUNKS == SUBLANES
MOE_PITCH = MOE_RB + SUBLANES


def _moe_kernel(cnt_ref, off_ref, tok_ref, gate_ref, h_ref, x1_ref, win_ref, bin_ref, wout_ref, bout_ref,
                gf_ref, out_ref, xt_ref, yt_ref):
    i, e = pl.program_id(0), pl.program_id(1)
    n = cnt_ref[i * N_EXPERTS + e]
    start = off_ref[i * N_EXPERTS + e]

    @pl.when(e == 0)
    def _():
        out_ref[...] = x1_ref[...]

    def slab(t):
        return pl.ds(pl.multiple_of(t * SUBLANES, SUBLANES), SUBLANES)

    def block(b, c):
        base = start + b * MOE_RB

        def gather8(r8, c2):
            for rr in range(SUBLANES):
                r = r8 * SUBLANES + rr
                t = tok_ref[0, 0, base + r]
                xt_ref[pl.ds(r, SUBLANES, stride=MOE_PITCH), :] = h_ref[slab(t), :]
            return c2

        lax.fori_loop(0, MOE_RB // SUBLANES, gather8, 0)
        xg = jnp.concatenate([xt_ref[j * MOE_PITCH:j * MOE_PITCH + MOE_RB, :] for j in range(ROW_CHUNKS)],
                             axis=1).astype(BF16)
        z = jnp.dot(xg, win_ref[0], preferred_element_type=F32) + bin_ref[0]
        half = MXU_COLS // 2
        groups = range(2 * D_EXPERT // MXU_COLS)
        zg = jnp.concatenate([z[:, c * MXU_COLS:c * MXU_COLS + half] for c in groups], axis=1)
        zl = jnp.concatenate([z[:, c * MXU_COLS + half:(c + 1) * MXU_COLS] for c in groups], axis=1)
        zg = jnp.minimum(zg, SWIGLU_LIMIT)
        zl = jnp.clip(zl, -SWIGLU_LIMIT, SWIGLU_LIMIT)
        act = zg * jax.nn.sigmoid(SWIGLU_ALPHA * zg) * (zl + 1.0)
        y = jnp.dot(act.astype(BF16), wout_ref[0], preferred_element_type=F32) + bout_ref[0]
        for j in range(ROW_CHUNKS):
            yt_ref[j * MOE_PITCH:j * MOE_PITCH + MOE_RB, :] = y[:, j * LANES:(j + 1) * LANES]

        def updated(r):
            t = tok_ref[0, 0, base + r]
            g = gate_ref[0, 0, base + r]
            return t, out_ref[slab(t), :] + g * yt_ref[pl.ds(r, SUBLANES, stride=MOE_PITCH), :]

        def scatter8(r8, c2):
            rows = [updated(r8 * SUBLANES + rr) for rr in range(SUBLANES)]
            for t, v in rows:
                out_ref[slab(t), :] = v
            return c2

        def scatter1(r, c2):
            t, v = updated(r)
            out_ref[slab(t), :] = v
            return c2

        n_rows = jnp.minimum(MOE_RB, n - b * MOE_RB)
        n_groups = n_rows // SUBLANES
        lax.fori_loop(0, n_groups, scatter8, 0)
        lax.fori_loop(n_groups * SUBLANES, n_rows, scatter1, 0)
        return c

    lax.fori_loop(0, (n + MOE_RB - 1) // MOE_RB, block, 0)

    @pl.when(e == N_EXPERTS - 1)
    def _():
        group = 256
        gf = gf_ref[...]

        def norm(c, c2):
            rows = pl.ds(pl.multiple_of(c * group * SUBLANES, group * SUBLANES), group * SUBLANES)
            v = out_ref[rows, :].reshape(group, SUBLANES, LANES)
            ss = jnp.sum(jnp.sum(v * v, axis=2, keepdims=True), axis=1, keepdims=True)
            o = v * lax.rsqrt(ss * (1.0 / D_MODEL) + RMS_EPS) * gf
            out_ref[rows, :] = o.reshape(group * SUBLANES, LANES)
            return c2

        lax.fori_loop(0, out_ref.shape[0] // (group * SUBLANES), norm, 0)


MXU_COLS = 256


def _expert_weights_kernel(win_ref, wout_ref, perm_ref, winp_ref, woutb_ref):
    w = win_ref[0].astype(BF16)
    for c in range(w.shape[1] // MXU_COLS):
        cols = slice(c * MXU_COLS, (c + 1) * MXU_COLS)
        winp_ref[0, :, cols] = jnp.dot(w[:, cols], perm_ref[...], preferred_element_type=F32).astype(BF16)
    woutb_ref[0] = wout_ref[0].astype(BF16)


def _expert_weights(w_moe_in, w_moe_out):
    half = MXU_COLS // 2
    src = jnp.concatenate([jnp.arange(half) * 2, jnp.arange(half) * 2 + 1])
    perm = (jnp.arange(MXU_COLS)[:, None] == src[None, :]).astype(BF16)
    per_expert = lambda a: pl.BlockSpec((1,) + a.shape[1:], lambda e: (e, 0, 0))
    return pl.pallas_call(
        _expert_weights_kernel,
        grid=(N_EXPERTS,),
        in_specs=[per_expert(w_moe_in), per_expert(w_moe_out), _const_spec(perm.shape)],
        out_specs=(per_expert(w_moe_in), per_expert(w_moe_out)),
        out_shape=(jax.ShapeDtypeStruct(w_moe_in.shape, BF16), jax.ShapeDtypeStruct(w_moe_out.shape, BF16)),
        compiler_params=_cparams(("arbitrary",)),
        name="expert_weights",
    )(w_moe_in, w_moe_out, perm)


def _moe(h, x1, idx4, gate4, w_moe_in, b_moe_in, w_moe_out, b_moe_out, norm_f_g):
    n = h.shape[0] // ROW_CHUNKS
    tt = min(MOE_TT, n)
    n_tiles = n // tt
    pairs = TOP_K_EXPERTS * tt
    cap = pairs + MOE_RB

    local = (jnp.arange(n, dtype=I32) % tt)[:, None]
    keys = (idx4 * tt + local).reshape(n_tiles, pairs)
    keys, gate = lax.sort((keys, gate4.reshape(n_tiles, pairs)), dimension=1, num_keys=1)
    bounds = jnp.arange(N_EXPERTS + 1, dtype=I32) * tt
    below = jnp.sum(keys[:, :, None] < bounds[None, None, :], axis=1).astype(I32)
    off, cnt = below[:, :-1], below[:, 1:] - below[:, :-1]
    pad = ((0, 0), (0, cap - pairs))
    tok = jnp.pad(keys % tt, pad).reshape(n_tiles, 1, cap)
    gate = jnp.pad(gate, pad).reshape(n_tiles, 1, cap)

    win, wout = _expert_weights(w_moe_in, w_moe_out)
    half = MXU_COLS // 2
    bin_ = jnp.swapaxes(b_moe_in.reshape(N_EXPERTS, -1, half, 2), 2, 3).reshape(N_EXPERTS, 1, 2 * D_EXPERT)
    bout = b_moe_out.reshape(N_EXPERTS, 1, D_MODEL)

    tile =pl.BlockSpec((tt * ROW_CHUNKS, LANES), lambda i, e, *_: (i, 0))
    tile1 = pl.BlockSpec((tt * ROW_CHUNKS, LANES), lambda i, e, *_: (i, 0), pipeline_mode=pl.Buffered(1))
    smem_list = pl.BlockSpec((1, 1, cap), lambda i, e, *_: (i, 0, 0), memory_space=pltpu.SMEM)
    per_expert = lambda shape: pl.BlockSpec((1,) + shape, lambda i, e, *_: (e, 0, 0))
    grid_spec = pltpu.PrefetchScalarGridSpec(
        num_scalar_prefetch=2,
        grid=(n_tiles, N_EXPERTS),
        in_specs=[smem_list, smem_list, tile1, tile1,
                  per_expert((D_MODEL, 2 * D_EXPERT)), per_expert((1, 2 * D_EXPERT)),
                  per_expert((D_EXPERT, D_MODEL)), per_expert((1, D_MODEL)),
                  pl.BlockSpec((SUBLANES, LANES), lambda i, e, *_: (0, 0))],
        out_specs=tile,
        scratch_shapes=[pltpu.VMEM((ROW_CHUNKS * MOE_PITCH, LANES), F32),
                        pltpu.VMEM((ROW_CHUNKS * MOE_PITCH, LANES), F32)],
    )
    out = pl.pallas_call(
        _moe_kernel,
        grid_spec=grid_spec,
        out_shape=jax.ShapeDtypeStruct((n * ROW_CHUNKS, LANES), F32),
        compiler_params=_cparams(("arbitrary", "arbitrary")),
        name="moe",
    )(cnt.reshape(-1), off.reshape(-1).astype(I32), tok, gate, h, x1, win, bin_, wout, bout,
      norm_f_g.reshape(SUBLANES, LANES))
    return out.reshape(n, D_MODEL)


def kernel(x, norm1_g, w_in, b_gate, ssm_lam_re, ssm_lam_im, ssm_log_dt, ssm_b_re, ssm_b_im, ssm_c_re, ssm_c_im, ssm_d, w_glu, b_glu, w_proj_ssm, w_proj_attn, w_out, norm2_g, w_router, b_router, w_moe_in, b_moe_in, w_moe_out, b_moe_out, norm_f_g):
    bsz, seq_len, _ = x.shape
    n = bsz * seq_len
    x2 = x.reshape(n, D_MODEL)
    u, qt, k, vt, qit, ki, wit, gs, ga = _in_proj(x2, norm1_g[0], w_in[0], b_gate[0], seq_len)
    y_ssm = _s5(u, bsz, seq_len, ssm_lam_re[0], ssm_lam_im[0], ssm_log_dt[0], ssm_b_re[0], ssm_b_im[0],
                ssm_c_re[0], ssm_c_im[0], ssm_d[0], w_glu[0], b_glu[0])
    y_att = _attention(qt, qit, wit, k, ki, vt, bsz, seq_len)
    x1, h, idx4, gate4 = _mix(x2, y_ssm, y_att, gs, ga, w_proj_ssm[0], w_proj_attn[0], w_out[0], norm2_g[0],
                              w_router[0], b_router[0])
    out = _moe(h, x1, idx4, gate4, w_moe_in[0], b_moe_in[0], w_moe_out[0], b_moe_out[0], norm_f_g)
    return out.reshape(x.shape)
```

```python
import functools
import math

import numpy as np
import jax
import jax.numpy as jnp
from jax import lax
from jax.experimental import pallas as pl
from jax.experimental.pallas import tpu as pltpu

D_MODEL = 1024
D_SSM = 512
SSM_GROUP = 16
N_SSM_GROUPS = 32
SSM_STATE = 64
N_HEADS = 8
N_KV_HEADS = 2
HEAD_DIM = 64
D_ATTN = N_HEADS * HEAD_DIM
D_KV = N_KV_HEADS * HEAD_DIM
IDX_HEADS = 16
IDX_DIM = 64
TOPK = 256
ROPE_THETA = 10000.0
N_EXPERTS = 32
TOP_K_EXPERTS = 4
D_EXPERT = D_MODEL
SWIGLU_LIMIT = 7.0
SWIGLU_ALPHA = 1.702
RMS_EPS = 1e-5
NEG = -1e30

LANES = 128
SUBLANES = 8
VMEM_LIMIT = 56 * 1024 * 1024

F32 = jnp.float32
BF16 = jnp.bfloat16
I32 = jnp.int32


def _cparams(sem):
    return pltpu.CompilerParams(dimension_semantics=sem, vmem_limit_bytes=VMEM_LIMIT)


def _const_spec(shape):
    nd = len(shape)
    return pl.BlockSpec(shape, lambda *_: (0,) * nd)


IN_TM = 512


def _in_proj_kernel(x_ref, g_ref, cos_ref, sin_ref, bg_ref,
                    wu_ref, wq_ref, wqr_ref, wkv_ref, wqi_ref, wqir_ref, wkw_ref, wgs_ref, wga_ref,
                    u_ref, qt_ref, k_ref, vt_ref, qit_ref, ki_ref, wit_ref, gs_ref, ga_ref):
    x = x_ref[...]
    xn = x * lax.rsqrt(jnp.mean(x * x, axis=-1, keepdims=True) + RMS_EPS) * g_ref[...]
    xb = xn.astype(BF16)

    def mm(w_ref):
        return jnp.dot(xb, w_ref[...], preferred_element_type=F32)

    cos = cos_ref[...]
    sin = sin_ref[...]

    def rope(a, ar):
        reps = a.shape[1] // LANES
        return a * jnp.tile(cos, (1, reps)) + ar * jnp.tile(sin, (1, reps))

    u_ref[...] = mm(wu_ref)
    q = rope(mm(wq_ref), mm(wqr_ref)) * (HEAD_DIM ** -0.5)
    qt_ref[...] = q.T.astype(BF16)
    kv = mm(wkv_ref)
    k = rope(kv[:, :D_KV], kv[:, D_KV:2 * D_KV])
    for g in range(N_KV_HEADS):
        k_ref[g] = k[:, g * HEAD_DIM:(g + 1) * HEAD_DIM].astype(BF16)
    vt_ref[...] = kv[:, 2 * D_KV:].T.astype(BF16)
    qi = rope(mm(wqi_ref), mm(wqir_ref)) * (IDX_DIM ** -0.5)
    qit_ref[...] = qi.T.astype(BF16)
    kw = mm(wkw_ref)
    ki = kw[:, :IDX_DIM] * cos[:, :IDX_DIM] + kw[:, IDX_DIM:2 * IDX_DIM] * sin[:, :IDX_DIM]
    ki_ref[...] = ki.astype(BF16)
    wit_ref[...] = (kw[:, LANES:2 * LANES].T)[:IDX_HEADS] * (IDX_HEADS ** -0.5)
    gs_ref[...] = jax.nn.sigmoid(mm(wgs_ref) + bg_ref[0:1, :]).astype(BF16)
    ga_ref[...] = jax.nn.sigmoid(mm(wga_ref) + bg_ref[1:2, :]).astype(BF16)


def _rot_half_cols(w, n_heads, d):
    k = w.shape[0]
    w4 = w.reshape(k, n_heads, 2, d // 2)
    return w4[:, :, ::-1, :].reshape(k, n_heads * d)


def _in_proj(x2, norm1_g, w_in, b_gate, seq_len):
    n = x2.shape[0]
    tm = min(IN_TM, seq_len)
    o = 0
    parts = []
    for width in (D_SSM, D_ATTN, D_KV, D_KV, IDX_HEADS * IDX_DIM, IDX_DIM, IDX_HEADS, D_MODEL, D_MODEL):
        parts.append(w_in[:, o:o + width])
        o += width
    wu, wq, wk, wv, wqi, wki, wwi, wgs, wga = parts
    wqr = _rot_half_cols(wq, N_HEADS, HEAD_DIM)
    wkr = _rot_half_cols(wk, N_KV_HEADS, HEAD_DIM)
    wqir = _rot_half_cols(wqi, IDX_HEADS, IDX_DIM)
    wkir = _rot_half_cols(wki, 1, IDX_DIM)
    wkv = jnp.concatenate([wk, wkr, wv], axis=1)
    wkw = jnp.concatenate([wki, wkir, wwi, jnp.zeros((D_MODEL, LANES - IDX_HEADS), F32)], axis=1)
    weights = [w.astype(BF16) for w in (wu, wq, wqr, wkv, wqi, wqir, wkw, wgs, wga)]

    half = HEAD_DIM // 2
    inv = ROPE_THETA ** (-jnp.arange(half, dtype=F32) / half)
    ang = jnp.arange(seq_len, dtype=F32)[:, None] * inv[None, :]
    cos = jnp.tile(jnp.cos(ang), (1, 4))
    sin = jnp.tile(jnp.concatenate([-jnp.sin(ang), jnp.sin(ang)], axis=1), (1, 2))

    tiles_per_seq = seq_len // tm
    row = lambda i: (i, 0)
    col = lambda i: (0, i)
    out_shapes = (
        jax.ShapeDtypeStruct((n, D_SSM), F32),
        jax.ShapeDtypeStruct((D_ATTN, n), BF16),
        jax.ShapeDtypeStruct((N_KV_HEADS, n, HEAD_DIM), BF16),
        jax.ShapeDtypeStruct((D_KV, n), BF16),
        jax.ShapeDtypeStruct((IDX_HEADS * IDX_DIM, n), BF16),
        jax.ShapeDtypeStruct((n, IDX_DIM), BF16),
        jax.ShapeDtypeStruct((IDX_HEADS, n), F32),
        jax.ShapeDtypeStruct((n, D_MODEL), BF16),
        jax.ShapeDtypeStruct((n, D_MODEL), BF16),
    )
    out_specs = (
        pl.BlockSpec((tm, D_SSM), row),
        pl.BlockSpec((D_ATTN, tm), col),
        pl.BlockSpec((N_KV_HEADS, tm, HEAD_DIM), lambda i: (0, i, 0)),
        pl.BlockSpec((D_KV, tm), col),
        pl.BlockSpec((IDX_HEADS * IDX_DIM, tm), col),
        pl.BlockSpec((tm, IDX_DIM), row),
        pl.BlockSpec((IDX_HEADS, tm), col),
        pl.BlockSpec((tm, D_MODEL), row),
        pl.BlockSpec((tm, D_MODEL), row),
    )
    in_specs = [
        pl.BlockSpec((tm, D_MODEL), row),
        _const_spec((1, D_MODEL)),
        pl.BlockSpec((tm, LANES), lambda i: (i % tiles_per_seq, 0)),
        pl.BlockSpec((tm, LANES), lambda i: (i % tiles_per_seq, 0)),
        _const_spec((2, D_MODEL)),
    ] + [_const_spec(w.shape) for w in weights]
    return pl.pallas_call(
        _in_proj_kernel,
        grid=(n // tm,),
        in_specs=in_specs,
        out_specs=out_specs,
        out_shape=out_shapes,
        compiler_params=_cparams(("arbitrary",)),
        name="in_proj",
    )(x2, norm1_g.reshape(1, D_MODEL), cos, sin, b_gate, *weights)


S5_T = 512
S5_CHUNK = 512
N_STATE = N_SSM_GROUPS * SSM_STATE
S5_HALF_IN = D_SSM // 2
S5_HALF_ST = N_STATE // 2


def _cmul(ar, ai, br, bi):
    return ar * br - ai * bi, ar * bi + ai * br


def _s5_kernel(u_ref, lr_ref, li_ref, ldt_ref, bre_ref, bim_ref, cre_ref, cim_ref, d_ref,
               wglu_ref, bglu_ref, y_ref,
               bb_ref, cc_ref, pw_re_ref, pw_im_ref, st_re_ref, st_im_ref, carry_re_ref, carry_im_ref):
    first = jnp.logical_and(pl.program_id(0) == 0, pl.program_id(1) == 0)

    @pl.when(first)
    def _():
        lr, li = lr_ref[...], li_ref[...]
        dt = jnp.exp(ldt_ref[...])
        mag = jnp.exp(lr * dt)
        ar = mag * jnp.cos(li * dt)
        ai = mag * jnp.sin(li * dt)
        den = lr * lr + li * li
        zr = ((ar - 1.0) * lr + ai * li) / den
        zi = (ai * lr - (ar - 1.0) * li) / den
        for h in range(2):
            rows = slice(h * S5_HALF_IN, (h + 1) * S5_HALF_IN)
            cols = slice(h * S5_HALF_ST, (h + 1) * S5_HALF_ST)
            bre, bim = bre_ref[rows, cols], bim_ref[rows, cols]
            bb_ref[h, :, :S5_HALF_ST] = (zr[:, cols] * bre - zi[:, cols] * bim).astype(BF16)
            bb_ref[h, :, S5_HALF_ST:] = (zr[:, cols] * bim + zi[:, cols] * bre).astype(BF16)
            cc_ref[h, :S5_HALF_ST, :] = cre_ref[cols, rows].astype(BF16)
            cc_ref[h, S5_HALF_ST:, :] = (-cim_ref[cols, rows]).astype(BF16)
        pr, pi = ar, ai
        pw_re_ref[0:1, :] = pr
        pw_im_ref[0:1, :] = pi
        for n in range(1, SUBLANES):
            pr, pi = _cmul(pr, pi, ar, ai)
            pw_re_ref[n:n + 1, :] = pr
            pw_im_ref[n:n + 1, :] = pi

    @pl.when(pl.program_id(1) == 0)
    def _():
        carry_re_ref[...] = jnp.zeros_like(carry_re_ref)
        carry_im_ref[...] = jnp.zeros_like(carry_im_ref)

    u = u_ref[...]
    ub = u.astype(BF16)
    for h in range(2):
        bu = jnp.dot(ub[:, h * S5_HALF_IN:(h + 1) * S5_HALF_IN], bb_ref[h], preferred_element_type=F32)
        st_re_ref[:, h * S5_HALF_ST:(h + 1) * S5_HALF_ST] = bu[:, :S5_HALF_ST]
        st_im_ref[:, h * S5_HALF_ST:(h + 1) * S5_HALF_ST] = bu[:, S5_HALF_ST:]

    t_len = u.shape[0]
    row = lax.broadcasted_iota(I32, (SUBLANES, S5_CHUNK), 0)
    for c in range(N_STATE // S5_CHUNK):
        lanes = slice(c * S5_CHUNK, (c + 1) * S5_CHUNK)
        p_re, p_im = pw_re_ref[:, lanes], pw_im_ref[:, lanes]
        steps = []
        for d in (1, 2, 4):
            a_re = jnp.where(row >= d, jnp.broadcast_to(p_re[d - 1:d, :], row.shape), 0.0)
            a_im = jnp.where(row >= d, jnp.broadcast_to(p_im[d - 1:d, :], row.shape), 0.0)
            steps.append((d, a_re, a_im))

        def block(i, carry):
            c_re, c_im = carry
            r0 = pl.multiple_of(i * SUBLANES, SUBLANES)
            xr = st_re_ref[pl.ds(r0, SUBLANES), lanes]
            xi = st_im_ref[pl.ds(r0, SUBLANES), lanes]
            for d, a_re, a_im in steps:
                sr = pltpu.roll(xr, d, 0)
                si = pltpu.roll(xi, d, 0)
                tr, ti = _cmul(a_re, a_im, sr, si)
                xr, xi = xr + tr, xi + ti
            tr, ti = _cmul(p_re, p_im, jnp.broadcast_to(c_re, xr.shape), jnp.broadcast_to(c_im, xi.shape))
            xr, xi = xr + tr, xi + ti
            st_re_ref[pl.ds(r0, SUBLANES), lanes] = xr
            st_im_ref[pl.ds(r0, SUBLANES), lanes] = xi
            return xr[SUBLANES - 1:, :], xi[SUBLANES - 1:, :]

        def two_blocks(j, carry):
            return block(2 * j + 1, block(2 * j, carry))

        c_re, c_im = lax.fori_loop(0, t_len // (2 * SUBLANES), two_blocks,
                                   (carry_re_ref[:, lanes], carry_im_ref[:, lanes]))
        carry_re_ref[:, lanes] = c_re
        carry_im_ref[:, lanes] = c_im

    ys = []
    for h in range(2):
        cols = slice(h * S5_HALF_ST, (h + 1) * S5_HALF_ST)
        xs = jnp.concatenate([st_re_ref[:, cols], st_im_ref[:, cols]], axis=1).astype(BF16)
        ys.append(jnp.dot(xs, cc_ref[h], preferred_element_type=F32))
    y = jnp.concatenate(ys, axis=1) + d_ref[...] * u
    y = jax.nn.gelu(y)
    gate = jnp.dot(y.astype(BF16), wglu_ref[...], preferred_element_type=F32) + bglu_ref[...]
    y_ref[...] = (y * jax.nn.sigmoid(gate)).astype(BF16)


def _s5(u, batch, seq_len, lam_re, lam_im, log_dt, b_re, b_im, c_re, c_im, d_skip, w_glu, b_glu):
    n = u.shape[0]
    t = min(S5_T, seq_len)
    g, p, hh = N_SSM_GROUPS, SSM_STATE, SSM_GROUP
    eye = jnp.eye(g, dtype=F32)
    bre_bd = jnp.einsum("gph,gk->ghkp", b_re, eye).reshape(D_SSM, N_STATE)
    bim_bd = jnp.einsum("gph,gk->ghkp", b_im, eye).reshape(D_SSM, N_STATE)
    cre_bd = jnp.einsum("ghp,gk->gpkh", c_re, eye).reshape(N_STATE, D_SSM)
    cim_bd = jnp.einsum("ghp,gk->gpkh", c_im, eye).reshape(N_STATE, D_SSM)
    lr = lam_re.reshape(1, N_STATE)
    li = lam_im.reshape(1, N_STATE)
    ldt = jnp.repeat(log_dt, p).reshape(1, N_STATE)
    tiles = seq_len // t
    consts = [lr, li, ldt, bre_bd, bim_bd, cre_bd, cim_bd, d_skip.reshape(1, D_SSM),
              w_glu.astype(BF16), b_glu.reshape(1, D_SSM)]
    return pl.pallas_call(
        _s5_kernel,
        grid=(batch, tiles),
        in_specs=[pl.BlockSpec((t, D_SSM), lambda b, i: (b * tiles + i, 0))]
        + [_const_spec(c.shape) for c in consts],
        out_specs=pl.BlockSpec((t, D_SSM), lambda b, i: (b * tiles + i, 0)),
        out_shape=jax.ShapeDtypeStruct((n, D_SSM), BF16),
        scratch_shapes=[
            pltpu.VMEM((2, S5_HALF_IN, 2 * S5_HALF_ST), BF16),
            pltpu.VMEM((2, 2 * S5_HALF_ST, S5_HALF_IN), BF16),
            pltpu.VMEM((SUBLANES, N_STATE), F32),
            pltpu.VMEM((SUBLANES, N_STATE), F32),
            pltpu.VMEM((t, N_STATE), F32),
            pltpu.VMEM((t, N_STATE), F32),
            pltpu.VMEM((1, N_STATE), F32),
            pltpu.VMEM((1, N_STATE), F32),
        ],
        compiler_params=_cparams(("arbitrary", "arbitrary")),
        name="s5",
    )(u, *consts)


AT_TQ = 256
AT_TK = 128
INT_MIN = -2 ** 31


def _attn_kernel(qt_ref, qit_ref, wit_ref, k_ref, ki_ref, vt_ref, y_ref, score_ref, m_ref, l_ref, acc_ref,
                 lg_ref, *, seq_len, n_sel):
    j = pl.program_id(1)
    tq, tk = AT_TQ, AT_TK
    n_diag = tq // tk
    n_full = j * n_diag
    q_pos = j * tq + lax.broadcasted_iota(I32, (tk, tq), 1)
    k_iota = lax.broadcasted_iota(I32, (tk, tq), 0)

    def tile_start(kt):
        return pl.multiple_of(kt * tk, tk)

    def score_tile(kt):
        r0 = tile_start(kt)
        ki_t = ki_ref[pl.ds(r0, tk), :]
        acc = jnp.zeros((tk, tq), F32)
        for h in range(IDX_HEADS):
            s = jnp.dot(ki_t, qit_ref[h * IDX_DIM:(h + 1) * IDX_DIM, :], preferred_element_type=F32)
            acc = acc + wit_ref[h:h + 1, :] * jnp.maximum(s, 0.0)
        score_ref[pl.ds(r0, tk), :] = jnp.where(r0 + k_iota <= q_pos, acc, NEG)

    def score_pair(i, c):
        score_tile(2 * i)
        score_tile(2 * i + 1)
        return c

    n_tot = n_full + n_diag
    lax.fori_loop(0, n_tot // 2, score_pair, 0)

    n_beyond = seq_len - (j + 1) * tq

    def candidate(u):
        s = u ^ INT_MIN
        return lax.bitcast_convert_type(jnp.where(s < 0, s ^ 0x7FFFFFFF, s), F32)

    n_acc = 4

    def count_tile(c, cnts, cand):
        r0 = pl.multiple_of(c * tq, tq)
        sc = score_ref[pl.ds(r0, tq), :]
        cnts = list(cnts)
        for i in range(tq // SUBLANES):
            rows = sc[i * SUBLANES:(i + 1) * SUBLANES]
            cnts[i % n_acc] = jnp.where(rows >= cand, cnts[i % n_acc] + 1, cnts[i % n_acc])
        return tuple(cnts)

    def bit_step(b, thr_u):
        cand_u = thr_u | lax.shift_left(jnp.int32(1), 31 - b)
        cand = candidate(cand_u)
        zero = jnp.zeros((SUBLANES, tq), I32)
        cnts = lax.fori_loop(0, j + 1, functools.partial(count_tile, cand=cand), (zero,) * n_acc)
        cnt = (cnts[0] + cnts[1]) + (cnts[2] + cnts[3])
        total = jnp.sum(cnt, axis=0, keepdims=True) + jnp.where(cand <= NEG, n_beyond, 0)
        return jnp.where(total >= n_sel, cand_u, thr_u)

    thr = candidate(lax.fori_loop(0, 32, bit_step, jnp.zeros((1, tq), I32)))

    m_ref[...] = jnp.full(m_ref.shape, NEG, F32)
    l_ref[...] = jnp.zeros(l_ref.shape, F32)
    acc_ref[...] = jnp.zeros(acc_ref.shape, F32)
    rep = N_HEADS // N_KV_HEADS

    def logits(kt, slot):
        r0 = tile_start(kt)
        valid = jnp.logical_and(score_ref[pl.ds(r0, tk), :] >= thr, r0 + k_iota <= q_pos)
        bias = jnp.where(valid, 0.0, NEG)
        for g in range(N_KV_HEADS):
            k_t = k_ref[g, pl.ds(r0, tk), :]
            for h in range(g * rep, (g + 1) * rep):
                q_h = qt_ref[h * HEAD_DIM:(h + 1) * HEAD_DIM, :]
                lg_ref[slot, h] = jnp.dot(k_t, q_h, preferred_element_type=F32) + bias

    def softmax_values(kt, slot):
        r0 = tile_start(kt)
        for g in range(N_KV_HEADS):
            v_t = vt_ref[g * HEAD_DIM:(g + 1) * HEAD_DIM, pl.ds(r0, tk)]
            for h in range(g * rep, (g + 1) * rep):
                lg = lg_ref[slot, h]
                m_old = m_ref[h]
                m_new = jnp.maximum(m_old, jnp.max(lg, axis=0, keepdims=True))
                p = jnp.exp(lg - m_new)
                alpha = jnp.exp(m_old - m_new)
                l_ref[h] = alpha * l_ref[h] + jnp.sum(p, axis=0, keepdims=True)
                acc_ref[h] = alpha * acc_ref[h] + jnp.dot(v_t, p.astype(BF16), preferred_element_type=F32)
                m_ref[h] = m_new

    logits(0, 0)

    def tile_pair(i, c):
        logits(2 * i + 1, 1)
        softmax_values(2 * i, 0)
        logits(jnp.minimum(2 * i + 2, n_tot - 1), 0)
        softmax_values(2 * i + 1, 1)
        return c

    lax.fori_loop(0, n_tot // 2, tile_pair, 0)
    outs = [acc_ref[h] / l_ref[h] for h in range(N_HEADS)]
    y_ref[...] = jnp.concatenate(outs, axis=0).T.astype(BF16)


def _attention(qt, qit, wit, k, ki, vt, batch, seq_len):
    n = ki.shape[0]
    tq = AT_TQ
    nq = seq_len // tq
    n_sel = min(TOPK, seq_len // 4)
    qcol = lambda b, j: (0, b * nq + j)
    return pl.pallas_call(
        functools.partial(_attn_kernel, seq_len=seq_len, n_sel=n_sel),
        grid=(batch, nq),
        in_specs=[
            pl.BlockSpec((D_ATTN, tq), qcol),
            pl.BlockSpec((IDX_HEADS * IDX_DIM, tq), qcol),
            pl.BlockSpec((IDX_HEADS, tq), qcol),
            pl.BlockSpec((N_KV_HEADS, seq_len, HEAD_DIM), lambda b, j: (0, b, 0)),
            pl.BlockSpec((seq_len, IDX_DIM), lambda b, j: (b, 0)),
            pl.BlockSpec((D_KV, seq_len), lambda b, j: (0, b)),
        ],
        out_specs=pl.BlockSpec((tq, D_ATTN), lambda b, j: (b * nq + j, 0)),
        out_shape=jax.ShapeDtypeStruct((n, D_ATTN), BF16),
        scratch_shapes=[pltpu.VMEM((seq_len, tq), F32),
                        pltpu.VMEM((N_HEADS, 1, tq), F32),
                        pltpu.VMEM((N_HEADS, 1, tq), F32),
                        pltpu.VMEM((N_HEADS, HEAD_DIM, tq), F32),
                        pltpu.VMEM((2, N_HEADS, AT_TK, tq), F32)],
        compiler_params=_cparams(("arbitrary", "arbitrary")),
        name="attn",
    )(qt, qit, wit, k, ki, vt)


MIX_TM = 512


def _split_bf16(a):
    hi = a.astype(BF16)
    return hi, (a - hi.astype(F32)).astype(BF16)


def _mix_kernel(x_ref, ys_ref, ya_ref, gs_ref, ga_ref, wps_ref, wpa_ref, wo_ref, g2_ref,
                wr_hi_ref, wr_lo_ref, br_ref, x1_ref, h_ref, idx_ref, gate_ref):
    ps = jnp.dot(ys_ref[...], wps_ref[...], preferred_element_type=F32)
    pa = jnp.dot(ya_ref[...], wpa_ref[...], preferred_element_type=F32)
    mixed = gs_ref[...].astype(F32) * ps + ga_ref[...].astype(F32) * pa
    x1 = x_ref[...] + jnp.dot(mixed.astype(BF16), wo_ref[...], preferred_element_type=F32)
    h = x1 * lax.rsqrt(jnp.mean(x1 * x1, axis=-1, keepdims=True) + RMS_EPS) * g2_ref[...]
    tm = x1.shape[0]
    for j in range(D_MODEL // LANES):
        x1_ref[pl.ds(j, tm, stride=D_MODEL // LANES), :] = x1[:, j * LANES:(j + 1) * LANES]
        h_ref[pl.ds(j, tm, stride=D_MODEL // LANES), :] = h[:, j * LANES:(j + 1) * LANES]
    h_hi, h_lo = _split_bf16(h)
    logits = (jnp.dot(h_hi, wr_hi_ref[...], preferred_element_type=F32)
              + jnp.dot(h_hi, wr_lo_ref[...], preferred_element_type=F32)
              + jnp.dot(h_lo, wr_hi_ref[...], preferred_element_type=F32)) + br_ref[...]
    lane = lax.broadcasted_iota(I32, logits.shape, 1)
    rest = logits
    firsts, vals = [], []
    for _ in range(TOP_K_EXPERTS):
        m = jnp.max(rest, axis=-1, keepdims=True)
        first = jnp.minimum(jnp.min(jnp.where(rest == m, lane, N_EXPERTS), axis=-1, keepdims=True),
                            N_EXPERTS - 1)
        firsts.append(first)
        vals.append(m)
        rest = jnp.where(lane == first, -jnp.inf, rest)
    es = [jnp.exp(v - vals[0]) for v in vals]
    denom = es[0] + es[1] + es[2] + es[3]
    idx_ref[...] = jnp.concatenate(firsts, axis=1)
    gate_ref[...] = jnp.concatenate([e / denom for e in es], axis=1)


def _mix(x2, y_ssm, y_att, gs, ga, w_proj_ssm, w_proj_attn, w_out, norm2_g, w_router, b_router):
    n = x2.shape[0]
    tm = min(MIX_TM, n)
    chunks = D_MODEL // LANES
    row = lambda i: (i, 0)
    wr_hi, wr_lo = _split_bf16(w_router)
    consts = [w_proj_ssm.astype(BF16), w_proj_attn.astype(BF16), w_out.astype(BF16),
              norm2_g.reshape(1, D_MODEL), wr_hi, wr_lo, b_router.reshape(1, N_EXPERTS)]
    return pl.pallas_call(
        _mix_kernel,
        grid=(n // tm,),
        in_specs=[pl.BlockSpec((tm, D_MODEL), row), pl.BlockSpec((tm, D_SSM), row),
                  pl.BlockSpec((tm, D_ATTN), row), pl.BlockSpec((tm, D_MODEL), row),
                  pl.BlockSpec((tm, D_MODEL), row)] + [_const_spec(c.shape) for c in consts],
        out_specs=(pl.BlockSpec((tm * chunks, LANES), row), pl.BlockSpec((tm * chunks, LANES), row),
                   pl.BlockSpec((tm, TOP_K_EXPERTS), row), pl.BlockSpec((tm, TOP_K_EXPERTS), row)),
        out_shape=(jax.ShapeDtypeStruct((n * chunks, LANES), F32), jax.ShapeDtypeStruct((n * chunks, LANES), F32),
                   jax.ShapeDtypeStruct((n, TOP_K_EXPERTS), I32), jax.ShapeDtypeStruct((n, TOP_K_EXPERTS), F32)),
        compiler_params=_cparams(("arbitrary",)),
        name="mix",
    )(x2, y_ssm, y_att, gs, ga, *consts)


MOE_TT = 2048
MOE_RB = 288
ROW_CHUNKS = D_MODEL // LANES
assert ROW_CHUNKS == SUBLANES
MOE_PITCH = MOE_RB + SUBLANES


def _moe_kernel(cnt_ref, off_ref, tok_ref, gate_ref, h_ref, win_ref, bin_ref, wout_ref, bout_ref,
                out_ref, xa_ref, xb_ref, xx_ref, ya_ref, yb_ref, yx_ref):
    i, e = pl.program_id(0), pl.program_id(1)
    trash = out_ref.shape[0] // SUBLANES - 1
    here = i * N_EXPERTS + e
    n, start = cnt_ref[here], off_ref[here]
    start_next = off_ref[i * N_EXPERTS + jnp.minimum(e + 1, N_EXPERTS - 1)]
    prev = i * N_EXPERTS + jnp.maximum(e - 1, 0)
    start_prev = off_ref[prev]
    n_prev = jnp.where(e > 0, jnp.minimum(cnt_ref[prev], MOE_RB), 0)

    def slab(t):
        return pl.ds(pl.multiple_of(t * SUBLANES, SUBLANES), SUBLANES)

    def tile_rows(r):
        return pl.ds(r, SUBLANES, stride=MOE_PITCH)

    def gather_row(base, r, x_ref):
        x_ref[tile_rows(r), :] = h_ref[slab(tok_ref[0, 0, base + r]), :]

    def gather_loop(base, x_ref):
        def body(r8, c):
            for rr in range(SUBLANES):
                gather_row(base, r8 * SUBLANES + rr, x_ref)
            return c

        lax.fori_loop(0, MOE_RB // SUBLANES, body, 0)

    def expert_mlp(x_ref, y_ref):
        xg = jnp.concatenate([x_ref[j * MOE_PITCH:j * MOE_PITCH + MOE_RB, :] for j in range(ROW_CHUNKS)],
                             axis=1).astype(BF16)
        z = jnp.dot(xg, win_ref[0], preferred_element_type=F32) + bin_ref[0]
        half = MXU_COLS // 2
        groups = range(2 * D_EXPERT // MXU_COLS)
        zg = jnp.concatenate([z[:, c * MXU_COLS:c * MXU_COLS + half] for c in groups], axis=1)
        zl = jnp.concatenate([z[:, c * MXU_COLS + half:(c + 1) * MXU_COLS] for c in groups], axis=1)
        zg = jnp.minimum(zg, SWIGLU_LIMIT)
        zl = jnp.clip(zl, -SWIGLU_LIMIT, SWIGLU_LIMIT)
        act = zg * jax.nn.sigmoid(SWIGLU_ALPHA * zg) * (zl + 1.0)
        y = jnp.dot(act.astype(BF16), wout_ref[0], preferred_element_type=F32) + bout_ref[0]
        for j in range(ROW_CHUNKS):
            y_ref[j * MOE_PITCH:j * MOE_PITCH + MOE_RB, :] = y[:, j * LANES:(j + 1) * LANES]

    def updated(base, r, n_valid, y_ref):
        t = jnp.where(r < n_valid, tok_ref[0, 0, base + r], trash)
        return t, out_ref[slab(t), :] + gate_ref[0, 0, base + r] * y_ref[tile_rows(r), :]

    def scatter_group(base, r0, n_valid, y_ref):
        rows = [updated(base, r0 + rr, n_valid, y_ref) for rr in range(SUBLANES)]
        for t, v in rows:
            out_ref[slab(t), :] = v

    def scatter_loop(base, n_valid, y_ref):
        def body(r8, c):
            scatter_group(base, r8 * SUBLANES, n_valid, y_ref)
            return c

        lax.fori_loop(0, (n_valid + SUBLANES - 1) // SUBLANES, body, 0)

    @pl.when(e == 0)
    def _():
        out_ref[...] = jnp.zeros(out_ref.shape, F32)
        gather_loop(start, xa_ref)

    @pl.when(jnp.logical_and(i == 0, e == 0))
    def _():
        yb_ref[...] = jnp.zeros(yb_ref.shape, F32)

    def pipelined(x_cur, x_next, y_cur, y_prev):
        for r in range(MOE_RB):
            gather_row(start_next, r, x_next)
        expert_mlp(x_cur, y_cur)
        for r0 in range(0, MOE_RB, SUBLANES):
            scatter_group(start_prev, r0, n_prev, y_prev)

    @pl.when(e % 2 == 0)
    def _():
        pipelined(xa_ref, xb_ref, ya_ref, yb_ref)

    @pl.when(e % 2 == 1)
    def _():
        pipelined(xb_ref, xa_ref, yb_ref, ya_ref)

    def extra_block(b, c):
        base = start + b * MOE_RB
        gather_loop(base, xx_ref)
        expert_mlp(xx_ref, yx_ref)
        scatter_loop(base, jnp.minimum(MOE_RB, n - b * MOE_RB), yx_ref)
        return c

    lax.fori_loop(1, (n + MOE_RB - 1) // MOE_RB, extra_block, 0)

    @pl.when(e == N_EXPERTS - 1)
    def _():
        scatter_loop(start, jnp.minimum(n, MOE_RB), yb_ref)


FINAL_TM = 512


def _final_kernel(x1_ref, moe_ref, g_ref, o_ref):
    tm = o_ref.shape[0]
    v = jnp.concatenate([x1_ref[pl.ds(j, tm, stride=ROW_CHUNKS), :] + moe_ref[0, pl.ds(j, tm, stride=ROW_CHUNKS), :]
                         for j in range(ROW_CHUNKS)], axis=1)
    o_ref[...] = v * lax.rsqrt(jnp.mean(v * v, axis=-1, keepdims=True) + RMS_EPS) * g_ref[...]


MXU_COLS = 256


def _expert_weights_kernel(win_ref, wout_ref, perm_ref, winp_ref, woutb_ref):
    w = win_ref[0].astype(BF16)
    for c in range(w.shape[1] // MXU_COLS):
        cols = slice(c * MXU_COLS, (c + 1) * MXU_COLS)
        winp_ref[0, :, cols] = jnp.dot(w[:, cols], perm_ref[...], preferred_element_type=F32).astype(BF16)
    woutb_ref[0] = wout_ref[0].astype(BF16)


def _expert_weights(w_moe_in, w_moe_out):
    half = MXU_COLS // 2
    src = jnp.concatenate([jnp.arange(half) * 2, jnp.arange(half) * 2 + 1])
    perm = (jnp.arange(MXU_COLS)[:, None] == src[None, :]).astype(BF16)
    per_expert = lambda a: pl.BlockSpec((1,) + a.shape[1:], lambda e: (e, 0, 0))
    return pl.pallas_call(
        _expert_weights_kernel,
        grid=(N_EXPERTS,),
        in_specs=[per_expert(w_moe_in), per_expert(w_moe_out), _const_spec(perm.shape)],
        out_specs=(per_expert(w_moe_in), per_expert(w_moe_out)),
        out_shape=(jax.ShapeDtypeStruct(w_moe_in.shape, BF16), jax.ShapeDtypeStruct(w_moe_out.shape, BF16)),
        compiler_params=_cparams(("arbitrary",)),
        name="expert_weights",
    )(w_moe_in, w_moe_out, perm)


def _moe(h, x1, idx4, gate4, w_moe_in, b_moe_in, w_moe_out, b_moe_out, norm_f_g):
    n = h.shape[0] // ROW_CHUNKS
    tt = min(MOE_TT, n)
    n_tiles = n // tt
    pairs = TOP_K_EXPERTS * tt
    cap = pairs + MOE_RB

    local = (jnp.arange(n, dtype=I32) % tt)[:, None]
    keys = (idx4 * tt + local).reshape(n_tiles, pairs)
    keys, gate = lax.sort((keys, gate4.reshape(n_tiles, pairs)), dimension=1, num_keys=1)
    bounds = jnp.arange(N_EXPERTS + 1, dtype=I32) * tt
    below = jnp.sum(keys[:, :, None] < bounds[None, None, :], axis=1).astype(I32)
    off, cnt = below[:, :-1], below[:, 1:] - below[:, :-1]
    pad = ((0, 0), (0, cap - pairs))
    tok = jnp.pad(keys % tt, pad).reshape(n_tiles, 1, cap)
    gate = jnp.pad(gate, pad).reshape(n_tiles, 1, cap)

    win, wout = _expert_weights(w_moe_in, w_moe_out)
    half = MXU_COLS // 2
    bin_ = jnp.swapaxes(b_moe_in.reshape(N_EXPERTS, -1, half, 2), 2, 3).reshape(N_EXPERTS, 1, 2 * D_EXPERT)
    bout = b_moe_out.reshape(N_EXPERTS, 1, D_MODEL)

    out_rows = (tt + 1) * ROW_CHUNKS
    tile1 = pl.BlockSpec((tt * ROW_CHUNKS, LANES), lambda i, e, *_: (i, 0), pipeline_mode=pl.Buffered(1))
    smem_list = pl.BlockSpec((1, 1, cap), lambda i, e, *_: (i, 0, 0), memory_space=pltpu.SMEM)
    per_expert = lambda shape: pl.BlockSpec((1,) + shape, lambda i, e, *_: (e, 0, 0))
    row_tile = pltpu.VMEM((ROW_CHUNKS * MOE_PITCH, LANES), F32)
    grid_spec = pltpu.PrefetchScalarGridSpec(
        num_scalar_prefetch=2,
        grid=(n_tiles, N_EXPERTS),
        in_specs=[smem_list, smem_list, tile1,
                  per_expert((D_MODEL, 2 * D_EXPERT)), per_expert((1, 2 * D_EXPERT)),
                  per_expert((D_EXPERT, D_MODEL)), per_expert((1, D_MODEL))],
        out_specs=pl.BlockSpec((out_rows, LANES), lambda i, e, *_: (i, 0)),
        scratch_shapes=[row_tile] * 6,
    )
    moe_out = pl.pallas_call(
        _moe_kernel,
        grid_spec=grid_spec,
        out_shape=jax.ShapeDtypeStruct((n_tiles * out_rows, LANES), F32),
        compiler_params=_cparams(("arbitrary", "arbitrary")),
        name="moe",
    )(cnt.reshape(-1), off.reshape(-1).astype(I32), tok, gate, h, win, bin_, wout, bout)

    tm = min(FINAL_TM, tt)
    per_tile = tt // tm
    return pl.pallas_call(
        _final_kernel,
        grid=(n_tiles, per_tile),
        in_specs=[pl.BlockSpec((tm * ROW_CHUNKS, LANES), lambda i, j: (i * per_tile + j, 0)),
                  pl.BlockSpec((1, tm * ROW_CHUNKS, LANES), lambda i, j: (i, j, 0)),
                  _const_spec((1, D_MODEL))],
        out_specs=pl.BlockSpec((tm, D_MODEL), lambda i, j: (i * per_tile + j, 0)),
        out_shape=jax.ShapeDtypeStruct((n, D_MODEL), F32),
        compiler_params=_cparams(("arbitrary", "arbitrary")),
        name="final_norm",
    )(x1, moe_out.reshape(n_tiles, out_rows, LANES), norm_f_g.reshape(1, D_MODEL))


def kernel(x, norm1_g, w_in, b_gate, ssm_lam_re, ssm_lam_im, ssm_log_dt, ssm_b_re, ssm_b_im, ssm_c_re, ssm_c_im, ssm_d, w_glu, b_glu, w_proj_ssm, w_proj_attn, w_out, norm2_g, w_router, b_router, w_moe_in, b_moe_in, w_moe_out, b_moe_out, norm_f_g):
    bsz, seq_len, _ = x.shape
    n = bsz * seq_len
    x2 = x.reshape(n, D_MODEL)
    u, qt, k, vt, qit, ki, wit, gs, ga = _in_proj(x2, norm1_g[0], w_in[0], b_gate[0], seq_len)
    y_ssm = _s5(u, bsz, seq_len, ssm_lam_re[0], ssm_lam_im[0], ssm_log_dt[0], ssm_b_re[0], ssm_b_im[0],
                ssm_c_re[0], ssm_c_im[0], ssm_d[0], w_glu[0], b_glu[0])
    y_att = _attention(qt, qit, wit, k, ki, vt, bsz, seq_len)
    x1, h, idx4, gate4 = _mix(x2, y_ssm, y_att, gs, ga, w_proj_ssm[0], w_proj_attn[0], w_out[0], norm2_g[0],
                              w_router[0], b_router[0])
    out = _moe(h, x1, idx4, gate4, w_moe_in[0], b_moe_in[0], w_moe_out[0], b_moe_out[0], norm_f_g)
    return out.reshape(x.shape)
```

```python
import functools
import math

import numpy as np
import jax
import jax.numpy as jnp
from jax import lax
from jax.experimental import pallas as pl
from jax.experimental.pallas import tpu as pltpu

D_MODEL = 1024
D_SSM = 512
SSM_GROUP = 16
N_SSM_GROUPS = 32
SSM_STATE = 64
N_HEADS = 8
N_KV_HEADS = 2
HEAD_DIM = 64
D_ATTN = N_HEADS * HEAD_DIM
D_KV = N_KV_HEADS * HEAD_DIM
IDX_HEADS = 16
IDX_DIM = 64
TOPK = 256
ROPE_THETA = 10000.0
N_EXPERTS = 32
TOP_K_EXPERTS = 4
D_EXPERT = D_MODEL
SWIGLU_LIMIT = 7.0
SWIGLU_ALPHA = 1.702
RMS_EPS = 1e-5
NEG = -1e30

LANES = 128
SUBLANES = 8
VMEM_LIMIT = 56 * 1024 * 1024

F32 = jnp.float32
BF16 = jnp.bfloat16
I32 = jnp.int32


def _cparams(sem):
    return pltpu.CompilerParams(dimension_semantics=sem, vmem_limit_bytes=VMEM_LIMIT)


def _const_spec(shape):
    nd = len(shape)
    return pl.BlockSpec(shape, lambda *_: (0,) * nd)


IN_TM = 512


def _in_proj_kernel(x_ref, g_ref, cos_ref, sin_ref, bg_ref,
                    wu_ref, wq_ref, wqr_ref, wkv_ref, wqi_ref, wqir_ref, wkw_ref, wgs_ref, wga_ref,
                    u_ref, qt_ref, k_ref, vt_ref, qit_ref, ki_ref, wit_ref, gs_ref, ga_ref):
    x = x_ref[...]
    xn = x * lax.rsqrt(jnp.mean(x * x, axis=-1, keepdims=True) + RMS_EPS) * g_ref[...]
    xb = xn.astype(BF16)

    def mm(w_ref):
        return jnp.dot(xb, w_ref[...], preferred_element_type=F32)

    cos = cos_ref[...]
    sin = sin_ref[...]

    def rope(a, ar):
        reps = a.shape[1] // LANES
        return a * jnp.tile(cos, (1, reps)) + ar * jnp.tile(sin, (1, reps))

    u_ref[...] = mm(wu_ref)
    q = rope(mm(wq_ref), mm(wqr_ref)) * (HEAD_DIM ** -0.5 * math.log2(math.e))
    qt_ref[...] = q.T.astype(BF16)
    kv = mm(wkv_ref)
    k = rope(kv[:, :D_KV], kv[:, D_KV:2 * D_KV])
    for g in range(N_KV_HEADS):
        k_ref[g] = k[:, g * HEAD_DIM:(g + 1) * HEAD_DIM].astype(BF16)
    vt_ref[...] = kv[:, 2 * D_KV:].T.astype(BF16)
    kw = mm(wkw_ref)
    ki = kw[:, :IDX_DIM] * cos[:, :IDX_DIM] + kw[:, IDX_DIM:2 * IDX_DIM] * sin[:, :IDX_DIM]
    ki_ref[...] = ki.astype(BF16)
    qi = rope(mm(wqi_ref), mm(wqir_ref)) * (IDX_DIM ** -0.5)
    qit_ref[...] = qi.T.astype(BF16)
    wit_ref[...] = (kw[:, LANES:2 * LANES].T)[:IDX_HEADS] * (IDX_HEADS ** -0.5)
    gs_ref[...] = jax.nn.sigmoid(mm(wgs_ref) + bg_ref[0:1, :]).astype(BF16)
    ga_ref[...] = jax.nn.sigmoid(mm(wga_ref) + bg_ref[1:2, :]).astype(BF16)


def _rot_half_cols(w, n_heads, d):
    k = w.shape[0]
    w4 = w.reshape(k, n_heads, 2, d // 2)
    return w4[:, :, ::-1, :].reshape(k, n_heads * d)


def _in_proj(x2, norm1_g, w_in, b_gate, seq_len):
    n = x2.shape[0]
    tm = min(IN_TM, seq_len)
    o = 0
    parts = []
    for width in (D_SSM, D_ATTN, D_KV, D_KV, IDX_HEADS * IDX_DIM, IDX_DIM, IDX_HEADS, D_MODEL, D_MODEL):
        parts.append(w_in[:, o:o + width])
        o += width
    wu, wq, wk, wv, wqi, wki, wwi, wgs, wga = parts
    wqr = _rot_half_cols(wq, N_HEADS, HEAD_DIM)
    wkr = _rot_half_cols(wk, N_KV_HEADS, HEAD_DIM)
    wqir = _rot_half_cols(wqi, IDX_HEADS, IDX_DIM)
    wkir = _rot_half_cols(wki, 1, IDX_DIM)
    wkv = jnp.concatenate([wk, wkr, wv], axis=1)
    wkw = jnp.concatenate([wki, wkir, wwi, jnp.zeros((D_MODEL, LANES - IDX_HEADS), F32)], axis=1)
    weights = [w.astype(BF16) for w in (wu, wq, wqr, wkv, wqi, wqir, wkw, wgs, wga)]

    half = HEAD_DIM // 2
    inv = ROPE_THETA ** (-jnp.arange(half, dtype=F32) / half)
    ang = jnp.arange(seq_len, dtype=F32)[:, None] * inv[None, :]
    cos = jnp.tile(jnp.cos(ang), (1, 4))
    sin = jnp.tile(jnp.concatenate([-jnp.sin(ang), jnp.sin(ang)], axis=1), (1, 2))

    tiles_per_seq = seq_len // tm
    row = lambda i: (i, 0)
    col = lambda i: (0, i)
    out_shapes = (
        jax.ShapeDtypeStruct((n, D_SSM), F32),
        jax.ShapeDtypeStruct((D_ATTN, n), BF16),
        jax.ShapeDtypeStruct((N_KV_HEADS, n, HEAD_DIM), BF16),
        jax.ShapeDtypeStruct((D_KV, n), BF16),
        jax.ShapeDtypeStruct((IDX_HEADS * IDX_DIM, n), BF16),
        jax.ShapeDtypeStruct((n, IDX_DIM), BF16),
        jax.ShapeDtypeStruct((IDX_HEADS, n), F32),
        jax.ShapeDtypeStruct((n, D_MODEL), BF16),
        jax.ShapeDtypeStruct((n, D_MODEL), BF16),
    )
    out_specs = (
        pl.BlockSpec((tm, D_SSM), row),
        pl.BlockSpec((D_ATTN, tm), col),
        pl.BlockSpec((N_KV_HEADS, tm, HEAD_DIM), lambda i: (0, i, 0)),
        pl.BlockSpec((D_KV, tm), col),
        pl.BlockSpec((IDX_HEADS * IDX_DIM, tm), col),
        pl.BlockSpec((tm, IDX_DIM), row),
        pl.BlockSpec((IDX_HEADS, tm), col),
        pl.BlockSpec((tm, D_MODEL), row),
        pl.BlockSpec((tm, D_MODEL), row),
    )
    in_specs = [
        pl.BlockSpec((tm, D_MODEL), row),
        _const_spec((1, D_MODEL)),
        pl.BlockSpec((tm, LANES), lambda i: (i % tiles_per_seq, 0)),
        pl.BlockSpec((tm, LANES), lambda i: (i % tiles_per_seq, 0)),
        _const_spec((2, D_MODEL)),
    ] + [_const_spec(w.shape) for w in weights]
    return pl.pallas_call(
        _in_proj_kernel,
        grid=(n // tm,),
        in_specs=in_specs,
        out_specs=out_specs,
        out_shape=out_shapes,
        compiler_params=_cparams(("arbitrary",)),
        name="in_proj",
    )(x2, norm1_g.reshape(1, D_MODEL), cos, sin, b_gate, *weights)


S5_T = 512
S5_CHUNK = 512
N_STATE = N_SSM_GROUPS * SSM_STATE
S5_HALF_IN = D_SSM // 2
S5_HALF_ST = N_STATE // 2


def _cmul(ar, ai, br, bi):
    return ar * br - ai * bi, ar * bi + ai * br


def _s5_kernel(u_ref, lr_ref, li_ref, ldt_ref, bre_ref, bim_ref, cre_ref, cim_ref, d_ref,
               wglu_ref, bglu_ref, y_ref,
               bb_ref, cc_ref, pw_re_ref, pw_im_ref, st_re_ref, st_im_ref, carry_re_ref, carry_im_ref):
    first = jnp.logical_and(pl.program_id(0) == 0, pl.program_id(1) == 0)

    @pl.when(first)
    def _():
        lr, li = lr_ref[...], li_ref[...]
        dt = jnp.exp(ldt_ref[...])
        mag = jnp.exp(lr * dt)
        ar = mag * jnp.cos(li * dt)
        ai = mag * jnp.sin(li * dt)
        den = lr * lr + li * li
        zr = ((ar - 1.0) * lr + ai * li) / den
        zi = (ai * lr - (ar - 1.0) * li) / den
        for h in range(2):
            rows = slice(h * S5_HALF_IN, (h + 1) * S5_HALF_IN)
            cols = slice(h * S5_HALF_ST, (h + 1) * S5_HALF_ST)
            bre, bim = bre_ref[rows, cols], bim_ref[rows, cols]
            bb_ref[h, :, :S5_HALF_ST] = (zr[:, cols] * bre - zi[:, cols] * bim).astype(BF16)
            bb_ref[h, :, S5_HALF_ST:] = (zr[:, cols] * bim + zi[:, cols] * bre).astype(BF16)
            cc_ref[h, :S5_HALF_ST, :] = cre_ref[cols, rows].astype(BF16)
            cc_ref[h, S5_HALF_ST:, :] = (-cim_ref[cols, rows]).astype(BF16)
        pr, pi = ar, ai
        pw_re_ref[0:1, :] = pr
        pw_im_ref[0:1, :] = pi
        for n in range(1, SUBLANES):
            pr, pi = _cmul(pr, pi, ar, ai)
            pw_re_ref[n:n + 1, :] = pr
            pw_im_ref[n:n + 1, :] = pi

    @pl.when(pl.program_id(1) == 0)
    def _():
        carry_re_ref[...] = jnp.zeros_like(carry_re_ref)
        carry_im_ref[...] = jnp.zeros_like(carry_im_ref)

    u = u_ref[...]
    ub = u.astype(BF16)
    for h in range(2):
        bu = jnp.dot(ub[:, h * S5_HALF_IN:(h + 1) * S5_HALF_IN], bb_ref[h], preferred_element_type=F32)
        st_re_ref[:, h * S5_HALF_ST:(h + 1) * S5_HALF_ST] = bu[:, :S5_HALF_ST]
        st_im_ref[:, h * S5_HALF_ST:(h + 1) * S5_HALF_ST] = bu[:, S5_HALF_ST:]

    t_len = u.shape[0]
    row = lax.broadcasted_iota(I32, (SUBLANES, S5_CHUNK), 0)
    for c in range(N_STATE // S5_CHUNK):
        lanes = slice(c * S5_CHUNK, (c + 1) * S5_CHUNK)
        p_re, p_im = pw_re_ref[:, lanes], pw_im_ref[:, lanes]
        steps = []
        for d in (1, 2, 4):
            a_re = jnp.where(row >= d, jnp.broadcast_to(p_re[d - 1:d, :], row.shape), 0.0)
            a_im = jnp.where(row >= d, jnp.broadcast_to(p_im[d - 1:d, :], row.shape), 0.0)
            steps.append((d, a_re, a_im))

        def block(i, carry):
            c_re, c_im = carry
            r0 = pl.multiple_of(i * SUBLANES, SUBLANES)
            xr = st_re_ref[pl.ds(r0, SUBLANES), lanes]
            xi = st_im_ref[pl.ds(r0, SUBLANES), lanes]
            for d, a_re, a_im in steps:
                sr = pltpu.roll(xr, d, 0)
                si = pltpu.roll(xi, d, 0)
                tr, ti = _cmul(a_re, a_im, sr, si)
                xr, xi = xr + tr, xi + ti
            tr, ti = _cmul(p_re, p_im, jnp.broadcast_to(c_re, xr.shape), jnp.broadcast_to(c_im, xi.shape))
            xr, xi = xr + tr, xi + ti
            st_re_ref[pl.ds(r0, SUBLANES), lanes] = xr
            st_im_ref[pl.ds(r0, SUBLANES), lanes] = xi
            return xr[SUBLANES - 1:, :], xi[SUBLANES - 1:, :]

        def two_blocks(j, carry):
            return block(2 * j + 1, block(2 * j, carry))

        c_re, c_im = lax.fori_loop(0, t_len // (2 * SUBLANES), two_blocks,
                                   (carry_re_ref[:, lanes], carry_im_ref[:, lanes]))
        carry_re_ref[:, lanes] = c_re
        carry_im_ref[:, lanes] = c_im

    ys = []
    for h in range(2):
        cols = slice(h * S5_HALF_ST, (h + 1) * S5_HALF_ST)
        xs = jnp.concatenate([st_re_ref[:, cols], st_im_ref[:, cols]], axis=1).astype(BF16)
        ys.append(jnp.dot(xs, cc_ref[h], preferred_element_type=F32))
    y = jnp.concatenate(ys, axis=1) + d_ref[...] * u
    y = jax.nn.gelu(y)
    gate = jnp.dot(y.astype(BF16), wglu_ref[...], preferred_element_type=F32) + bglu_ref[...]
    y_ref[...] = (y * jax.nn.sigmoid(gate)).astype(BF16)


def _s5(u, batch, seq_len, lam_re, lam_im, log_dt, b_re, b_im, c_re, c_im, d_skip, w_glu, b_glu):
    n = u.shape[0]
    t = min(S5_T, seq_len)
    g, p, hh = N_SSM_GROUPS, SSM_STATE, SSM_GROUP
    eye = jnp.eye(g, dtype=F32)
    bre_bd = jnp.einsum("gph,gk->ghkp", b_re, eye).reshape(D_SSM, N_STATE)
    bim_bd = jnp.einsum("gph,gk->ghkp", b_im, eye).reshape(D_SSM, N_STATE)
    cre_bd = jnp.einsum("ghp,gk->gpkh", c_re, eye).reshape(N_STATE, D_SSM)
    cim_bd = jnp.einsum("ghp,gk->gpkh", c_im, eye).reshape(N_STATE, D_SSM)
    lr = lam_re.reshape(1, N_STATE)
    li = lam_im.reshape(1, N_STATE)
    ldt = jnp.repeat(log_dt, p).reshape(1, N_STATE)
    tiles = seq_len // t
    consts = [lr, li, ldt, bre_bd, bim_bd, cre_bd, cim_bd, d_skip.reshape(1, D_SSM),
              w_glu.astype(BF16), b_glu.reshape(1, D_SSM)]
    return pl.pallas_call(
        _s5_kernel,
        grid=(batch, tiles),
        in_specs=[pl.BlockSpec((t, D_SSM), lambda b, i: (b * tiles + i, 0))]
        + [_const_spec(c.shape) for c in consts],
        out_specs=pl.BlockSpec((t, D_SSM), lambda b, i: (b * tiles + i, 0)),
        out_shape=jax.ShapeDtypeStruct((n, D_SSM), BF16),
        scratch_shapes=[
            pltpu.VMEM((2, S5_HALF_IN, 2 * S5_HALF_ST), BF16),
            pltpu.VMEM((2, 2 * S5_HALF_ST, S5_HALF_IN), BF16),
            pltpu.VMEM((SUBLANES, N_STATE), F32),
            pltpu.VMEM((SUBLANES, N_STATE), F32),
            pltpu.VMEM((t, N_STATE), F32),
            pltpu.VMEM((t, N_STATE), F32),
            pltpu.VMEM((1, N_STATE), F32),
            pltpu.VMEM((1, N_STATE), F32),
        ],
        compiler_params=_cparams(("arbitrary", "arbitrary")),
        name="s5",
    )(u, *consts)


AT_TQ = 256
AT_TK = 128
INT_MIN = -2 ** 31


def _attn_kernel(qt_ref, qit_ref, wit_ref, qit_next_ref, wit_next_ref, k_ref, ki_ref, vt_ref, y_ref,
                 score_ref, m_ref, l_ref, acc_ref, lg_ref, *, seq_len, n_sel):
    j = pl.program_id(1)
    last_block = pl.num_programs(1) - 1
    tq, tk = AT_TQ, AT_TK
    n_diag = tq // tk
    n_full = j * n_diag
    n_tot = n_full + n_diag
    q_pos = j * tq + lax.broadcasted_iota(I32, (tk, tq), 1)
    k_iota = lax.broadcasted_iota(I32, (tk, tq), 0)
    scores = score_ref.at[j % 2]
    scores_next = score_ref.at[(j + 1) % 2]

    def tile_start(kt):
        return pl.multiple_of(kt * tk, tk)

    def score_tile(kt, q_ref, w_ref, dst, dst_pos):
        r0 = tile_start(kt)
        ki_t = ki_ref[pl.ds(r0, tk), :]
        acc = jnp.zeros((tk, tq), F32)
        for h in range(IDX_HEADS):
            s = jnp.dot(ki_t, q_ref[h * IDX_DIM:(h + 1) * IDX_DIM, :], preferred_element_type=F32)
            acc = acc + w_ref[h:h + 1, :] * jnp.maximum(s, 0.0)
        dst[pl.ds(r0, tk), :] = jnp.where(r0 + k_iota <= dst_pos, acc, NEG)

    for dd in range(n_diag):
        score_tile(n_full + dd, qit_ref, wit_ref, scores, q_pos)

    n_beyond = seq_len - (j + 1) * tq

    def candidate(u):
        s = u ^ INT_MIN
        return lax.bitcast_convert_type(jnp.where(s < 0, s ^ 0x7FFFFFFF, s), F32)

    n_acc = 4

    def count_tile(c, cnts, cand):
        r0 = pl.multiple_of(c * tq, tq)
        sc = scores[pl.ds(r0, tq), :]
        cnts = list(cnts)
        for i in range(tq // SUBLANES):
            rows = sc[i * SUBLANES:(i + 1) * SUBLANES]
            cnts[i % n_acc] = jnp.where(rows >= cand, cnts[i % n_acc] + 1, cnts[i % n_acc])
        return tuple(cnts)

    def bit_step(b, thr_u):
        cand_u = thr_u | lax.shift_left(jnp.int32(1), 31 - b)
        cand = candidate(cand_u)
        zero = jnp.zeros((SUBLANES, tq), I32)
        cnts = lax.fori_loop(0, j + 1, functools.partial(count_tile, cand=cand), (zero,) * n_acc)
        cnt = (cnts[0] + cnts[1]) + (cnts[2] + cnts[3])
        total = jnp.sum(cnt, axis=0, keepdims=True) + jnp.where(cand <= NEG, n_beyond, 0)
        return jnp.where(total >= n_sel, cand_u, thr_u)

    thr = candidate(lax.fori_loop(0, 32, bit_step, jnp.zeros((1, tq), I32)))

    m_ref[...] = jnp.full(m_ref.shape, NEG, F32)
    l_ref[...] = jnp.zeros(l_ref.shape, F32)
    acc_ref[...] = jnp.zeros(acc_ref.shape, F32)
    rep = N_HEADS // N_KV_HEADS

    def logits(kt, slot):
        r0 = tile_start(kt)
        valid = jnp.logical_and(scores[pl.ds(r0, tk), :] >= thr, r0 + k_iota <= q_pos)
        bias = jnp.where(valid, 0.0, NEG)
        for g in range(N_KV_HEADS):
            k_t = k_ref[g, pl.ds(r0, tk), :]
            for h in range(g * rep, (g + 1) * rep):
                q_h = qt_ref[h * HEAD_DIM:(h + 1) * HEAD_DIM, :]
                lg_ref[slot, h] = jnp.dot(k_t, q_h, preferred_element_type=F32) + bias

    ones_rows = jnp.ones((2 * SUBLANES, tk), BF16)

    def softmax_values(kt, slot):
        r0 = tile_start(kt)
        for g in range(N_KV_HEADS):
            v_t = jnp.concatenate([vt_ref[g * HEAD_DIM:(g + 1) * HEAD_DIM, pl.ds(r0, tk)], ones_rows], axis=0)
            for h in range(g * rep, (g + 1) * rep):
                lg = lg_ref[slot, h]
                m_old = m_ref[h]
                m_new = jnp.maximum(m_old, jnp.max(lg, axis=0, keepdims=True))
                p = jnp.exp2(lg - m_new)
                alpha = jnp.exp2(m_old - m_new)
                pv = jnp.dot(v_t, p.astype(BF16), preferred_element_type=F32)
                l_ref[h] = alpha * l_ref[h] + pv[HEAD_DIM:HEAD_DIM + 1, :]
                acc_ref[h] = alpha * acc_ref[h] + pv[:HEAD_DIM, :]
                m_ref[h] = m_new

    logits(0, 0)

    def tile_pair(i, c, with_next):
        logits(2 * i + 1, 1)
        softmax_values(2 * i, 0)
        logits(jnp.minimum(2 * i + 2, n_tot - 1), 0)
        softmax_values(2 * i + 1, 1)
        if with_next:
            for kt in (2 * i, 2 * i + 1):
                score_tile(kt, qit_next_ref, wit_next_ref, scores_next, q_pos + tq)
        return c

    @pl.when(j < last_block)
    def _():
        lax.fori_loop(0, n_tot // 2, functools.partial(tile_pair, with_next=True), 0)

    @pl.when(j == last_block)
    def _():
        lax.fori_loop(0, n_tot // 2, functools.partial(tile_pair, with_next=False), 0)

    outs = [acc_ref[h] / l_ref[h] for h in range(N_HEADS)]
    y_ref[...] = jnp.concatenate(outs, axis=0).T.astype(BF16)


def _attention(qt, qit, wit, k, ki, vt, batch, seq_len):
    n = ki.shape[0]
    tq = AT_TQ
    nq = seq_len // tq
    n_sel = min(TOPK, seq_len // 4)
    qcol = lambda b, j: (0, b * nq + j)
    qcol_next = lambda b, j: (0, b * nq + jnp.minimum(j + 1, nq - 1))
    return pl.pallas_call(
        functools.partial(_attn_kernel, seq_len=seq_len, n_sel=n_sel),
        grid=(batch, nq),
        in_specs=[
            pl.BlockSpec((D_ATTN, tq), qcol),
            pl.BlockSpec((IDX_HEADS * IDX_DIM, tq), qcol),
            pl.BlockSpec((IDX_HEADS, tq), qcol),
            pl.BlockSpec((IDX_HEADS * IDX_DIM, tq), qcol_next),
            pl.BlockSpec((IDX_HEADS, tq), qcol_next),
            pl.BlockSpec((N_KV_HEADS, seq_len, HEAD_DIM), lambda b, j: (0, b, 0)),
            pl.BlockSpec((seq_len, IDX_DIM), lambda b, j: (b, 0)),
            pl.BlockSpec((D_KV, seq_len), lambda b, j: (0, b)),
        ],
        out_specs=pl.BlockSpec((tq, D_ATTN), lambda b, j: (b * nq + j, 0)),
        out_shape=jax.ShapeDtypeStruct((n, D_ATTN), BF16),
        scratch_shapes=[pltpu.VMEM((2, seq_len, tq), F32),
                        pltpu.VMEM((N_HEADS, 1, tq), F32),
                        pltpu.VMEM((N_HEADS, 1, tq), F32),
                        pltpu.VMEM((N_HEADS, HEAD_DIM, tq), F32),
                        pltpu.VMEM((2, N_HEADS, AT_TK, tq), F32)],
        compiler_params=_cparams(("arbitrary", "arbitrary")),
        name="attn",
    )(qt, qit, wit, qit, wit, k, ki, vt)


MIX_TM = 512


def _split_bf16(a):
    hi = a.astype(BF16)
    return hi, (a - hi.astype(F32)).astype(BF16)


def _mix_kernel(x_ref, ys_ref, ya_ref, gs_ref, ga_ref, wps_ref, wpa_ref, wo_ref, g2_ref,
                wr_hi_ref, wr_lo_ref, br_ref, x1_ref, h_ref, idx_ref, gate_ref):
    ps = jnp.dot(ys_ref[...], wps_ref[...], preferred_element_type=F32)
    pa = jnp.dot(ya_ref[...], wpa_ref[...], preferred_element_type=F32)
    mixed = gs_ref[...].astype(F32) * ps + ga_ref[...].astype(F32) * pa
    x1 = x_ref[...] + jnp.dot(mixed.astype(BF16), wo_ref[...], preferred_element_type=F32)
    h = x1 * lax.rsqrt(jnp.mean(x1 * x1, axis=-1, keepdims=True) + RMS_EPS) * g2_ref[...]
    tm = x1.shape[0]
    for j in range(D_MODEL // LANES):
        x1_ref[pl.ds(j, tm, stride=D_MODEL // LANES), :] = x1[:, j * LANES:(j + 1) * LANES]
        h_ref[pl.ds(j, tm, stride=D_MODEL // LANES), :] = h[:, j * LANES:(j + 1) * LANES]
    h_hi, h_lo = _split_bf16(h)
    logits = (jnp.dot(h_hi, wr_hi_ref[...], preferred_element_type=F32)
              + jnp.dot(h_hi, wr_lo_ref[...], preferred_element_type=F32)
              + jnp.dot(h_lo, wr_hi_ref[...], preferred_element_type=F32)) + br_ref[...]
    lane = lax.broadcasted_iota(I32, logits.shape, 1)
    rest = logits
    firsts, vals = [], []
    for _ in range(TOP_K_EXPERTS):
        m = jnp.max(rest, axis=-1, keepdims=True)
        first = jnp.minimum(jnp.min(jnp.where(rest == m, lane, N_EXPERTS), axis=-1, keepdims=True),
                            N_EXPERTS - 1)
        firsts.append(first)
        vals.append(m)
        rest = jnp.where(lane == first, -jnp.inf, rest)
    es = [jnp.exp(v - vals[0]) for v in vals]
    denom = es[0] + es[1] + es[2] + es[3]
    idx_ref[...] = jnp.concatenate(firsts, axis=1)
    gate_ref[...] = jnp.concatenate([e / denom for e in es], axis=1)


def _mix(x2, y_ssm, y_att, gs, ga, w_proj_ssm, w_proj_attn, w_out, norm2_g, w_router, b_router):
    n = x2.shape[0]
    tm = min(MIX_TM, n)
    chunks = D_MODEL // LANES
    row = lambda i: (i, 0)
    wr_hi, wr_lo = _split_bf16(w_router)
    consts = [w_proj_ssm.astype(BF16), w_proj_attn.astype(BF16), w_out.astype(BF16),
              norm2_g.reshape(1, D_MODEL), wr_hi, wr_lo, b_router.reshape(1, N_EXPERTS)]
    return pl.pallas_call(
        _mix_kernel,
        grid=(n // tm,),
        in_specs=[pl.BlockSpec((tm, D_MODEL), row), pl.BlockSpec((tm, D_SSM), row),
                  pl.BlockSpec((tm, D_ATTN), row), pl.BlockSpec((tm, D_MODEL), row),
                  pl.BlockSpec((tm, D_MODEL), row)] + [_const_spec(c.shape) for c in consts],
        out_specs=(pl.BlockSpec((tm * chunks, LANES), row), pl.BlockSpec((tm * chunks, LANES), row),
                   pl.BlockSpec((tm, TOP_K_EXPERTS), row), pl.BlockSpec((tm, TOP_K_EXPERTS), row)),
        out_shape=(jax.ShapeDtypeStruct((n * chunks, LANES), F32), jax.ShapeDtypeStruct((n * chunks, LANES), F32),
                   jax.ShapeDtypeStruct((n, TOP_K_EXPERTS), I32), jax.ShapeDtypeStruct((n, TOP_K_EXPERTS), F32)),
        compiler_params=_cparams(("arbitrary",)),
        name="mix",
    )(x2, y_ssm, y_att, gs, ga, *consts)


MOE_TT = 2048
MOE_RB = 288
ROW_CHUNKS = D_MODEL // LANES
assert ROW_CHUNKS == SUBLANES
MOE_PITCH = MOE_RB + SUBLANES


def _moe_kernel(cnt_ref, off_ref, tok_ref, gate_ref, h_ref, win_ref, bin_ref, wout_ref, bout_ref,
                out_ref, xa_ref, xb_ref, xx_ref, ya_ref, yb_ref, yx_ref):
    i, e = pl.program_id(0), pl.program_id(1)
    trash = out_ref.shape[0] // SUBLANES - 1
    here = i * N_EXPERTS + e
    n, start = cnt_ref[here], off_ref[here]
    start_next = off_ref[i * N_EXPERTS + jnp.minimum(e + 1, N_EXPERTS - 1)]
    prev = i * N_EXPERTS + jnp.maximum(e - 1, 0)
    start_prev = off_ref[prev]
    n_prev = jnp.where(e > 0, jnp.minimum(cnt_ref[prev], MOE_RB), 0)

    def slab(t):
        return pl.ds(pl.multiple_of(t * SUBLANES, SUBLANES), SUBLANES)

    def tile_rows(r):
        return pl.ds(r, SUBLANES, stride=MOE_PITCH)

    def gather_row(base, r, x_ref):
        x_ref[tile_rows(r), :] = h_ref[slab(tok_ref[0, 0, base + r]), :]

    def gather_loop(base, x_ref):
        def body(r8, c):
            for rr in range(SUBLANES):
                gather_row(base, r8 * SUBLANES + rr, x_ref)
            return c

        lax.fori_loop(0, MOE_RB // SUBLANES, body, 0)

    def expert_mlp(x_ref, y_ref):
        xg = jnp.concatenate([x_ref[j * MOE_PITCH:j * MOE_PITCH + MOE_RB, :] for j in range(ROW_CHUNKS)],
                             axis=1).astype(BF16)
        z = jnp.dot(xg, win_ref[0], preferred_element_type=F32) + bin_ref[0]
        half = MXU_COLS // 2
        groups = range(2 * D_EXPERT // MXU_COLS)
        zg = jnp.concatenate([z[:, c * MXU_COLS:c * MXU_COLS + half] for c in groups], axis=1)
        zl = jnp.concatenate([z[:, c * MXU_COLS + half:(c + 1) * MXU_COLS] for c in groups], axis=1)
        zg = jnp.minimum(zg, SWIGLU_LIMIT)
        zl = jnp.clip(zl, -SWIGLU_LIMIT, SWIGLU_LIMIT)
        act = zg * jax.nn.sigmoid(SWIGLU_ALPHA * zg) * (zl + 1.0)
        y = jnp.dot(act.astype(BF16), wout_ref[0], preferred_element_type=F32) + bout_ref[0]
        for j in range(ROW_CHUNKS):
            y_ref[j * MOE_PITCH:j * MOE_PITCH + MOE_RB, :] = y[:, j * LANES:(j + 1) * LANES]

    def updated(base, r, n_valid, y_ref):
        t = jnp.where(r < n_valid, tok_ref[0, 0, base + r], trash)
        return t, out_ref[slab(t), :] + gate_ref[0, 0, base + r] * y_ref[tile_rows(r), :]

    def scatter_group(base, r0, n_valid, y_ref):
        rows = [updated(base, r0 + rr, n_valid, y_ref) for rr in range(SUBLANES)]
        for t, v in rows:
            out_ref[slab(t), :] = v

    def scatter_loop(base, n_valid, y_ref):
        def body(r8, c):
            scatter_group(base, r8 * SUBLANES, n_valid, y_ref)
            return c

        lax.fori_loop(0, (n_valid + SUBLANES - 1) // SUBLANES, body, 0)

    @pl.when(e == 0)
    def _():
        out_ref[...] = jnp.zeros(out_ref.shape, F32)
        gather_loop(start, xa_ref)

    @pl.when(jnp.logical_and(i == 0, e == 0))
    def _():
        yb_ref[...] = jnp.zeros(yb_ref.shape, F32)

    def pipelined(x_cur, x_next, y_cur, y_prev):
        for r in range(MOE_RB):
            gather_row(start_next, r, x_next)
        expert_mlp(x_cur, y_cur)
        for r0 in range(0, MOE_RB, SUBLANES):
            scatter_group(start_prev, r0, n_prev, y_prev)

    @pl.when(e % 2 == 0)
    def _():
        pipelined(xa_ref, xb_ref, ya_ref, yb_ref)

    @pl.when(e % 2 == 1)
    def _():
        pipelined(xb_ref, xa_ref, yb_ref, ya_ref)

    def extra_block(b, c):
        base = start + b * MOE_RB
        gather_loop(base, xx_ref)
        expert_mlp(xx_ref, yx_ref)
        scatter_loop(base, jnp.minimum(MOE_RB, n - b * MOE_RB), yx_ref)
        return c

    lax.fori_loop(1, (n + MOE_RB - 1) // MOE_RB, extra_block, 0)

    @pl.when(e == N_EXPERTS - 1)
    def _():
        scatter_loop(start, jnp.minimum(n, MOE_RB), yb_ref)


FINAL_TM = 512


def _final_kernel(x1_ref, moe_ref, g_ref, o_ref):
    tm = o_ref.shape[0]
    v = jnp.concatenate([x1_ref[pl.ds(j, tm, stride=ROW_CHUNKS), :] + moe_ref[0, pl.ds(j, tm, stride=ROW_CHUNKS), :]
                         for j in range(ROW_CHUNKS)], axis=1)
    o_ref[...] = v * lax.rsqrt(jnp.mean(v * v, axis=-1, keepdims=True) + RMS_EPS) * g_ref[...]


MXU_COLS = 256


def _expert_weights_kernel(win_ref, wout_ref, perm_ref, winp_ref, woutb_ref):
    w = win_ref[0].astype(BF16)
    for c in range(w.shape[1] // MXU_COLS):
        cols = slice(c * MXU_COLS, (c + 1) * MXU_COLS)
        winp_ref[0, :, cols] = jnp.dot(w[:, cols], perm_ref[...], preferred_element_type=F32).astype(BF16)
    woutb_ref[0] = wout_ref[0].astype(BF16)


def _expert_weights(w_moe_in, w_moe_out):
    half = MXU_COLS // 2
    src = jnp.concatenate([jnp.arange(half) * 2, jnp.arange(half) * 2 + 1])
    perm = (jnp.arange(MXU_COLS)[:, None] == src[None, :]).astype(BF16)
    per_expert = lambda a: pl.BlockSpec((1,) + a.shape[1:], lambda e: (e, 0, 0))
    return pl.pallas_call(
        _expert_weights_kernel,
        grid=(N_EXPERTS,),
        in_specs=[per_expert(w_moe_in), per_expert(w_moe_out), _const_spec(perm.shape)],
        out_specs=(per_expert(w_moe_in), per_expert(w_moe_out)),
        out_shape=(jax.ShapeDtypeStruct(w_moe_in.shape, BF16), jax.ShapeDtypeStruct(w_moe_out.shape, BF16)),
        compiler_params=_cparams(("arbitrary",)),
        name="expert_weights",
    )(w_moe_in, w_moe_out, perm)


def _moe(h, x1, idx4, gate4, w_moe_in, b_moe_in, w_moe_out, b_moe_out, norm_f_g):
    n = h.shape[0] // ROW_CHUNKS
    tt = min(MOE_TT, n)
    n_tiles = n // tt
    pairs = TOP_K_EXPERTS * tt
    cap = pairs + MOE_RB

    local = (jnp.arange(n, dtype=I32) % tt)[:, None]
    keys = (idx4 * tt + local).reshape(n_tiles, pairs)
    keys, gate = lax.sort((keys, gate4.reshape(n_tiles, pairs)), dimension=1, num_keys=1)
    bounds = jnp.arange(N_EXPERTS + 1, dtype=I32) * tt
    below = jnp.sum(keys[:, :, None] < bounds[None, None, :], axis=1).astype(I32)
    off, cnt = below[:, :-1], below[:, 1:] - below[:, :-1]
    pad = ((0, 0), (0, cap - pairs))
    tok = jnp.pad(keys % tt, pad).reshape(n_tiles, 1, cap)
    gate = jnp.pad(gate, pad).reshape(n_tiles, 1, cap)

    win, wout = _expert_weights(w_moe_in, w_moe_out)
    half = MXU_COLS // 2
    bin_ = jnp.swapaxes(b_moe_in.reshape(N_EXPERTS, -1, half, 2), 2, 3).reshape(N_EXPERTS, 1, 2 * D_EXPERT)
    bout = b_moe_out.reshape(N_EXPERTS, 1, D_MODEL)

    out_rows = (tt + 1) * ROW_CHUNKS
    tile1 = pl.BlockSpec((tt * ROW_CHUNKS, LANES), lambda i, e, *_: (i, 0), pipeline_mode=pl.Buffered(1))
    smem_list = pl.BlockSpec((1, 1, cap), lambda i, e, *_: (i, 0, 0), memory_space=pltpu.SMEM)
    per_expert = lambda shape: pl.BlockSpec((1,) + shape, lambda i, e, *_: (e, 0, 0))
    row_tile = pltpu.VMEM((ROW_CHUNKS * MOE_PITCH, LANES), F32)
    grid_spec = pltpu.PrefetchScalarGridSpec(
        num_scalar_prefetch=2,
        grid=(n_tiles, N_EXPERTS),
        in_specs=[smem_list, smem_list, tile1,
                  per_expert((D_MODEL, 2 * D_EXPERT)), per_expert((1, 2 * D_EXPERT)),
                  per_expert((D_EXPERT, D_MODEL)), per_expert((1, D_MODEL))],
        out_specs=pl.BlockSpec((out_rows, LANES), lambda i, e, *_: (i, 0)),
        scratch_shapes=[row_tile] * 6,
    )
    moe_out = pl.pallas_call(
        _moe_kernel,
        grid_spec=grid_spec,
        out_shape=jax.ShapeDtypeStruct((n_tiles * out_rows, LANES), F32),
        compiler_params=_cparams(("arbitrary", "arbitrary")),
        name="moe",
    )(cnt.reshape(-1), off.reshape(-1).astype(I32), tok, gate, h, win, bin_, wout, bout)

    tm = min(FINAL_TM, tt)
    per_tile = tt // tm
    return pl.pallas_call(
        _final_kernel,
        grid=(n_tiles, per_tile),
        in_specs=[pl.BlockSpec((tm * ROW_CHUNKS, LANES), lambda i, j: (i * per_tile + j, 0)),
                  pl.BlockSpec((1, tm * ROW_CHUNKS, LANES), lambda i, j: (i, j, 0)),
                  _const_spec((1, D_MODEL))],
        out_specs=pl.BlockSpec((tm, D_MODEL), lambda i, j: (i * per_tile + j, 0)),
        out_shape=jax.ShapeDtypeStruct((n, D_MODEL), F32),
        compiler_params=_cparams(("arbitrary", "arbitrary")),
        name="final_norm",
    )(x1, moe_out.reshape(n_tiles, out_rows, LANES), norm_f_g.reshape(1, D_MODEL))


def kernel(x, norm1_g, w_in, b_gate, ssm_lam_re, ssm_lam_im, ssm_log_dt, ssm_b_re, ssm_b_im, ssm_c_re, ssm_c_im, ssm_d, w_glu, b_glu, w_proj_ssm, w_proj_attn, w_out, norm2_g, w_router, b_router, w_moe_in, b_moe_in, w_moe_out, b_moe_out, norm_f_g):
    bsz, seq_len, _ = x.shape
    n = bsz * seq_len
    x2 = x.reshape(n, D_MODEL)
    u, qt, k, vt, qit, ki, wit, gs, ga = _in_proj(x2, norm1_g[0], w_in[0], b_gate[0], seq_len)
    y_ssm = _s5(u, bsz, seq_len, ssm_lam_re[0], ssm_lam_im[0], ssm_log_dt[0], ssm_b_re[0], ssm_b_im[0],
                ssm_c_re[0], ssm_c_im[0], ssm_d[0], w_glu[0], b_glu[0])
    y_att = _attention(qt, qit, wit, k, ki, vt, bsz, seq_len)
    x1, h, idx4, gate4 = _mix(x2, y_ssm, y_att, gs, ga, w_proj_ssm[0], w_proj_attn[0], w_out[0], norm2_g[0],
                              w_router[0], b_router[0])
    out = _moe(h, x1, idx4, gate4, w_moe_in[0], b_moe_in[0], w_moe_out[0], b_moe_out[0], norm_f_g)
    return out.reshape(x.shape)
```

```python
import functools
import math

import numpy as np
import jax
import jax.numpy as jnp
from jax import lax
from jax.experimental import pallas as pl
from jax.experimental.pallas import tpu as pltpu

D_MODEL = 1024
D_SSM = 512
SSM_GROUP = 16
N_SSM_GROUPS = 32
SSM_STATE = 64
N_HEADS = 8
N_KV_HEADS = 2
HEAD_DIM = 64
D_ATTN = N_HEADS * HEAD_DIM
D_KV = N_KV_HEADS * HEAD_DIM
IDX_HEADS = 16
IDX_DIM = 64
TOPK = 256
ROPE_THETA = 10000.0
N_EXPERTS = 32
TOP_K_EXPERTS = 4
D_EXPERT = D_MODEL
SWIGLU_LIMIT = 7.0
SWIGLU_ALPHA = 1.702
RMS_EPS = 1e-5
NEG = -1e30

LANES = 128
SUBLANES = 8
VMEM_LIMIT = 56 * 1024 * 1024

F32 = jnp.float32
BF16 = jnp.bfloat16
I32 = jnp.int32


def _cparams(sem):
    return pltpu.CompilerParams(dimension_semantics=sem, vmem_limit_bytes=VMEM_LIMIT)


def _const_spec(shape):
    nd = len(shape)
    return pl.BlockSpec(shape, lambda *_: (0,) * nd)


IN_TM = 512


def _in_proj_kernel(x_ref, g_ref, cos_ref, sin_ref, bg_ref,
                    wu_ref, wq_ref, wkv_ref, wqi_ref, wkw_ref, wgs_ref, wga_ref,
                    u_ref, qt_ref, k_ref, vt_ref, qit_ref, ki_ref, wit_ref, gs_ref, ga_ref):
    x = x_ref[...]
    xn = x * lax.rsqrt(jnp.mean(x * x, axis=-1, keepdims=True) + RMS_EPS) * g_ref[...]
    xb = xn.astype(BF16)

    def mm(w_ref):
        return jnp.dot(xb, w_ref[...], preferred_element_type=F32)

    cos = cos_ref[...]
    sin = sin_ref[...]
    lane = lax.broadcasted_iota(I32, cos.shape, 1)
    first_half = lane % HEAD_DIM < HEAD_DIM // 2

    def rope(a):
        cols = []
        for c in range(a.shape[1] // LANES):
            blk = a[:, c * LANES:(c + 1) * LANES]
            partner = jnp.where(first_half, pltpu.roll(blk, LANES - HEAD_DIM // 2, 1),
                                pltpu.roll(blk, HEAD_DIM // 2, 1))
            cols.append(blk * cos + partner * sin)
        return jnp.concatenate(cols, axis=1)

    u_ref[...] = mm(wu_ref)
    q = rope(mm(wq_ref)) * (HEAD_DIM ** -0.5 * math.log2(math.e))
    qt_ref[...] = q.T.astype(BF16)
    kv = mm(wkv_ref)
    k = rope(kv[:, :D_KV])
    for g in range(N_KV_HEADS):
        k_ref[g] = k[:, g * HEAD_DIM:(g + 1) * HEAD_DIM].astype(BF16)
    vt_ref[...] = kv[:, D_KV:].T.astype(BF16)
    kw = mm(wkw_ref)
    ki_ref[...] = rope(kw[:, :LANES])[:, :IDX_DIM].astype(BF16)
    qi = rope(mm(wqi_ref)) * (IDX_DIM ** -0.5)
    qit_ref[...] = qi.T.astype(BF16)
    wit_ref[...] = (kw[:, LANES:2 * LANES].T)[:IDX_HEADS] * (IDX_HEADS ** -0.5)
    gs_ref[...] = jax.nn.sigmoid(mm(wgs_ref) + bg_ref[0:1, :]).astype(BF16)
    ga_ref[...] = jax.nn.sigmoid(mm(wga_ref) + bg_ref[1:2, :]).astype(BF16)


def _in_proj(x2, norm1_g, w_in, b_gate, seq_len):
    n = x2.shape[0]
    tm = min(IN_TM, seq_len)
    o = 0
    parts = []
    for width in (D_SSM, D_ATTN, D_KV, D_KV, IDX_HEADS * IDX_DIM, IDX_DIM, IDX_HEADS, D_MODEL, D_MODEL):
        parts.append(w_in[:, o:o + width])
        o += width
    wu, wq, wk, wv, wqi, wki, wwi, wgs, wga = parts
    wkv = jnp.concatenate([wk, wv], axis=1)
    wkw = jnp.concatenate([wki, jnp.zeros((D_MODEL, LANES - IDX_DIM), F32),
                           wwi, jnp.zeros((D_MODEL, LANES - IDX_HEADS), F32)], axis=1)
    weights = [w.astype(BF16) for w in (wu, wq, wkv, wqi, wkw, wgs, wga)]

    half = HEAD_DIM // 2
    inv = ROPE_THETA ** (-jnp.arange(half, dtype=F32) / half)
    ang = jnp.arange(seq_len, dtype=F32)[:, None] * inv[None, :]
    cos = jnp.tile(jnp.cos(ang), (1, 4))
    sin = jnp.tile(jnp.concatenate([-jnp.sin(ang), jnp.sin(ang)], axis=1), (1, 2))

    tiles_per_seq = seq_len // tm
    row = lambda i: (i, 0)
    col = lambda i: (0, i)
    out_shapes = (
        jax.ShapeDtypeStruct((n, D_SSM), F32),
        jax.ShapeDtypeStruct((D_ATTN, n), BF16),
        jax.ShapeDtypeStruct((N_KV_HEADS, n, HEAD_DIM), BF16),
        jax.ShapeDtypeStruct((D_KV, n), BF16),
        jax.ShapeDtypeStruct((IDX_HEADS * IDX_DIM, n), BF16),
        jax.ShapeDtypeStruct((n, IDX_DIM), BF16),
        jax.ShapeDtypeStruct((IDX_HEADS, n), F32),
        jax.ShapeDtypeStruct((n, D_MODEL), BF16),
        jax.ShapeDtypeStruct((n, D_MODEL), BF16),
    )
    out_specs = (
        pl.BlockSpec((tm, D_SSM), row),
        pl.BlockSpec((D_ATTN, tm), col),
        pl.BlockSpec((N_KV_HEADS, tm, HEAD_DIM), lambda i: (0, i, 0)),
        pl.BlockSpec((D_KV, tm), col),
        pl.BlockSpec((IDX_HEADS * IDX_DIM, tm), col),
        pl.BlockSpec((tm, IDX_DIM), row),
        pl.BlockSpec((IDX_HEADS, tm), col),
        pl.BlockSpec((tm, D_MODEL), row),
        pl.BlockSpec((tm, D_MODEL), row),
    )
    in_specs = [
        pl.BlockSpec((tm, D_MODEL), row),
        _const_spec((1, D_MODEL)),
        pl.BlockSpec((tm, LANES), lambda i: (i % tiles_per_seq, 0)),
        pl.BlockSpec((tm, LANES), lambda i: (i % tiles_per_seq, 0)),
        _const_spec((2, D_MODEL)),
    ] + [_const_spec(w.shape) for w in weights]
    return pl.pallas_call(
        _in_proj_kernel,
        grid=(n // tm,),
        in_specs=in_specs,
        out_specs=out_specs,
        out_shape=out_shapes,
        compiler_params=_cparams(("arbitrary",)),
        name="in_proj",
    )(x2, norm1_g.reshape(1, D_MODEL), cos, sin, b_gate, *weights)


S5_T = 512
S5_CHUNK = 512
N_STATE = N_SSM_GROUPS * SSM_STATE
S5_HALF_IN = D_SSM // 2
S5_HALF_ST = N_STATE // 2


def _cmul(ar, ai, br, bi):
    return ar * br - ai * bi, ar * bi + ai * br


def _s5_kernel(u_ref, lr_ref, li_ref, ldt_ref, bre_ref, bim_ref, cre_ref, cim_ref, d_ref,
               wglu_ref, bglu_ref, y_ref,
               bb_ref, cc_ref, pw_re_ref, pw_im_ref, st_re_ref, st_im_ref, carry_re_ref, carry_im_ref):
    first = jnp.logical_and(pl.program_id(0) == 0, pl.program_id(1) == 0)

    @pl.when(first)
    def _():
        lr, li = lr_ref[...], li_ref[...]
        dt = jnp.exp(ldt_ref[...])
        mag = jnp.exp(lr * dt)
        ar = mag * jnp.cos(li * dt)
        ai = mag * jnp.sin(li * dt)
        den = lr * lr + li * li
        zr = ((ar - 1.0) * lr + ai * li) / den
        zi = (ai * lr - (ar - 1.0) * li) / den
        for h in range(2):
            rows = slice(h * S5_HALF_IN, (h + 1) * S5_HALF_IN)
            cols = slice(h * S5_HALF_ST, (h + 1) * S5_HALF_ST)
            bre, bim = bre_ref[rows, cols], bim_ref[rows, cols]
            bb_ref[h, :, :S5_HALF_ST] = (zr[:, cols] * bre - zi[:, cols] * bim).astype(BF16)
            bb_ref[h, :, S5_HALF_ST:] = (zr[:, cols] * bim + zi[:, cols] * bre).astype(BF16)
            cc_ref[h, :S5_HALF_ST, :] = cre_ref[cols, rows].astype(BF16)
            cc_ref[h, S5_HALF_ST:, :] = (-cim_ref[cols, rows]).astype(BF16)
        pr, pi = ar, ai
        pw_re_ref[0:1, :] = pr
        pw_im_ref[0:1, :] = pi
        for n in range(1, SUBLANES):
            pr, pi = _cmul(pr, pi, ar, ai)
            pw_re_ref[n:n + 1, :] = pr
            pw_im_ref[n:n + 1, :] = pi

    @pl.when(pl.program_id(1) == 0)
    def _():
        carry_re_ref[...] = jnp.zeros_like(carry_re_ref)
        carry_im_ref[...] = jnp.zeros_like(carry_im_ref)

    u = u_ref[...]
    ub = u.astype(BF16)
    for h in range(2):
        bu = jnp.dot(ub[:, h * S5_HALF_IN:(h + 1) * S5_HALF_IN], bb_ref[h], preferred_element_type=F32)
        st_re_ref[:, h * S5_HALF_ST:(h + 1) * S5_HALF_ST] = bu[:, :S5_HALF_ST]
        st_im_ref[:, h * S5_HALF_ST:(h + 1) * S5_HALF_ST] = bu[:, S5_HALF_ST:]

    t_len = u.shape[0]
    row = lax.broadcasted_iota(I32, (SUBLANES, S5_CHUNK), 0)
    for c in range(N_STATE // S5_CHUNK):
        lanes = slice(c * S5_CHUNK, (c + 1) * S5_CHUNK)
        p_re, p_im = pw_re_ref[:, lanes], pw_im_ref[:, lanes]
        steps = []
        for d in (1, 2, 4):
            a_re = jnp.where(row >= d, jnp.broadcast_to(p_re[d - 1:d, :], row.shape), 0.0)
            a_im = jnp.where(row >= d, jnp.broadcast_to(p_im[d - 1:d, :], row.shape), 0.0)
            steps.append((d, a_re, a_im))

        def block(i, carry):
            c_re, c_im = carry
            r0 = pl.multiple_of(i * SUBLANES, SUBLANES)
            xr = st_re_ref[pl.ds(r0, SUBLANES), lanes]
            xi = st_im_ref[pl.ds(r0, SUBLANES), lanes]
            for d, a_re, a_im in steps:
                sr = pltpu.roll(xr, d, 0)
                si = pltpu.roll(xi, d, 0)
                tr, ti = _cmul(a_re, a_im, sr, si)
                xr, xi = xr + tr, xi + ti
            tr, ti = _cmul(p_re, p_im, jnp.broadcast_to(c_re, xr.shape), jnp.broadcast_to(c_im, xi.shape))
            xr, xi = xr + tr, xi + ti
            st_re_ref[pl.ds(r0, SUBLANES), lanes] = xr
            st_im_ref[pl.ds(r0, SUBLANES), lanes] = xi
            return xr[SUBLANES - 1:, :], xi[SUBLANES - 1:, :]

        def two_blocks(j, carry):
            return block(2 * j + 1, block(2 * j, carry))

        c_re, c_im = lax.fori_loop(0, t_len // (2 * SUBLANES), two_blocks,
                                   (carry_re_ref[:, lanes], carry_im_ref[:, lanes]))
        carry_re_ref[:, lanes] = c_re
        carry_im_ref[:, lanes] = c_im

    ys = []
    for h in range(2):
        cols = slice(h * S5_HALF_ST, (h + 1) * S5_HALF_ST)
        xs = jnp.concatenate([st_re_ref[:, cols], st_im_ref[:, cols]], axis=1).astype(BF16)
        ys.append(jnp.dot(xs, cc_ref[h], preferred_element_type=F32))
    y = jnp.concatenate(ys, axis=1) + d_ref[...] * u
    y = jax.nn.gelu(y)
    gate = jnp.dot(y.astype(BF16), wglu_ref[...], preferred_element_type=F32) + bglu_ref[...]
    y_ref[...] = (y * jax.nn.sigmoid(gate)).astype(BF16)


def _s5(u, batch, seq_len, lam_re, lam_im, log_dt, b_re, b_im, c_re, c_im, d_skip, w_glu, b_glu):
    n = u.shape[0]
    t = min(S5_T, seq_len)
    g, p, hh = N_SSM_GROUPS, SSM_STATE, SSM_GROUP
    eye = jnp.eye(g, dtype=F32)
    bre_bd = jnp.einsum("gph,gk->ghkp", b_re, eye).reshape(D_SSM, N_STATE)
    bim_bd = jnp.einsum("gph,gk->ghkp", b_im, eye).reshape(D_SSM, N_STATE)
    cre_bd = jnp.einsum("ghp,gk->gpkh", c_re, eye).reshape(N_STATE, D_SSM)
    cim_bd = jnp.einsum("ghp,gk->gpkh", c_im, eye).reshape(N_STATE, D_SSM)
    lr = lam_re.reshape(1, N_STATE)
    li = lam_im.reshape(1, N_STATE)
    ldt = jnp.repeat(log_dt, p).reshape(1, N_STATE)
    tiles = seq_len // t
    consts = [lr, li, ldt, bre_bd, bim_bd, cre_bd, cim_bd, d_skip.reshape(1, D_SSM),
              w_glu.astype(BF16), b_glu.reshape(1, D_SSM)]
    return pl.pallas_call(
        _s5_kernel,
        grid=(batch, tiles),
        in_specs=[pl.BlockSpec((t, D_SSM), lambda b, i: (b * tiles + i, 0))]
        + [_const_spec(c.shape) for c in consts],
        out_specs=pl.BlockSpec((t, D_SSM), lambda b, i: (b * tiles + i, 0)),
        out_shape=jax.ShapeDtypeStruct((n, D_SSM), BF16),
        scratch_shapes=[
            pltpu.VMEM((2, S5_HALF_IN, 2 * S5_HALF_ST), BF16),
            pltpu.VMEM((2, 2 * S5_HALF_ST, S5_HALF_IN), BF16),
            pltpu.VMEM((SUBLANES, N_STATE), F32),
            pltpu.VMEM((SUBLANES, N_STATE), F32),
            pltpu.VMEM((t, N_STATE), F32),
            pltpu.VMEM((t, N_STATE), F32),
            pltpu.VMEM((1, N_STATE), F32),
            pltpu.VMEM((1, N_STATE), F32),
        ],
        compiler_params=_cparams(("arbitrary", "arbitrary")),
        name="s5",
    )(u, *consts)


AT_TQ = 256
AT_TK = 128
INT_MIN = -2 ** 31


def _attn_kernel(qt_ref, qit_ref, wit_ref, qit_next_ref, wit_next_ref, k_ref, ki_ref, vt_ref, y_ref,
                 score_ref, m_ref, l_ref, acc_ref, lg_ref, *, seq_len, n_sel):
    j = pl.program_id(1)
    last_block = pl.num_programs(1) - 1
    tq, tk = AT_TQ, AT_TK
    n_diag = tq // tk
    n_full = j * n_diag
    n_tot = n_full + n_diag
    q_pos = j * tq + lax.broadcasted_iota(I32, (tk, tq), 1)
    k_iota = lax.broadcasted_iota(I32, (tk, tq), 0)
    scores = score_ref.at[j % 2]
    scores_next = score_ref.at[(j + 1) % 2]

    def tile_start(kt):
        return pl.multiple_of(kt * tk, tk)

    def score_tile(kt, q_ref, w_ref, dst, dst_pos):
        r0 = tile_start(kt)
        ki_t = ki_ref[pl.ds(r0, tk), :]
        acc = jnp.zeros((tk, tq), F32)
        for h in range(IDX_HEADS):
            s = jnp.dot(ki_t, q_ref[h * IDX_DIM:(h + 1) * IDX_DIM, :], preferred_element_type=F32)
            acc = acc + w_ref[h:h + 1, :] * jnp.maximum(s, 0.0)
        dst[pl.ds(r0, tk), :] = jnp.where(r0 + k_iota <= dst_pos, acc, NEG)

    for dd in range(n_diag):
        score_tile(n_full + dd, qit_ref, wit_ref, scores, q_pos)

    n_beyond = seq_len - (j + 1) * tq

    def candidate(u):
        s = u ^ INT_MIN
        return lax.bitcast_convert_type(jnp.where(s < 0, s ^ 0x7FFFFFFF, s), F32)

    n_acc = 4

    def count_tile(c, cnts, cand):
        r0 = pl.multiple_of(c * tq, tq)
        sc = scores[pl.ds(r0, tq), :]
        cnts = list(cnts)
        for i in range(tq // SUBLANES):
            rows = sc[i * SUBLANES:(i + 1) * SUBLANES]
            cnts[i % n_acc] = jnp.where(rows >= cand, cnts[i % n_acc] + 1, cnts[i % n_acc])
        return tuple(cnts)

    def bit_step(b, thr_u):
        cand_u = thr_u | lax.shift_left(jnp.int32(1), 31 - b)
        cand = candidate(cand_u)
        zero = jnp.zeros((SUBLANES, tq), I32)
        cnts = lax.fori_loop(0, j + 1, functools.partial(count_tile, cand=cand), (zero,) * n_acc)
        cnt = (cnts[0] + cnts[1]) + (cnts[2] + cnts[3])
        total = jnp.sum(cnt, axis=0, keepdims=True) + jnp.where(cand <= NEG, n_beyond, 0)
        return jnp.where(total >= n_sel, cand_u, thr_u)

    thr = candidate(lax.fori_loop(0, 32, bit_step, jnp.zeros((1, tq), I32)))

    m_ref[...] = jnp.full(m_ref.shape, NEG, F32)
    l_ref[...] = jnp.zeros(l_ref.shape, F32)
    acc_ref[...] = jnp.zeros(acc_ref.shape, F32)
    rep = N_HEADS // N_KV_HEADS

    def logits(kt, slot):
        r0 = tile_start(kt)
        valid = jnp.logical_and(scores[pl.ds(r0, tk), :] >= thr, r0 + k_iota <= q_pos)
        bias = jnp.where(valid, 0.0, NEG)
        for g in range(N_KV_HEADS):
            k_t = k_ref[g, pl.ds(r0, tk), :]
            for h in range(g * rep, (g + 1) * rep):
                q_h = qt_ref[h * HEAD_DIM:(h + 1) * HEAD_DIM, :]
                lg_ref[slot, h] = jnp.dot(k_t, q_h, preferred_element_type=F32) + bias

    ones_rows = jnp.ones((2 * SUBLANES, tk), BF16)

    def softmax_values(kt, slot):
        r0 = tile_start(kt)
        for g in range(N_KV_HEADS):
            v_t = jnp.concatenate([vt_ref[g * HEAD_DIM:(g + 1) * HEAD_DIM, pl.ds(r0, tk)], ones_rows], axis=0)
            for h in range(g * rep, (g + 1) * rep):
                lg = lg_ref[slot, h]
                m_old = m_ref[h]
                m_new = jnp.maximum(m_old, jnp.max(lg, axis=0, keepdims=True))
                p = jnp.exp2(lg - m_new)
                alpha = jnp.exp2(m_old - m_new)
                pv = jnp.dot(v_t, p.astype(BF16), preferred_element_type=F32)
                l_ref[h] = alpha * l_ref[h] + pv[HEAD_DIM:HEAD_DIM + 1, :]
                acc_ref[h] = alpha * acc_ref[h] + pv[:HEAD_DIM, :]
                m_ref[h] = m_new

    logits(0, 0)

    def tile_pair(i, c, with_next):
        logits(2 * i + 1, 1)
        softmax_values(2 * i, 0)
        logits(jnp.minimum(2 * i + 2, n_tot - 1), 0)
        softmax_values(2 * i + 1, 1)
        if with_next:
            for kt in (2 * i, 2 * i + 1):
                score_tile(kt, qit_next_ref, wit_next_ref, scores_next, q_pos + tq)
        return c

    @pl.when(j < last_block)
    def _():
        lax.fori_loop(0, n_tot // 2, functools.partial(tile_pair, with_next=True), 0)

    @pl.when(j == last_block)
    def _():
        lax.fori_loop(0, n_tot // 2, functools.partial(tile_pair, with_next=False), 0)

    outs = [acc_ref[h] / l_ref[h] for h in range(N_HEADS)]
    y_ref[...] = jnp.concatenate(outs, axis=0).T.astype(BF16)


def _attention(qt, qit, wit, k, ki, vt, batch, seq_len):
    n = ki.shape[0]
    tq = AT_TQ
    nq = seq_len // tq
    n_sel = min(TOPK, seq_len // 4)
    qcol = lambda b, j: (0, b * nq + j)
    qcol_next = lambda b, j: (0, b * nq + jnp.minimum(j + 1, nq - 1))
    return pl.pallas_call(
        functools.partial(_attn_kernel, seq_len=seq_len, n_sel=n_sel),
        grid=(batch, nq),
        in_specs=[
            pl.BlockSpec((D_ATTN, tq), qcol),
            pl.BlockSpec((IDX_HEADS * IDX_DIM, tq), qcol),
            pl.BlockSpec((IDX_HEADS, tq), qcol),
            pl.BlockSpec((IDX_HEADS * IDX_DIM, tq), qcol_next),
            pl.BlockSpec((IDX_HEADS, tq), qcol_next),
            pl.BlockSpec((N_KV_HEADS, seq_len, HEAD_DIM), lambda b, j: (0, b, 0)),
            pl.BlockSpec((seq_len, IDX_DIM), lambda b, j: (b, 0)),
            pl.BlockSpec((D_KV, seq_len), lambda b, j: (0, b)),
        ],
        out_specs=pl.BlockSpec((tq, D_ATTN), lambda b, j: (b * nq + j, 0)),
        out_shape=jax.ShapeDtypeStruct((n, D_ATTN), BF16),
        scratch_shapes=[pltpu.VMEM((2, seq_len, tq), F32),
                        pltpu.VMEM((N_HEADS, 1, tq), F32),
                        pltpu.VMEM((N_HEADS, 1, tq), F32),
                        pltpu.VMEM((N_HEADS, HEAD_DIM, tq), F32),
                        pltpu.VMEM((2, N_HEADS, AT_TK, tq), F32)],
        compiler_params=_cparams(("arbitrary", "arbitrary")),
        name="attn",
    )(qt, qit, wit, qit, wit, k, ki, vt)


MIX_TM = 512


def _split_bf16(a):
    hi = a.astype(BF16)
    return hi, (a - hi.astype(F32)).astype(BF16)


def _mix_kernel(x_ref, ys_ref, ya_ref, gs_ref, ga_ref, wps_ref, wpa_ref, wo_ref, g2_ref,
                wr_hi_ref, wr_lo_ref, br_ref, x1_ref, h_ref, idx_ref, gate_ref):
    ps = jnp.dot(ys_ref[...], wps_ref[...], preferred_element_type=F32)
    pa = jnp.dot(ya_ref[...], wpa_ref[...], preferred_element_type=F32)
    mixed = gs_ref[...].astype(F32) * ps + ga_ref[...].astype(F32) * pa
    x1 = x_ref[...] + jnp.dot(mixed.astype(BF16), wo_ref[...], preferred_element_type=F32)
    h = x1 * lax.rsqrt(jnp.mean(x1 * x1, axis=-1, keepdims=True) + RMS_EPS) * g2_ref[...]
    tm = x1.shape[0]
    for j in range(D_MODEL // LANES):
        x1_ref[pl.ds(j, tm, stride=D_MODEL // LANES), :] = x1[:, j * LANES:(j + 1) * LANES]
        h_ref[pl.ds(j, tm, stride=D_MODEL // LANES), :] = h[:, j * LANES:(j + 1) * LANES]
    h_hi, h_lo = _split_bf16(h)
    logits = (jnp.dot(h_hi, wr_hi_ref[...], preferred_element_type=F32)
              + jnp.dot(h_hi, wr_lo_ref[...], preferred_element_type=F32)
              + jnp.dot(h_lo, wr_hi_ref[...], preferred_element_type=F32)) + br_ref[...]
    lane = lax.broadcasted_iota(I32, logits.shape, 1)
    rest = logits
    firsts, vals = [], []
    for _ in range(TOP_K_EXPERTS):
        m = jnp.max(rest, axis=-1, keepdims=True)
        first = jnp.minimum(jnp.min(jnp.where(rest == m, lane, N_EXPERTS), axis=-1, keepdims=True),
                            N_EXPERTS - 1)
        firsts.append(first)
        vals.append(m)
        rest = jnp.where(lane == first, -jnp.inf, rest)
    es = [jnp.exp(v - vals[0]) for v in vals]
    denom = es[0] + es[1] + es[2] + es[3]
    idx_ref[...] = jnp.concatenate(firsts, axis=1)
    gate_ref[...] = jnp.concatenate([e / denom for e in es], axis=1)


def _mix(x2, y_ssm, y_att, gs, ga, w_proj_ssm, w_proj_attn, w_out, norm2_g, w_router, b_router):
    n = x2.shape[0]
    tm = min(MIX_TM, n)
    chunks = D_MODEL // LANES
    row = lambda i: (i, 0)
    wr_hi, wr_lo = _split_bf16(w_router)
    consts = [w_proj_ssm.astype(BF16), w_proj_attn.astype(BF16), w_out.astype(BF16),
              norm2_g.reshape(1, D_MODEL), wr_hi, wr_lo, b_router.reshape(1, N_EXPERTS)]
    return pl.pallas_call(
        _mix_kernel,
        grid=(n // tm,),
        in_specs=[pl.BlockSpec((tm, D_MODEL), row), pl.BlockSpec((tm, D_SSM), row),
                  pl.BlockSpec((tm, D_ATTN), row), pl.BlockSpec((tm, D_MODEL), row),
                  pl.BlockSpec((tm, D_MODEL), row)] + [_const_spec(c.shape) for c in consts],
        out_specs=(pl.BlockSpec((tm * chunks, LANES), row), pl.BlockSpec((tm * chunks, LANES), row),
                   pl.BlockSpec((tm, TOP_K_EXPERTS), row), pl.BlockSpec((tm, TOP_K_EXPERTS), row)),
        out_shape=(jax.ShapeDtypeStruct((n * chunks, LANES), F32), jax.ShapeDtypeStruct((n * chunks, LANES), F32),
                   jax.ShapeDtypeStruct((n, TOP_K_EXPERTS), I32), jax.ShapeDtypeStruct((n, TOP_K_EXPERTS), F32)),
        compiler_params=_cparams(("arbitrary",)),
        name="mix",
    )(x2, y_ssm, y_att, gs, ga, *consts)


MOE_TT = 2048
MOE_RB = 288
ROW_CHUNKS = D_MODEL // LANES
assert ROW_CHUNKS == SUBLANES
MOE_PITCH = MOE_RB + SUBLANES


def _moe_kernel(cnt_ref, off_ref, tok_ref, gate_ref, h_ref, win_ref, bin_ref, wout_ref, bout_ref,
                out_ref, xa_ref, xb_ref, xx_ref, ya_ref, yb_ref, yx_ref):
    i, e = pl.program_id(0), pl.program_id(1)
    trash = out_ref.shape[0] // SUBLANES - 1
    here = i * N_EXPERTS + e
    n, start = cnt_ref[here], off_ref[here]
    start_next = off_ref[i * N_EXPERTS + jnp.minimum(e + 1, N_EXPERTS - 1)]
    prev = i * N_EXPERTS + jnp.maximum(e - 1, 0)
    start_prev = off_ref[prev]
    n_prev = jnp.where(e > 0, jnp.minimum(cnt_ref[prev], MOE_RB), 0)

    def slab(t):
        return pl.ds(pl.multiple_of(t * SUBLANES, SUBLANES), SUBLANES)

    def tile_rows(r):
        return pl.ds(r, SUBLANES, stride=MOE_PITCH)

    def gather_row(base, r, x_ref):
        x_ref[tile_rows(r), :] = h_ref[slab(tok_ref[0, 0, base + r]), :]

    def gather_loop(base, x_ref):
        def body(r8, c):
            for rr in range(SUBLANES):
                gather_row(base, r8 * SUBLANES + rr, x_ref)
            return c

        lax.fori_loop(0, MOE_RB // SUBLANES, body, 0)

    def expert_mlp(x_ref, y_ref):
        xg = jnp.concatenate([x_ref[j * MOE_PITCH:j * MOE_PITCH + MOE_RB, :] for j in range(ROW_CHUNKS)],
                             axis=1).astype(BF16)
        z = jnp.dot(xg, win_ref[0], preferred_element_type=F32) + bin_ref[0]
        half = MXU_COLS // 2
        groups = range(2 * D_EXPERT // MXU_COLS)
        zg = jnp.concatenate([z[:, c * MXU_COLS:c * MXU_COLS + half] for c in groups], axis=1)
        zl = jnp.concatenate([z[:, c * MXU_COLS + half:(c + 1) * MXU_COLS] for c in groups], axis=1)
        zg = jnp.minimum(zg, SWIGLU_LIMIT)
        zl = jnp.clip(zl, -SWIGLU_LIMIT, SWIGLU_LIMIT)
        act = zg * jax.nn.sigmoid(SWIGLU_ALPHA * zg) * (zl + 1.0)
        y = jnp.dot(act.astype(BF16), wout_ref[0], preferred_element_type=F32) + bout_ref[0]
        for j in range(ROW_CHUNKS):
            y_ref[j * MOE_PITCH:j * MOE_PITCH + MOE_RB, :] = y[:, j * LANES:(j + 1) * LANES]

    def updated(base, r, n_valid, y_ref):
        t = jnp.where(r < n_valid, tok_ref[0, 0, base + r], trash)
        return t, out_ref[slab(t), :] + gate_ref[0, 0, base + r] * y_ref[tile_rows(r), :]

    def scatter_group(base, r0, n_valid, y_ref):
        rows = [updated(base, r0 + rr, n_valid, y_ref) for rr in range(SUBLANES)]
        for t, v in rows:
            out_ref[slab(t), :] = v

    def scatter_loop(base, n_valid, y_ref):
        def body(r8, c):
            scatter_group(base, r8 * SUBLANES, n_valid, y_ref)
            return c

        lax.fori_loop(0, (n_valid + SUBLANES - 1) // SUBLANES, body, 0)

    @pl.when(e == 0)
    def _():
        out_ref[...] = jnp.zeros(out_ref.shape, F32)
        gather_loop(start, xa_ref)

    @pl.when(jnp.logical_and(i == 0, e == 0))
    def _():
        yb_ref[...] = jnp.zeros(yb_ref.shape, F32)

    def pipelined(x_cur, x_next, y_cur, y_prev):
        for r in range(MOE_RB):
            gather_row(start_next, r, x_next)
        expert_mlp(x_cur, y_cur)
        for r0 in range(0, MOE_RB, SUBLANES):
            scatter_group(start_prev, r0, n_prev, y_prev)

    @pl.when(e % 2 == 0)
    def _():
        pipelined(xa_ref, xb_ref, ya_ref, yb_ref)

    @pl.when(e % 2 == 1)
    def _():
        pipelined(xb_ref, xa_ref, yb_ref, ya_ref)

    def extra_block(b, c):
        base = start + b * MOE_RB
        gather_loop(base, xx_ref)
        expert_mlp(xx_ref, yx_ref)
        scatter_loop(base, jnp.minimum(MOE_RB, n - b * MOE_RB), yx_ref)
        return c

    lax.fori_loop(1, (n + MOE_RB - 1) // MOE_RB, extra_block, 0)

    @pl.when(e == N_EXPERTS - 1)
    def _():
        scatter_loop(start, jnp.minimum(n, MOE_RB), yb_ref)


FINAL_TM = 512


def _final_kernel(x1_ref, moe_ref, g_ref, o_ref):
    tm = o_ref.shape[0]
    v = jnp.concatenate([x1_ref[pl.ds(j, tm, stride=ROW_CHUNKS), :] + moe_ref[0, pl.ds(j, tm, stride=ROW_CHUNKS), :]
                         for j in range(ROW_CHUNKS)], axis=1)
    o_ref[...] = v * lax.rsqrt(jnp.mean(v * v, axis=-1, keepdims=True) + RMS_EPS) * g_ref[...]


MXU_COLS = 256


def _expert_weights_kernel(win_ref, wout_ref, perm_ref, winp_ref, woutb_ref):
    w = win_ref[0].astype(BF16)
    for c in range(w.shape[1] // MXU_COLS):
        cols = slice(c * MXU_COLS, (c + 1) * MXU_COLS)
        winp_ref[0, :, cols] = jnp.dot(w[:, cols], perm_ref[...], preferred_element_type=F32).astype(BF16)
    woutb_ref[0] = wout_ref[0].astype(BF16)


def _expert_weights(w_moe_in, w_moe_out):
    half = MXU_COLS // 2
    src = jnp.concatenate([jnp.arange(half) * 2, jnp.arange(half) * 2 + 1])
    perm = (jnp.arange(MXU_COLS)[:, None] == src[None, :]).astype(BF16)
    per_expert = lambda a: pl.BlockSpec((1,) + a.shape[1:], lambda e: (e, 0, 0))
    return pl.pallas_call(
        _expert_weights_kernel,
        grid=(N_EXPERTS,),
        in_specs=[per_expert(w_moe_in), per_expert(w_moe_out), _const_spec(perm.shape)],
        out_specs=(per_expert(w_moe_in), per_expert(w_moe_out)),
        out_shape=(jax.ShapeDtypeStruct(w_moe_in.shape, BF16), jax.ShapeDtypeStruct(w_moe_out.shape, BF16)),
        compiler_params=_cparams(("arbitrary",)),
        name="expert_weights",
    )(w_moe_in, w_moe_out, perm)


def _moe(h, x1, idx4, gate4, w_moe_in, b_moe_in, w_moe_out, b_moe_out, norm_f_g):
    n = h.shape[0] // ROW_CHUNKS
    tt = min(MOE_TT, n)
    n_tiles = n // tt
    pairs = TOP_K_EXPERTS * tt
    cap = pairs + MOE_RB

    local = (jnp.arange(n, dtype=I32) % tt)[:, None]
    keys = (idx4 * tt + local).reshape(n_tiles, pairs)
    keys, gate = lax.sort((keys, gate4.reshape(n_tiles, pairs)), dimension=1, num_keys=1)
    bounds = jnp.arange(N_EXPERTS + 1, dtype=I32) * tt
    below = jnp.sum(keys[:, :, None] < bounds[None, None, :], axis=1).astype(I32)
    off, cnt = below[:, :-1], below[:, 1:] - below[:, :-1]
    pad = ((0, 0), (0, cap - pairs))
    tok = jnp.pad(keys % tt, pad).reshape(n_tiles, 1, cap)
    gate = jnp.pad(gate, pad).reshape(n_tiles, 1, cap)

    win, wout = _expert_weights(w_moe_in, w_moe_out)
    half = MXU_COLS // 2
    bin_ = jnp.swapaxes(b_moe_in.reshape(N_EXPERTS, -1, half, 2), 2, 3).reshape(N_EXPERTS, 1, 2 * D_EXPERT)
    bout = b_moe_out.reshape(N_EXPERTS, 1, D_MODEL)

    out_rows = (tt + 1) * ROW_CHUNKS
    tile1 = pl.BlockSpec((tt * ROW_CHUNKS, LANES), lambda i, e, *_: (i, 0), pipeline_mode=pl.Buffered(1))
    smem_list = pl.BlockSpec((1, 1, cap), lambda i, e, *_: (i, 0, 0), memory_space=pltpu.SMEM)
    per_expert = lambda shape: pl.BlockSpec((1,) + shape, lambda i, e, *_: (e, 0, 0))
    row_tile = pltpu.VMEM((ROW_CHUNKS * MOE_PITCH, LANES), F32)
    grid_spec = pltpu.PrefetchScalarGridSpec(
        num_scalar_prefetch=2,
        grid=(n_tiles, N_EXPERTS),
        in_specs=[smem_list, smem_list, tile1,
                  per_expert((D_MODEL, 2 * D_EXPERT)), per_expert((1, 2 * D_EXPERT)),
                  per_expert((D_EXPERT, D_MODEL)), per_expert((1, D_MODEL))],
        out_specs=pl.BlockSpec((out_rows, LANES), lambda i, e, *_: (i, 0)),
        scratch_shapes=[row_tile] * 6,
    )
    moe_out = pl.pallas_call(
        _moe_kernel,
        grid_spec=grid_spec,
        out_shape=jax.ShapeDtypeStruct((n_tiles * out_rows, LANES), F32),
        compiler_params=_cparams(("arbitrary", "arbitrary")),
        name="moe",
    )(cnt.reshape(-1), off.reshape(-1).astype(I32), tok, gate, h, win, bin_, wout, bout)

    tm = min(FINAL_TM, tt)
    per_tile = tt // tm
    return pl.pallas_call(
        _final_kernel,
        grid=(n_tiles, per_tile),
        in_specs=[pl.BlockSpec((tm * ROW_CHUNKS, LANES), lambda i, j: (i * per_tile + j, 0)),
                  pl.BlockSpec((1, tm * ROW_CHUNKS, LANES), lambda i, j: (i, j, 0)),
                  _const_spec((1, D_MODEL))],
        out_specs=pl.BlockSpec((tm, D_MODEL), lambda i, j: (i * per_tile + j, 0)),
        out_shape=jax.ShapeDtypeStruct((n, D_MODEL), F32),
        compiler_params=_cparams(("arbitrary", "arbitrary")),
        name="final_norm",
    )(x1, moe_out.reshape(n_tiles, out_rows, LANES), norm_f_g.reshape(1, D_MODEL))


def kernel(x, norm1_g, w_in, b_gate, ssm_lam_re, ssm_lam_im, ssm_log_dt, ssm_b_re, ssm_b_im, ssm_c_re, ssm_c_im, ssm_d, w_glu, b_glu, w_proj_ssm, w_proj_attn, w_out, norm2_g, w_router, b_router, w_moe_in, b_moe_in, w_moe_out, b_moe_out, norm_f_g):
    bsz, seq_len, _ = x.shape
    n = bsz * seq_len
    x2 = x.reshape(n, D_MODEL)
    u, qt, k, vt, qit, ki, wit, gs, ga = _in_proj(x2, norm1_g[0], w_in[0], b_gate[0], seq_len)
    y_ssm = _s5(u, bsz, seq_len, ssm_lam_re[0], ssm_lam_im[0], ssm_log_dt[0], ssm_b_re[0], ssm_b_im[0],
                ssm_c_re[0], ssm_c_im[0], ssm_d[0], w_glu[0], b_glu[0])
    y_att = _attention(qt, qit, wit, k, ki, vt, bsz, seq_len)
    x1, h, idx4, gate4 = _mix(x2, y_ssm, y_att, gs, ga, w_proj_ssm[0], w_proj_attn[0], w_out[0], norm2_g[0],
                              w_router[0], b_router[0])
    out = _moe(h, x1, idx4, gate4, w_moe_in[0], b_moe_in[0], w_moe_out[0], b_moe_out[0], norm_f_g)
    return out.reshape(x.shape)
```

```python
import functools
import math

import numpy as np
import jax
import jax.numpy as jnp
from jax import lax
from jax.experimental import pallas as pl
from jax.experimental.pallas import tpu as pltpu

D_MODEL = 1024
D_SSM = 512
SSM_GROUP = 16
N_SSM_GROUPS = 32
SSM_STATE = 64
N_HEADS = 8
N_KV_HEADS = 2
HEAD_DIM = 64
D_ATTN = N_HEADS * HEAD_DIM
D_KV = N_KV_HEADS * HEAD_DIM
IDX_HEADS = 16
IDX_DIM = 64
TOPK = 256
ROPE_THETA = 10000.0
N_EXPERTS = 32
TOP_K_EXPERTS = 4
D_EXPERT = D_MODEL
SWIGLU_LIMIT = 7.0
SWIGLU_ALPHA = 1.702
RMS_EPS = 1e-5
NEG = -1e30

LANES = 128
SUBLANES = 8
VMEM_LIMIT = 56 * 1024 * 1024

F32 = jnp.float32
BF16 = jnp.bfloat16
I32 = jnp.int32


def _cparams(sem):
    return pltpu.CompilerParams(dimension_semantics=sem, vmem_limit_bytes=VMEM_LIMIT)


def _const_spec(shape):
    nd = len(shape)
    return pl.BlockSpec(shape, lambda *_: (0,) * nd)


IN_TM = 512


def _in_proj_kernel(x_ref, g_ref, cos_ref, sin_ref, bg_ref,
                    wu_ref, wq_ref, wkv_ref, wqi_ref, wkw_ref, wgs_ref, wga_ref,
                    u_ref, qt_ref, k_ref, vt_ref, qit_ref, ki_ref, wit_ref, gs_ref, ga_ref):
    x = x_ref[...]
    xn = x * lax.rsqrt(jnp.mean(x * x, axis=-1, keepdims=True) + RMS_EPS) * g_ref[...]
    xb = xn.astype(BF16)

    def mm(w_ref):
        return jnp.dot(xb, w_ref[...], preferred_element_type=F32)

    cos = cos_ref[...]
    sin = sin_ref[...]
    lane = lax.broadcasted_iota(I32, cos.shape, 1)
    first_half = lane % HEAD_DIM < HEAD_DIM // 2

    def rope(a):
        cols = []
        for c in range(a.shape[1] // LANES):
            blk = a[:, c * LANES:(c + 1) * LANES]
            partner = jnp.where(first_half, pltpu.roll(blk, LANES - HEAD_DIM // 2, 1),
                                pltpu.roll(blk, HEAD_DIM // 2, 1))
            cols.append(blk * cos + partner * sin)
        return jnp.concatenate(cols, axis=1)

    u_ref[...] = mm(wu_ref)
    q = rope(mm(wq_ref)) * (HEAD_DIM ** -0.5 * math.log2(math.e))
    qt_ref[...] = q.T.astype(BF16)
    kv = mm(wkv_ref)
    k = rope(kv[:, :D_KV])
    for g in range(N_KV_HEADS):
        k_ref[g] = k[:, g * HEAD_DIM:(g + 1) * HEAD_DIM].astype(BF16)
    vt_ref[...] = kv[:, D_KV:].T.astype(BF16)
    kw = mm(wkw_ref)
    ki_ref[...] = rope(kw[:, :LANES])[:, :IDX_DIM].astype(BF16)
    qi = rope(mm(wqi_ref)) * (IDX_DIM ** -0.5)
    qit_ref[...] = qi.T.astype(BF16)
    wit_ref[...] = (kw[:, LANES:2 * LANES].T)[:IDX_HEADS] * (IDX_HEADS ** -0.5)
    gs_ref[...] = jax.nn.sigmoid(mm(wgs_ref) + bg_ref[0:1, :]).astype(BF16)
    ga_ref[...] = jax.nn.sigmoid(mm(wga_ref) + bg_ref[1:2, :]).astype(BF16)


def _in_proj(x2, norm1_g, w_in, b_gate, seq_len):
    n = x2.shape[0]
    tm = min(IN_TM, seq_len)
    o = 0
    parts = []
    for width in (D_SSM, D_ATTN, D_KV, D_KV, IDX_HEADS * IDX_DIM, IDX_DIM, IDX_HEADS, D_MODEL, D_MODEL):
        parts.append(w_in[:, o:o + width])
        o += width
    wu, wq, wk, wv, wqi, wki, wwi, wgs, wga = parts
    wkv = jnp.concatenate([wk, wv], axis=1)
    wkw = jnp.concatenate([wki, jnp.zeros((D_MODEL, LANES - IDX_DIM), F32),
                           wwi, jnp.zeros((D_MODEL, LANES - IDX_HEADS), F32)], axis=1)
    weights = [w.astype(BF16) for w in (wu, wq, wkv, wqi, wkw, wgs, wga)]

    half = HEAD_DIM // 2
    inv = ROPE_THETA ** (-jnp.arange(half, dtype=F32) / half)
    ang = jnp.arange(seq_len, dtype=F32)[:, None] * inv[None, :]
    cos = jnp.tile(jnp.cos(ang), (1, 4))
    sin = jnp.tile(jnp.concatenate([-jnp.sin(ang), jnp.sin(ang)], axis=1), (1, 2))

    tiles_per_seq = seq_len // tm
    row = lambda i: (i, 0)
    col = lambda i: (0, i)
    out_shapes = (
        jax.ShapeDtypeStruct((n, D_SSM), F32),
        jax.ShapeDtypeStruct((D_ATTN, n), BF16),
        jax.ShapeDtypeStruct((N_KV_HEADS, n, HEAD_DIM), BF16),
        jax.ShapeDtypeStruct((D_KV, n), BF16),
        jax.ShapeDtypeStruct((IDX_HEADS * IDX_DIM, n), BF16),
        jax.ShapeDtypeStruct((n, IDX_DIM), BF16),
        jax.ShapeDtypeStruct((IDX_HEADS, n), F32),
        jax.ShapeDtypeStruct((n, D_MODEL), BF16),
        jax.ShapeDtypeStruct((n, D_MODEL), BF16),
    )
    out_specs = (
        pl.BlockSpec((tm, D_SSM), row),
        pl.BlockSpec((D_ATTN, tm), col),
        pl.BlockSpec((N_KV_HEADS, tm, HEAD_DIM), lambda i: (0, i, 0)),
        pl.BlockSpec((D_KV, tm), col),
        pl.BlockSpec((IDX_HEADS * IDX_DIM, tm), col),
        pl.BlockSpec((tm, IDX_DIM), row),
        pl.BlockSpec((IDX_HEADS, tm), col),
        pl.BlockSpec((tm, D_MODEL), row),
        pl.BlockSpec((tm, D_MODEL), row),
    )
    in_specs = [
        pl.BlockSpec((tm, D_MODEL), row),
        _const_spec((1, D_MODEL)),
        pl.BlockSpec((tm, LANES), lambda i: (i % tiles_per_seq, 0)),
        pl.BlockSpec((tm, LANES), lambda i: (i % tiles_per_seq, 0)),
        _const_spec((2, D_MODEL)),
    ] + [_const_spec(w.shape) for w in weights]
    return pl.pallas_call(
        _in_proj_kernel,
        grid=(n // tm,),
        in_specs=in_specs,
        out_specs=out_specs,
        out_shape=out_shapes,
        compiler_params=_cparams(("arbitrary",)),
        name="in_proj",
    )(x2, norm1_g.reshape(1, D_MODEL), cos, sin, b_gate, *weights)


S5_T = 512
S5_CHUNK = 512
N_STATE = N_SSM_GROUPS * SSM_STATE
S5_HALF_IN = D_SSM // 2
S5_HALF_ST = N_STATE // 2


def _cmul(ar, ai, br, bi):
    return ar * br - ai * bi, ar * bi + ai * br


def _s5_kernel(u_ref, lr_ref, li_ref, ldt_ref, bre_ref, bim_ref, cre_ref, cim_ref, d_ref,
               wglu_ref, bglu_ref, y_ref,
               bb_ref, cc_ref, pw_re_ref, pw_im_ref, st_re_ref, st_im_ref, carry_re_ref, carry_im_ref):
    first = jnp.logical_and(pl.program_id(0) == 0, pl.program_id(1) == 0)

    @pl.when(first)
    def _():
        lr, li = lr_ref[...], li_ref[...]
        dt = jnp.exp(ldt_ref[...])
        mag = jnp.exp(lr * dt)
        ar = mag * jnp.cos(li * dt)
        ai = mag * jnp.sin(li * dt)
        den = lr * lr + li * li
        zr = ((ar - 1.0) * lr + ai * li) / den
        zi = (ai * lr - (ar - 1.0) * li) / den
        for h in range(2):
            rows = slice(h * S5_HALF_IN, (h + 1) * S5_HALF_IN)
            cols = slice(h * S5_HALF_ST, (h + 1) * S5_HALF_ST)
            bre, bim = bre_ref[rows, cols], bim_ref[rows, cols]
            bb_ref[h, :, :S5_HALF_ST] = (zr[:, cols] * bre - zi[:, cols] * bim).astype(BF16)
            bb_ref[h, :, S5_HALF_ST:] = (zr[:, cols] * bim + zi[:, cols] * bre).astype(BF16)
            cc_ref[h, :S5_HALF_ST, :] = cre_ref[cols, rows].astype(BF16)
            cc_ref[h, S5_HALF_ST:, :] = (-cim_ref[cols, rows]).astype(BF16)
        pr, pi = ar, ai
        pw_re_ref[0:1, :] = pr
        pw_im_ref[0:1, :] = pi
        for n in range(1, SUBLANES):
            pr, pi = _cmul(pr, pi, ar, ai)
            pw_re_ref[n:n + 1, :] = pr
            pw_im_ref[n:n + 1, :] = pi

    @pl.when(pl.program_id(1) == 0)
    def _():
        carry_re_ref[...] = jnp.zeros_like(carry_re_ref)
        carry_im_ref[...] = jnp.zeros_like(carry_im_ref)

    u = u_ref[...]
    ub = u.astype(BF16)
    for h in range(2):
        bu = jnp.dot(ub[:, h * S5_HALF_IN:(h + 1) * S5_HALF_IN], bb_ref[h], preferred_element_type=F32)
        st_re_ref[:, h * S5_HALF_ST:(h + 1) * S5_HALF_ST] = bu[:, :S5_HALF_ST]
        st_im_ref[:, h * S5_HALF_ST:(h + 1) * S5_HALF_ST] = bu[:, S5_HALF_ST:]

    t_len = u.shape[0]
    row = lax.broadcasted_iota(I32, (SUBLANES, S5_CHUNK), 0)
    for c in range(N_STATE // S5_CHUNK):
        lanes = slice(c * S5_CHUNK, (c + 1) * S5_CHUNK)
        p_re, p_im = pw_re_ref[:, lanes], pw_im_ref[:, lanes]
        steps = []
        for d in (1, 2, 4):
            a_re = jnp.where(row >= d, jnp.broadcast_to(p_re[d - 1:d, :], row.shape), 0.0)
            a_im = jnp.where(row >= d, jnp.broadcast_to(p_im[d - 1:d, :], row.shape), 0.0)
            steps.append((d, a_re, a_im))

        def block(i, carry):
            c_re, c_im = carry
            r0 = pl.multiple_of(i * SUBLANES, SUBLANES)
            xr = st_re_ref[pl.ds(r0, SUBLANES), lanes]
            xi = st_im_ref[pl.ds(r0, SUBLANES), lanes]
            for d, a_re, a_im in steps:
                sr = pltpu.roll(xr, d, 0)
                si = pltpu.roll(xi, d, 0)
                tr, ti = _cmul(a_re, a_im, sr, si)
                xr, xi = xr + tr, xi + ti
            tr, ti = _cmul(p_re, p_im, jnp.broadcast_to(c_re, xr.shape), jnp.broadcast_to(c_im, xi.shape))
            xr, xi = xr + tr, xi + ti
            st_re_ref[pl.ds(r0, SUBLANES), lanes] = xr
            st_im_ref[pl.ds(r0, SUBLANES), lanes] = xi
            return xr[SUBLANES - 1:, :], xi[SUBLANES - 1:, :]

        def two_blocks(j, carry):
            return block(2 * j + 1, block(2 * j, carry))

        c_re, c_im = lax.fori_loop(0, t_len // (2 * SUBLANES), two_blocks,
                                   (carry_re_ref[:, lanes], carry_im_ref[:, lanes]))
        carry_re_ref[:, lanes] = c_re
        carry_im_ref[:, lanes] = c_im

    ys = []
    for h in range(2):
        cols = slice(h * S5_HALF_ST, (h + 1) * S5_HALF_ST)
        xs = jnp.concatenate([st_re_ref[:, cols], st_im_ref[:, cols]], axis=1).astype(BF16)
        ys.append(jnp.dot(xs, cc_ref[h], preferred_element_type=F32))
    y = jnp.concatenate(ys, axis=1) + d_ref[...] * u
    y = jax.nn.gelu(y)
    gate = jnp.dot(y.astype(BF16), wglu_ref[...], preferred_element_type=F32) + bglu_ref[...]
    y_ref[...] = (y * jax.nn.sigmoid(gate)).astype(BF16)


def _s5(u, batch, seq_len, lam_re, lam_im, log_dt, b_re, b_im, c_re, c_im, d_skip, w_glu, b_glu):
    n = u.shape[0]
    t = min(S5_T, seq_len)
    g, p, hh = N_SSM_GROUPS, SSM_STATE, SSM_GROUP
    eye = jnp.eye(g, dtype=F32)
    bre_bd = jnp.einsum("gph,gk->ghkp", b_re, eye).reshape(D_SSM, N_STATE)
    bim_bd = jnp.einsum("gph,gk->ghkp", b_im, eye).reshape(D_SSM, N_STATE)
    cre_bd = jnp.einsum("ghp,gk->gpkh", c_re, eye).reshape(N_STATE, D_SSM)
    cim_bd = jnp.einsum("ghp,gk->gpkh", c_im, eye).reshape(N_STATE, D_SSM)
    lr = lam_re.reshape(1, N_STATE)
    li = lam_im.reshape(1, N_STATE)
    ldt = jnp.repeat(log_dt, p).reshape(1, N_STATE)
    tiles = seq_len // t
    consts = [lr, li, ldt, bre_bd, bim_bd, cre_bd, cim_bd, d_skip.reshape(1, D_SSM),
              w_glu.astype(BF16), b_glu.reshape(1, D_SSM)]
    return pl.pallas_call(
        _s5_kernel,
        grid=(batch, tiles),
        in_specs=[pl.BlockSpec((t, D_SSM), lambda b, i: (b * tiles + i, 0))]
        + [_const_spec(c.shape) for c in consts],
        out_specs=pl.BlockSpec((t, D_SSM), lambda b, i: (b * tiles + i, 0)),
        out_shape=jax.ShapeDtypeStruct((n, D_SSM), BF16),
        scratch_shapes=[
            pltpu.VMEM((2, S5_HALF_IN, 2 * S5_HALF_ST), BF16),
            pltpu.VMEM((2, 2 * S5_HALF_ST, S5_HALF_IN), BF16),
            pltpu.VMEM((SUBLANES, N_STATE), F32),
            pltpu.VMEM((SUBLANES, N_STATE), F32),
            pltpu.VMEM((t, N_STATE), F32),
            pltpu.VMEM((t, N_STATE), F32),
            pltpu.VMEM((1, N_STATE), F32),
            pltpu.VMEM((1, N_STATE), F32),
        ],
        compiler_params=_cparams(("arbitrary", "arbitrary")),
        name="s5",
    )(u, *consts)


AT_TQ = 256
AT_TK = 128
INT_MIN = -2 ** 31


def _attn_kernel(qt_ref, qit_ref, wit_ref, qit_next_ref, wit_next_ref, k_ref, ki_ref, vt_ref, y_ref,
                 score_ref, m_ref, l_ref, acc_ref, lg_ref, cut_ref, *, seq_len, n_sel):
    j = pl.program_id(1)
    last_block = pl.num_programs(1) - 1
    tq, tk = AT_TQ, AT_TK
    n_diag = tq // tk
    n_full = j * n_diag
    n_tot = n_full + n_diag
    q_pos = j * tq + lax.broadcasted_iota(I32, (tk, tq), 1)
    k_iota = lax.broadcasted_iota(I32, (tk, tq), 0)
    scores = score_ref.at[j % 2]
    scores_next = score_ref.at[(j + 1) % 2]

    def tile_start(kt):
        return pl.multiple_of(kt * tk, tk)

    def score_tile(kt, q_ref, w_ref, dst, dst_pos):
        r0 = tile_start(kt)
        ki_t = ki_ref[pl.ds(r0, tk), :]
        acc = jnp.zeros((tk, tq), F32)
        for h in range(IDX_HEADS):
            s = jnp.dot(ki_t, q_ref[h * IDX_DIM:(h + 1) * IDX_DIM, :], preferred_element_type=F32)
            acc = acc + w_ref[h:h + 1, :] * jnp.maximum(s, 0.0)
        dst[pl.ds(r0, tk), :] = jnp.where(r0 + k_iota <= dst_pos, acc, NEG)

    for dd in range(n_diag):
        score_tile(n_full + dd, qit_ref, wit_ref, scores, q_pos)

    n_beyond = seq_len - (j + 1) * tq

    def candidate(u):
        s = u ^ INT_MIN
        return lax.bitcast_convert_type(jnp.where(s < 0, s ^ 0x7FFFFFFF, s), F32)

    n_acc = 4

    def count_tile(c, cnts, cand):
        r0 = pl.multiple_of(c * tq, tq)
        sc = scores[pl.ds(r0, tq), :]
        cnts = list(cnts)
        for i in range(tq // SUBLANES):
            rows = sc[i * SUBLANES:(i + 1) * SUBLANES]
            cnts[i % n_acc] = jnp.where(rows >= cand, cnts[i % n_acc] + 1, cnts[i % n_acc])
        return tuple(cnts)

    def bit_step(b, thr_u):
        cand_u = thr_u | lax.shift_left(jnp.int32(1), 31 - b)
        cand = candidate(cand_u)
        zero = jnp.zeros((SUBLANES, tq), I32)
        cnts = lax.fori_loop(0, j + 1, functools.partial(count_tile, cand=cand), (zero,) * n_acc)
        cnt = (cnts[0] + cnts[1]) + (cnts[2] + cnts[3])
        total = jnp.sum(cnt, axis=0, keepdims=True) + jnp.where(cand <= NEG, n_beyond, 0)
        return jnp.where(total >= n_sel, cand_u, thr_u)

    thr = candidate(lax.fori_loop(0, 32, bit_step, jnp.zeros((1, tq), I32)))

    sub_iota = lax.broadcasted_iota(I32, (SUBLANES, tq), 0)

    def rows_of(c):
        r0 = pl.multiple_of(c * tq, tq)
        sc = scores[pl.ds(r0, tq), :]
        return [(sc[i * SUBLANES:(i + 1) * SUBLANES], r0 + i * SUBLANES + sub_iota)
                for i in range(tq // SUBLANES)]

    def tie_tile(c, carry):
        gt, ge = carry
        for rows, _ in rows_of(c):
            gt = jnp.where(rows > thr, gt + 1, gt)
            ge = jnp.where(rows >= thr, ge + 1, ge)
        return gt, ge

    zero = jnp.zeros((SUBLANES, tq), I32)
    gt, ge = lax.fori_loop(0, j + 1, tie_tile, (zero, zero))
    n_gt = jnp.sum(gt, axis=0, keepdims=True) + jnp.where(thr < NEG, n_beyond, 0)
    n_ge = jnp.sum(ge, axis=0, keepdims=True) + jnp.where(thr <= NEG, n_beyond, 0)
    need = n_sel - n_gt
    cut_ref[...] = jnp.full((1, tq), seq_len, I32)

    @pl.when(jnp.max(n_ge) > n_sel)
    def _():
        n_bits = (seq_len - 1).bit_length()

        def eq_tile(c, cnt, cand):
            for rows, pos in rows_of(c):
                cnt = jnp.where(jnp.logical_and(rows == thr, pos < cand), cnt + 1, cnt)
            return cnt

        def index_step(b, c0):
            cand = c0 | lax.shift_left(jnp.int32(1), n_bits - 1 - b)
            cnt = lax.fori_loop(0, j + 1, functools.partial(eq_tile, cand=cand), zero)
            below = jnp.sum(cnt, axis=0, keepdims=True) + jnp.where(
                thr == NEG, jnp.clip(cand - (j + 1) * tq, 0, n_beyond), 0)
            return jnp.where(below < need, cand, c0)

        cut_ref[...] = lax.fori_loop(0, n_bits, index_step, jnp.zeros((1, tq), I32))

    cut = cut_ref[...]

    m_ref[...] = jnp.full(m_ref.shape, NEG, F32)
    l_ref[...] = jnp.zeros(l_ref.shape, F32)
    acc_ref[...] = jnp.zeros(acc_ref.shape, F32)
    rep = N_HEADS // N_KV_HEADS

    q_row = q_pos[0:1, :]
    tie_row = jnp.minimum(cut, q_row)

    def logits(kt, slot):
        r0 = tile_start(kt)
        sc = scores[pl.ds(r0, tk), :]
        last = jnp.where(sc > thr, q_row, jnp.where(sc == thr, tie_row, -1))
        bias = jnp.where(r0 + k_iota <= last, 0.0, NEG)
        for g in range(N_KV_HEADS):
            k_t = k_ref[g, pl.ds(r0, tk), :]
            for h in range(g * rep, (g + 1) * rep):
                q_h = qt_ref[h * HEAD_DIM:(h + 1) * HEAD_DIM, :]
                lg_ref[slot, h] = jnp.dot(k_t, q_h, preferred_element_type=F32) + bias

    ones_rows = jnp.ones((2 * SUBLANES, tk), BF16)

    def softmax_values(kt, slot):
        r0 = tile_start(kt)
        for g in range(N_KV_HEADS):
            v_t = jnp.concatenate([vt_ref[g * HEAD_DIM:(g + 1) * HEAD_DIM, pl.ds(r0, tk)], ones_rows], axis=0)
            for h in range(g * rep, (g + 1) * rep):
                lg = lg_ref[slot, h]
                m_old = m_ref[h]
                m_new = jnp.maximum(m_old, jnp.max(lg, axis=0, keepdims=True))
                p = jnp.exp2(lg - m_new)
                alpha = jnp.exp2(m_old - m_new)
                pv = jnp.dot(v_t, p.astype(BF16), preferred_element_type=F32)
                l_ref[h] = alpha * l_ref[h] + pv[HEAD_DIM:HEAD_DIM + 1, :]
                acc_ref[h] = alpha * acc_ref[h] + pv[:HEAD_DIM, :]
                m_ref[h] = m_new

    logits(0, 0)

    def tile_pair(i, c, with_next):
        logits(2 * i + 1, 1)
        softmax_values(2 * i, 0)
        logits(jnp.minimum(2 * i + 2, n_tot - 1), 0)
        softmax_values(2 * i + 1, 1)
        if with_next:
            for kt in (2 * i, 2 * i + 1):
                score_tile(kt, qit_next_ref, wit_next_ref, scores_next, q_pos + tq)
        return c

    @pl.when(j < last_block)
    def _():
        lax.fori_loop(0, n_tot // 2, functools.partial(tile_pair, with_next=True), 0)

    @pl.when(j == last_block)
    def _():
        lax.fori_loop(0, n_tot // 2, functools.partial(tile_pair, with_next=False), 0)

    outs = [acc_ref[h] / l_ref[h] for h in range(N_HEADS)]
    y_ref[...] = jnp.concatenate(outs, axis=0).T.astype(BF16)


def _attention(qt, qit, wit, k, ki, vt, batch, seq_len):
    n = ki.shape[0]
    tq = AT_TQ
    nq = seq_len // tq
    n_sel = min(TOPK, seq_len // 4)
    qcol = lambda b, j: (0, b * nq + j)
    qcol_next = lambda b, j: (0, b * nq + jnp.minimum(j + 1, nq - 1))
    return pl.pallas_call(
        functools.partial(_attn_kernel, seq_len=seq_len, n_sel=n_sel),
        grid=(batch, nq),
        in_specs=[
            pl.BlockSpec((D_ATTN, tq), qcol),
            pl.BlockSpec((IDX_HEADS * IDX_DIM, tq), qcol),
            pl.BlockSpec((IDX_HEADS, tq), qcol),
            pl.BlockSpec((IDX_HEADS * IDX_DIM, tq), qcol_next),
            pl.BlockSpec((IDX_HEADS, tq), qcol_next),
            pl.BlockSpec((N_KV_HEADS, seq_len, HEAD_DIM), lambda b, j: (0, b, 0)),
            pl.BlockSpec((seq_len, IDX_DIM), lambda b, j: (b, 0)),
            pl.BlockSpec((D_KV, seq_len), lambda b, j: (0, b)),
        ],
        out_specs=pl.BlockSpec((tq, D_ATTN), lambda b, j: (b * nq + j, 0)),
        out_shape=jax.ShapeDtypeStruct((n, D_ATTN), BF16),
        scratch_shapes=[pltpu.VMEM((2, seq_len, tq), F32),
                        pltpu.VMEM((N_HEADS, 1, tq), F32),
                        pltpu.VMEM((N_HEADS, 1, tq), F32),
                        pltpu.VMEM((N_HEADS, HEAD_DIM, tq), F32),
                        pltpu.VMEM((2, N_HEADS, AT_TK, tq), F32),
                        pltpu.VMEM((1, tq), I32)],
        compiler_params=_cparams(("arbitrary", "arbitrary")),
        name="attn",
    )(qt, qit, wit, qit, wit, k, ki, vt)


MIX_TM = 512


def _split_bf16(a):
    hi = a.astype(BF16)
    return hi, (a - hi.astype(F32)).astype(BF16)


def _mix_kernel(x_ref, ys_ref, ya_ref, gs_ref, ga_ref, wps_ref, wpa_ref, wo_ref, g2_ref,
                wr_hi_ref, wr_lo_ref, br_ref, x1_ref, h_ref, idx_ref, gate_ref):
    ps = jnp.dot(ys_ref[...], wps_ref[...], preferred_element_type=F32)
    pa = jnp.dot(ya_ref[...], wpa_ref[...], preferred_element_type=F32)
    mixed = gs_ref[...].astype(F32) * ps + ga_ref[...].astype(F32) * pa
    x1 = x_ref[...] + jnp.dot(mixed.astype(BF16), wo_ref[...], preferred_element_type=F32)
    h = x1 * lax.rsqrt(jnp.mean(x1 * x1, axis=-1, keepdims=True) + RMS_EPS) * g2_ref[...]
    tm = x1.shape[0]
    for j in range(D_MODEL // LANES):
        x1_ref[pl.ds(j, tm, stride=D_MODEL // LANES), :] = x1[:, j * LANES:(j + 1) * LANES]
        h_ref[pl.ds(j, tm, stride=D_MODEL // LANES), :] = h[:, j * LANES:(j + 1) * LANES]
    h_hi, h_lo = _split_bf16(h)
    logits = (jnp.dot(h_hi, wr_hi_ref[...], preferred_element_type=F32)
              + jnp.dot(h_hi, wr_lo_ref[...], preferred_element_type=F32)
              + jnp.dot(h_lo, wr_hi_ref[...], preferred_element_type=F32)) + br_ref[...]
    lane = lax.broadcasted_iota(I32, logits.shape, 1)
    rest = logits
    firsts, vals = [], []
    for _ in range(TOP_K_EXPERTS):
        m = jnp.max(rest, axis=-1, keepdims=True)
        first = jnp.minimum(jnp.min(jnp.where(rest == m, lane, N_EXPERTS), axis=-1, keepdims=True),
                            N_EXPERTS - 1)
        firsts.append(first)
        vals.append(m)
        rest = jnp.where(lane == first, -jnp.inf, rest)
    es = [jnp.exp(v - vals[0]) for v in vals]
    denom = es[0] + es[1] + es[2] + es[3]
    idx_ref[...] = jnp.concatenate(firsts, axis=1)
    gate_ref[...] = jnp.concatenate([e / denom for e in es], axis=1)


def _mix(x2, y_ssm, y_att, gs, ga, w_proj_ssm, w_proj_attn, w_out, norm2_g, w_router, b_router):
    n = x2.shape[0]
    tm = min(MIX_TM, n)
    chunks = D_MODEL // LANES
    row = lambda i: (i, 0)
    wr_hi, wr_lo = _split_bf16(w_router)
    consts = [w_proj_ssm.astype(BF16), w_proj_attn.astype(BF16), w_out.astype(BF16),
              norm2_g.reshape(1, D_MODEL), wr_hi, wr_lo, b_router.reshape(1, N_EXPERTS)]
    return pl.pallas_call(
        _mix_kernel,
        grid=(n // tm,),
        in_specs=[pl.BlockSpec((tm, D_MODEL), row), pl.BlockSpec((tm, D_SSM), row),
                  pl.BlockSpec((tm, D_ATTN), row), pl.BlockSpec((tm, D_MODEL), row),
                  pl.BlockSpec((tm, D_MODEL), row)] + [_const_spec(c.shape) for c in consts],
        out_specs=(pl.BlockSpec((tm * chunks, LANES), row), pl.BlockSpec((tm * chunks, LANES), row),
                   pl.BlockSpec((tm, TOP_K_EXPERTS), row), pl.BlockSpec((tm, TOP_K_EXPERTS), row)),
        out_shape=(jax.ShapeDtypeStruct((n * chunks, LANES), F32), jax.ShapeDtypeStruct((n * chunks, LANES), F32),
                   jax.ShapeDtypeStruct((n, TOP_K_EXPERTS), I32), jax.ShapeDtypeStruct((n, TOP_K_EXPERTS), F32)),
        compiler_params=_cparams(("arbitrary",)),
        name="mix",
    )(x2, y_ssm, y_att, gs, ga, *consts)


MOE_TT = 2048
MOE_RB = 288
ROW_CHUNKS = D_MODEL // LANES
assert ROW_CHUNKS == SUBLANES
MOE_PITCH = MOE_RB + SUBLANES


def _moe_kernel(cnt_ref, off_ref, tok_ref, gate_ref, h_ref, win_ref, bin_ref, wout_ref, bout_ref,
                out_ref, xa_ref, xb_ref, xx_ref, ya_ref, yb_ref, yx_ref):
    i, e = pl.program_id(0), pl.program_id(1)
    trash = out_ref.shape[0] // SUBLANES - 1
    here = i * N_EXPERTS + e
    n, start = cnt_ref[here], off_ref[here]
    start_next = off_ref[i * N_EXPERTS + jnp.minimum(e + 1, N_EXPERTS - 1)]
    prev = i * N_EXPERTS + jnp.maximum(e - 1, 0)
    start_prev = off_ref[prev]
    n_prev = jnp.where(e > 0, jnp.minimum(cnt_ref[prev], MOE_RB), 0)

    def slab(t):
        return pl.ds(pl.multiple_of(t * SUBLANES, SUBLANES), SUBLANES)

    def tile_rows(r):
        return pl.ds(r, SUBLANES, stride=MOE_PITCH)

    def gather_row(base, r, x_ref):
        x_ref[tile_rows(r), :] = h_ref[slab(tok_ref[0, 0, base + r]), :]

    def gather_loop(base, x_ref):
        def body(r8, c):
            for rr in range(SUBLANES):
                gather_row(base, r8 * SUBLANES + rr, x_ref)
            return c

        lax.fori_loop(0, MOE_RB // SUBLANES, body, 0)

    def expert_mlp(x_ref, y_ref):
        xg = jnp.concatenate([x_ref[j * MOE_PITCH:j * MOE_PITCH + MOE_RB, :] for j in range(ROW_CHUNKS)],
                             axis=1).astype(BF16)
        z = jnp.dot(xg, win_ref[0], preferred_element_type=F32) + bin_ref[0]
        half = MXU_COLS // 2
        groups = range(2 * D_EXPERT // MXU_COLS)
        zg = jnp.concatenate([z[:, c * MXU_COLS:c * MXU_COLS + half] for c in groups], axis=1)
        zl = jnp.concatenate([z[:, c * MXU_COLS + half:(c + 1) * MXU_COLS] for c in groups], axis=1)
        zg = jnp.minimum(zg, SWIGLU_LIMIT)
        zl = jnp.clip(zl, -SWIGLU_LIMIT, SWIGLU_LIMIT)
        act = zg * jax.nn.sigmoid(SWIGLU_ALPHA * zg) * (zl + 1.0)
        y = jnp.dot(act.astype(BF16), wout_ref[0], preferred_element_type=F32) + bout_ref[0]
        for j in range(ROW_CHUNKS):
            y_ref[j * MOE_PITCH:j * MOE_PITCH + MOE_RB, :] = y[:, j * LANES:(j + 1) * LANES]

    def updated(base, r, n_valid, y_ref):
        t = jnp.where(r < n_valid, tok_ref[0, 0, base + r], trash)
        return t, out_ref[slab(t), :] + gate_ref[0, 0, base + r] * y_ref[tile_rows(r), :]

    def scatter_group(base, r0, n_valid, y_ref):
        rows = [updated(base, r0 + rr, n_valid, y_ref) for rr in range(SUBLANES)]
        for t, v in rows:
            out_ref[slab(t), :] = v

    def scatter_loop(base, n_valid, y_ref):
        def body(r8, c):
            scatter_group(base, r8 * SUBLANES, n_valid, y_ref)
            return c

        lax.fori_loop(0, (n_valid + SUBLANES - 1) // SUBLANES, body, 0)

    @pl.when(e == 0)
    def _():
        out_ref[...] = jnp.zeros(out_ref.shape, F32)
        gather_loop(start, xa_ref)

    @pl.when(jnp.logical_and(i == 0, e == 0))
    def _():
        yb_ref[...] = jnp.zeros(yb_ref.shape, F32)

    def pipelined(x_cur, x_next, y_cur, y_prev):
        for r in range(MOE_RB):
            gather_row(start_next, r, x_next)
        expert_mlp(x_cur, y_cur)
        for r0 in range(0, MOE_RB, SUBLANES):
            scatter_group(start_prev, r0, n_prev, y_prev)

    @pl.when(e % 2 == 0)
    def _():
        pipelined(xa_ref, xb_ref, ya_ref, yb_ref)

    @pl.when(e % 2 == 1)
    def _():
        pipelined(xb_ref, xa_ref, yb_ref, ya_ref)

    def extra_block(b, c):
        base = start + b * MOE_RB
        gather_loop(base, xx_ref)
        expert_mlp(xx_ref, yx_ref)
        scatter_loop(base, jnp.minimum(MOE_RB, n - b * MOE_RB), yx_ref)
        return c

    lax.fori_loop(1, (n + MOE_RB - 1) // MOE_RB, extra_block, 0)

    @pl.when(e == N_EXPERTS - 1)
    def _():
        scatter_loop(start, jnp.minimum(n, MOE_RB), yb_ref)


FINAL_TM = 512


def _final_kernel(x1_ref, moe_ref, g_ref, o_ref):
    tm = o_ref.shape[0]
    v = jnp.concatenate([x1_ref[pl.ds(j, tm, stride=ROW_CHUNKS), :] + moe_ref[0, pl.ds(j, tm, stride=ROW_CHUNKS), :]
                         for j in range(ROW_CHUNKS)], axis=1)
    o_ref[...] = v * lax.rsqrt(jnp.mean(v * v, axis=-1, keepdims=True) + RMS_EPS) * g_ref[...]


MXU_COLS = 256


def _expert_weights_kernel(win_ref, wout_ref, perm_ref, winp_ref, woutb_ref):
    w = win_ref[0].astype(BF16)
    for c in range(w.shape[1] // MXU_COLS):
        cols = slice(c * MXU_COLS, (c + 1) * MXU_COLS)
        winp_ref[0, :, cols] = jnp.dot(w[:, cols], perm_ref[...], preferred_element_type=F32).astype(BF16)
    woutb_ref[0] = wout_ref[0].astype(BF16)


def _expert_weights(w_moe_in, w_moe_out):
    half = MXU_COLS // 2
    src = jnp.concatenate([jnp.arange(half) * 2, jnp.arange(half) * 2 + 1])
    perm = (jnp.arange(MXU_COLS)[:, None] == src[None, :]).astype(BF16)
    per_expert = lambda a: pl.BlockSpec((1,) + a.shape[1:], lambda e: (e, 0, 0))
    return pl.pallas_call(
        _expert_weights_kernel,
        grid=(N_EXPERTS,),
        in_specs=[per_expert(w_moe_in), per_expert(w_moe_out), _const_spec(perm.shape)],
        out_specs=(per_expert(w_moe_in), per_expert(w_moe_out)),
        out_shape=(jax.ShapeDtypeStruct(w_moe_in.shape, BF16), jax.ShapeDtypeStruct(w_moe_out.shape, BF16)),
        compiler_params=_cparams(("arbitrary",)),
        name="expert_weights",
    )(w_moe_in, w_moe_out, perm)


def _moe(h, x1, idx4, gate4, w_moe_in, b_moe_in, w_moe_out, b_moe_out, norm_f_g):
    n = h.shape[0] // ROW_CHUNKS
    tt = min(MOE_TT, n)
    n_tiles = n // tt
    pairs = TOP_K_EXPERTS * tt
    cap = pairs + MOE_RB

    local = (jnp.arange(n, dtype=I32) % tt)[:, None]
    keys = (idx4 * tt + local).reshape(n_tiles, pairs)
    keys, gate = lax.sort((keys, gate4.reshape(n_tiles, pairs)), dimension=1, num_keys=1)
    bounds = jnp.arange(N_EXPERTS + 1, dtype=I32) * tt
    below = jnp.sum(keys[:, :, None] < bounds[None, None, :], axis=1).astype(I32)
    off, cnt = below[:, :-1], below[:, 1:] - below[:, :-1]
    pad = ((0, 0), (0, cap - pairs))
    tok = jnp.pad(keys % tt, pad).reshape(n_tiles, 1, cap)
    gate = jnp.pad(gate, pad).reshape(n_tiles, 1, cap)

    win, wout = _expert_weights(w_moe_in, w_moe_out)
    half = MXU_COLS // 2
    bin_ = jnp.swapaxes(b_moe_in.reshape(N_EXPERTS, -1, half, 2), 2, 3).reshape(N_EXPERTS, 1, 2 * D_EXPERT)
    bout = b_moe_out.reshape(N_EXPERTS, 1, D_MODEL)

    out_rows = (tt + 1) * ROW_CHUNKS
    tile1 = pl.BlockSpec((tt * ROW_CHUNKS, LANES), lambda i, e, *_: (i, 0), pipeline_mode=pl.Buffered(1))
    smem_list = pl.BlockSpec((1, 1, cap), lambda i, e, *_: (i, 0, 0), memory_space=pltpu.SMEM)
    per_expert = lambda shape: pl.BlockSpec((1,) + shape, lambda i, e, *_: (e, 0, 0))
    row_tile = pltpu.VMEM((ROW_CHUNKS * MOE_PITCH, LANES), F32)
    grid_spec = pltpu.PrefetchScalarGridSpec(
        num_scalar_prefetch=2,
        grid=(n_tiles, N_EXPERTS),
        in_specs=[smem_list, smem_list, tile1,
                  per_expert((D_MODEL, 2 * D_EXPERT)), per_expert((1, 2 * D_EXPERT)),
                  per_expert((D_EXPERT, D_MODEL)), per_expert((1, D_MODEL))],
        out_specs=pl.BlockSpec((out_rows, LANES), lambda i, e, *_: (i, 0)),
        scratch_shapes=[row_tile] * 6,
    )
    moe_out = pl.pallas_call(
        _moe_kernel,
        grid_spec=grid_spec,
        out_shape=jax.ShapeDtypeStruct((n_tiles * out_rows, LANES), F32),
        compiler_params=_cparams(("arbitrary", "arbitrary")),
        name="moe",
    )(cnt.reshape(-1), off.reshape(-1).astype(I32), tok, gate, h, win, bin_, wout, bout)

    tm = min(FINAL_TM, tt)
    per_tile = tt // tm
    return pl.pallas_call(
        _final_kernel,
        grid=(n_tiles, per_tile),
        in_specs=[pl.BlockSpec((tm * ROW_CHUNKS, LANES), lambda i, j: (i * per_tile + j, 0)),
                  pl.BlockSpec((1, tm * ROW_CHUNKS, LANES), lambda i, j: (i, j, 0)),
                  _const_spec((1, D_MODEL))],
        out_specs=pl.BlockSpec((tm, D_MODEL), lambda i, j: (i * per_tile + j, 0)),
        out_shape=jax.ShapeDtypeStruct((n, D_MODEL), F32),
        compiler_params=_cparams(("arbitrary", "arbitrary")),
        name="final_norm",
    )(x1, moe_out.reshape(n_tiles, out_rows, LANES), norm_f_g.reshape(1, D_MODEL))


def kernel(x, norm1_g, w_in, b_gate, ssm_lam_re, ssm_lam_im, ssm_log_dt, ssm_b_re, ssm_b_im, ssm_c_re, ssm_c_im, ssm_d, w_glu, b_glu, w_proj_ssm, w_proj_attn, w_out, norm2_g, w_router, b_router, w_moe_in, b_moe_in, w_moe_out, b_moe_out, norm_f_g):
    bsz, seq_len, _ = x.shape
    n = bsz * seq_len
    x2 = x.reshape(n, D_MODEL)
    u, qt, k, vt, qit, ki, wit, gs, ga = _in_proj(x2, norm1_g[0], w_in[0], b_gate[0], seq_len)
    y_ssm = _s5(u, bsz, seq_len, ssm_lam_re[0], ssm_lam_im[0], ssm_log_dt[0], ssm_b_re[0], ssm_b_im[0],
                ssm_c_re[0], ssm_c_im[0], ssm_d[0], w_glu[0], b_glu[0])
    y_att = _attention(qt, qit, wit, k, ki, vt, bsz, seq_len)
    x1, h, idx4, gate4 = _mix(x2, y_ssm, y_att, gs, ga, w_proj_ssm[0], w_proj_attn[0], w_out[0], norm2_g[0],
                              w_router[0], b_router[0])
    out = _moe(h, x1, idx4, gate4, w_moe_in[0], b_moe_in[0], w_moe_out[0], b_moe_out[0], norm_f_g)
    return out.reshape(x.shape)
```

```python
import functools
import math

import numpy as np
import jax
import jax.numpy as jnp
from jax import lax
from jax.experimental import pallas as pl
from jax.experimental.pallas import tpu as pltpu

D_MODEL = 1024
D_SSM = 512
SSM_GROUP = 16
N_SSM_GROUPS = 32
SSM_STATE = 64
N_HEADS = 8
N_KV_HEADS = 2
HEAD_DIM = 64
D_ATTN = N_HEADS * HEAD_DIM
D_KV = N_KV_HEADS * HEAD_DIM
IDX_HEADS = 16
IDX_DIM = 64
TOPK = 256
ROPE_THETA = 10000.0
N_EXPERTS = 32
TOP_K_EXPERTS = 4
D_EXPERT = D_MODEL
SWIGLU_LIMIT = 7.0
SWIGLU_ALPHA = 1.702
RMS_EPS = 1e-5
NEG = -1e30

LANES = 128
SUBLANES = 8
VMEM_LIMIT = 56 * 1024 * 1024

F32 = jnp.float32
BF16 = jnp.bfloat16
I32 = jnp.int32


def _cparams(sem):
    return pltpu.CompilerParams(dimension_semantics=sem, vmem_limit_bytes=VMEM_LIMIT)


def _const_spec(shape):
    nd = len(shape)
    return pl.BlockSpec(shape, lambda *_: (0,) * nd)


IN_TM = 512


def _in_proj_kernel(x_ref, g_ref, cos_ref, sin_ref, bg_ref,
                    wu_ref, wq_ref, wkv_ref, wqi_ref, wkw_ref, wgs_ref, wga_ref,
                    u_ref, qt_ref, k_ref, vt_ref, qit_ref, ki_ref, wit_ref, gs_ref, ga_ref):
    x = x_ref[...]
    xn = x * lax.rsqrt(jnp.mean(x * x, axis=-1, keepdims=True) + RMS_EPS) * g_ref[...]
    xb = xn.astype(BF16)

    def mm(w_ref):
        return jnp.dot(xb, w_ref[...], preferred_element_type=F32)

    cos = cos_ref[...]
    sin = sin_ref[...]
    lane = lax.broadcasted_iota(I32, cos.shape, 1)
    first_half = lane % HEAD_DIM < HEAD_DIM // 2

    def rope(a):
        cols = []
        for c in range(a.shape[1] // LANES):
            blk = a[:, c * LANES:(c + 1) * LANES]
            partner = jnp.where(first_half, pltpu.roll(blk, LANES - HEAD_DIM // 2, 1),
                                pltpu.roll(blk, HEAD_DIM // 2, 1))
            cols.append(blk * cos + partner * sin)
        return jnp.concatenate(cols, axis=1)

    u_ref[...] = mm(wu_ref)
    q = rope(mm(wq_ref)) * (HEAD_DIM ** -0.5 * math.log2(math.e))
    qt_ref[...] = q.T.astype(BF16)
    kv = mm(wkv_ref)
    k = rope(kv[:, :D_KV])
    for g in range(N_KV_HEADS):
        k_ref[g] = k[:, g * HEAD_DIM:(g + 1) * HEAD_DIM].astype(BF16)
    vt_ref[...] = kv[:, D_KV:].T.astype(BF16)
    kw = mm(wkw_ref)
    ki_ref[...] = rope(kw[:, :LANES])[:, :IDX_DIM].astype(BF16)
    qi = rope(mm(wqi_ref)) * (IDX_DIM ** -0.5)
    qit_ref[...] = qi.T.astype(BF16)
    wit_ref[...] = (kw[:, LANES:2 * LANES].T)[:IDX_HEADS] * (IDX_HEADS ** -0.5)
    gs_ref[...] = jax.nn.sigmoid(mm(wgs_ref) + bg_ref[0:1, :]).astype(BF16)
    ga_ref[...] = jax.nn.sigmoid(mm(wga_ref) + bg_ref[1:2, :]).astype(BF16)


def _in_proj(x2, norm1_g, w_in, b_gate, seq_len):
    n = x2.shape[0]
    tm = min(IN_TM, seq_len)
    o = 0
    parts = []
    for width in (D_SSM, D_ATTN, D_KV, D_KV, IDX_HEADS * IDX_DIM, IDX_DIM, IDX_HEADS, D_MODEL, D_MODEL):
        parts.append(w_in[:, o:o + width])
        o += width
    wu, wq, wk, wv, wqi, wki, wwi, wgs, wga = parts
    wkv = jnp.concatenate([wk, wv], axis=1)
    wkw = jnp.concatenate([wki, jnp.zeros((D_MODEL, LANES - IDX_DIM), F32),
                           wwi, jnp.zeros((D_MODEL, LANES - IDX_HEADS), F32)], axis=1)
    weights = [w.astype(BF16) for w in (wu, wq, wkv, wqi, wkw, wgs, wga)]

    half = HEAD_DIM // 2
    inv = ROPE_THETA ** (-jnp.arange(half, dtype=F32) / half)
    ang = jnp.arange(seq_len, dtype=F32)[:, None] * inv[None, :]
    cos = jnp.tile(jnp.cos(ang), (1, 4))
    sin = jnp.tile(jnp.concatenate([-jnp.sin(ang), jnp.sin(ang)], axis=1), (1, 2))

    tiles_per_seq = seq_len // tm
    row = lambda i: (i, 0)
    col = lambda i: (0, i)
    out_shapes = (
        jax.ShapeDtypeStruct((n, D_SSM), F32),
        jax.ShapeDtypeStruct((D_ATTN, n), BF16),
        jax.ShapeDtypeStruct((N_KV_HEADS, n, HEAD_DIM), BF16),
        jax.ShapeDtypeStruct((D_KV, n), BF16),
        jax.ShapeDtypeStruct((IDX_HEADS * IDX_DIM, n), BF16),
        jax.ShapeDtypeStruct((n, IDX_DIM), BF16),
        jax.ShapeDtypeStruct((IDX_HEADS, n), F32),
        jax.ShapeDtypeStruct((n, D_MODEL), BF16),
        jax.ShapeDtypeStruct((n, D_MODEL), BF16),
    )
    out_specs = (
        pl.BlockSpec((tm, D_SSM), row),
        pl.BlockSpec((D_ATTN, tm), col),
        pl.BlockSpec((N_KV_HEADS, tm, HEAD_DIM), lambda i: (0, i, 0)),
        pl.BlockSpec((D_KV, tm), col),
        pl.BlockSpec((IDX_HEADS * IDX_DIM, tm), col),
        pl.BlockSpec((tm, IDX_DIM), row),
        pl.BlockSpec((IDX_HEADS, tm), col),
        pl.BlockSpec((tm, D_MODEL), row),
        pl.BlockSpec((tm, D_MODEL), row),
    )
    in_specs = [
        pl.BlockSpec((tm, D_MODEL), row),
        _const_spec((1, D_MODEL)),
        pl.BlockSpec((tm, LANES), lambda i: (i % tiles_per_seq, 0)),
        pl.BlockSpec((tm, LANES), lambda i: (i % tiles_per_seq, 0)),
        _const_spec((2, D_MODEL)),
    ] + [_const_spec(w.shape) for w in weights]
    return pl.pallas_call(
        _in_proj_kernel,
        grid=(n // tm,),
        in_specs=in_specs,
        out_specs=out_specs,
        out_shape=out_shapes,
        compiler_params=_cparams(("arbitrary",)),
        name="in_proj",
    )(x2, norm1_g.reshape(1, D_MODEL), cos, sin, b_gate, *weights)


S5_T = 512
S5_CHUNK = 512
N_STATE = N_SSM_GROUPS * SSM_STATE
S5_HALF_IN = D_SSM // 2
S5_HALF_ST = N_STATE // 2


def _cmul(ar, ai, br, bi):
    return ar * br - ai * bi, ar * bi + ai * br


def _s5_kernel(u_ref, lr_ref, li_ref, ldt_ref, bre_ref, bim_ref, cre_ref, cim_ref, d_ref,
               wglu_ref, bglu_ref, y_ref,
               bb_ref, cc_ref, pw_re_ref, pw_im_ref, st_re_ref, st_im_ref, carry_re_ref, carry_im_ref):
    first = jnp.logical_and(pl.program_id(0) == 0, pl.program_id(1) == 0)

    @pl.when(first)
    def _():
        lr, li = lr_ref[...], li_ref[...]
        dt = jnp.exp(ldt_ref[...])
        mag = jnp.exp(lr * dt)
        ar = mag * jnp.cos(li * dt)
        ai = mag * jnp.sin(li * dt)
        den = lr * lr + li * li
        zr = ((ar - 1.0) * lr + ai * li) / den
        zi = (ai * lr - (ar - 1.0) * li) / den
        for h in range(2):
            rows = slice(h * S5_HALF_IN, (h + 1) * S5_HALF_IN)
            cols = slice(h * S5_HALF_ST, (h + 1) * S5_HALF_ST)
            bre, bim = bre_ref[rows, cols], bim_ref[rows, cols]
            bb_ref[h, :, :S5_HALF_ST] = (zr[:, cols] * bre - zi[:, cols] * bim).astype(BF16)
            bb_ref[h, :, S5_HALF_ST:] = (zr[:, cols] * bim + zi[:, cols] * bre).astype(BF16)
            cc_ref[h, :S5_HALF_ST, :] = cre_ref[cols, rows].astype(BF16)
            cc_ref[h, S5_HALF_ST:, :] = (-cim_ref[cols, rows]).astype(BF16)
        pr, pi = ar, ai
        pw_re_ref[0:1, :] = pr
        pw_im_ref[0:1, :] = pi
        for n in range(1, SUBLANES):
            pr, pi = _cmul(pr, pi, ar, ai)
            pw_re_ref[n:n + 1, :] = pr
            pw_im_ref[n:n + 1, :] = pi

    @pl.when(pl.program_id(1) == 0)
    def _():
        carry_re_ref[...] = jnp.zeros_like(carry_re_ref)
        carry_im_ref[...] = jnp.zeros_like(carry_im_ref)

    u = u_ref[...]
    ub = u.astype(BF16)
    for h in range(2):
        bu = jnp.dot(ub[:, h * S5_HALF_IN:(h + 1) * S5_HALF_IN], bb_ref[h], preferred_element_type=F32)
        st_re_ref[:, h * S5_HALF_ST:(h + 1) * S5_HALF_ST] = bu[:, :S5_HALF_ST]
        st_im_ref[:, h * S5_HALF_ST:(h + 1) * S5_HALF_ST] = bu[:, S5_HALF_ST:]

    t_len = u.shape[0]
    row = lax.broadcasted_iota(I32, (SUBLANES, S5_CHUNK), 0)
    for c in range(N_STATE // S5_CHUNK):
        lanes = slice(c * S5_CHUNK, (c + 1) * S5_CHUNK)
        p_re, p_im = pw_re_ref[:, lanes], pw_im_ref[:, lanes]
        steps = []
        for d in (1, 2, 4):
            a_re = jnp.where(row >= d, jnp.broadcast_to(p_re[d - 1:d, :], row.shape), 0.0)
            a_im = jnp.where(row >= d, jnp.broadcast_to(p_im[d - 1:d, :], row.shape), 0.0)
            steps.append((d, a_re, a_im))

        def block(i, carry):
            c_re, c_im = carry
            r0 = pl.multiple_of(i * SUBLANES, SUBLANES)
            xr = st_re_ref[pl.ds(r0, SUBLANES), lanes]
            xi = st_im_ref[pl.ds(r0, SUBLANES), lanes]
            for d, a_re, a_im in steps:
                sr = pltpu.roll(xr, d, 0)
                si = pltpu.roll(xi, d, 0)
                tr, ti = _cmul(a_re, a_im, sr, si)
                xr, xi = xr + tr, xi + ti
            tr, ti = _cmul(p_re, p_im, jnp.broadcast_to(c_re, xr.shape), jnp.broadcast_to(c_im, xi.shape))
            xr, xi = xr + tr, xi + ti
            st_re_ref[pl.ds(r0, SUBLANES), lanes] = xr
            st_im_ref[pl.ds(r0, SUBLANES), lanes] = xi
            return xr[SUBLANES - 1:, :], xi[SUBLANES - 1:, :]

        def two_blocks(j, carry):
            return block(2 * j + 1, block(2 * j, carry))

        c_re, c_im = lax.fori_loop(0, t_len // (2 * SUBLANES), two_blocks,
                                   (carry_re_ref[:, lanes], carry_im_ref[:, lanes]))
        carry_re_ref[:, lanes] = c_re
        carry_im_ref[:, lanes] = c_im

    ys = []
    for h in range(2):
        cols = slice(h * S5_HALF_ST, (h + 1) * S5_HALF_ST)
        xs = jnp.concatenate([st_re_ref[:, cols], st_im_ref[:, cols]], axis=1).astype(BF16)
        ys.append(jnp.dot(xs, cc_ref[h], preferred_element_type=F32))
    y = jnp.concatenate(ys, axis=1) + d_ref[...] * u
    y = jax.nn.gelu(y)
    gate = jnp.dot(y.astype(BF16), wglu_ref[...], preferred_element_type=F32) + bglu_ref[...]
    y_ref[...] = (y * jax.nn.sigmoid(gate)).astype(BF16)


def _s5(u, batch, seq_len, lam_re, lam_im, log_dt, b_re, b_im, c_re, c_im, d_skip, w_glu, b_glu):
    n = u.shape[0]
    t = min(S5_T, seq_len)
    g, p, hh = N_SSM_GROUPS, SSM_STATE, SSM_GROUP
    eye = jnp.eye(g, dtype=F32)
    bre_bd = jnp.einsum("gph,gk->ghkp", b_re, eye).reshape(D_SSM, N_STATE)
    bim_bd = jnp.einsum("gph,gk->ghkp", b_im, eye).reshape(D_SSM, N_STATE)
    cre_bd = jnp.einsum("ghp,gk->gpkh", c_re, eye).reshape(N_STATE, D_SSM)
    cim_bd = jnp.einsum("ghp,gk->gpkh", c_im, eye).reshape(N_STATE, D_SSM)
    lr = lam_re.reshape(1, N_STATE)
    li = lam_im.reshape(1, N_STATE)
    ldt = jnp.repeat(log_dt, p).reshape(1, N_STATE)
    tiles = seq_len // t
    consts = [lr, li, ldt, bre_bd, bim_bd, cre_bd, cim_bd, d_skip.reshape(1, D_SSM),
              w_glu.astype(BF16), b_glu.reshape(1, D_SSM)]
    return pl.pallas_call(
        _s5_kernel,
        grid=(batch, tiles),
        in_specs=[pl.BlockSpec((t, D_SSM), lambda b, i: (b * tiles + i, 0))]
        + [_const_spec(c.shape) for c in consts],
        out_specs=pl.BlockSpec((t, D_SSM), lambda b, i: (b * tiles + i, 0)),
        out_shape=jax.ShapeDtypeStruct((n, D_SSM), BF16),
        scratch_shapes=[
            pltpu.VMEM((2, S5_HALF_IN, 2 * S5_HALF_ST), BF16),
            pltpu.VMEM((2, 2 * S5_HALF_ST, S5_HALF_IN), BF16),
            pltpu.VMEM((SUBLANES, N_STATE), F32),
            pltpu.VMEM((SUBLANES, N_STATE), F32),
            pltpu.VMEM((t, N_STATE), F32),
            pltpu.VMEM((t, N_STATE), F32),
            pltpu.VMEM((1, N_STATE), F32),
            pltpu.VMEM((1, N_STATE), F32),
        ],
        compiler_params=_cparams(("arbitrary", "arbitrary")),
        name="s5",
    )(u, *consts)


AT_TQ = 256
AT_TK = 128
INT_MIN = -2 ** 31


def _attn_kernel(qt_ref, qit_ref, wit_ref, qit_next_ref, wit_next_ref, k_ref, ki_ref, vt_ref, y_ref,
                 score_ref, m_ref, l_ref, acc_ref, lg_ref, cut_ref, *, seq_len, n_sel):
    j = pl.program_id(1)
    last_block = pl.num_programs(1) - 1
    tq, tk = AT_TQ, AT_TK
    n_diag = tq // tk
    n_full = j * n_diag
    n_tot = n_full + n_diag
    q_pos = j * tq + lax.broadcasted_iota(I32, (tk, tq), 1)
    k_iota = lax.broadcasted_iota(I32, (tk, tq), 0)
    scores = score_ref.at[j % 2]
    scores_next = score_ref.at[(j + 1) % 2]

    def tile_start(kt):
        return pl.multiple_of(kt * tk, tk)

    def score_tile(kt, q_ref, w_ref, dst, dst_pos):
        r0 = tile_start(kt)
        ki_t = ki_ref[pl.ds(r0, tk), :]
        acc = jnp.zeros((tk, tq), F32)
        for h in range(IDX_HEADS):
            s = jnp.dot(ki_t, q_ref[h * IDX_DIM:(h + 1) * IDX_DIM, :], preferred_element_type=F32)
            acc = acc + w_ref[h:h + 1, :] * jnp.maximum(s, 0.0)
        dst[pl.ds(r0, tk), :] = jnp.where(r0 + k_iota <= dst_pos, acc, NEG)

    for dd in range(n_diag):
        score_tile(n_full + dd, qit_ref, wit_ref, scores, q_pos)

    n_beyond = seq_len - (j + 1) * tq

    def candidate(u):
        s = u ^ INT_MIN
        return lax.bitcast_convert_type(jnp.where(s < 0, s ^ 0x7FFFFFFF, s), F32)

    n_acc = 4

    def count_tile(c, cnts, cand):
        r0 = pl.multiple_of(c * tq, tq)
        sc = scores[pl.ds(r0, tq), :]
        cnts = list(cnts)
        for i in range(tq // SUBLANES):
            rows = sc[i * SUBLANES:(i + 1) * SUBLANES]
            cnts[i % n_acc] = jnp.where(rows >= cand, cnts[i % n_acc] + 1, cnts[i % n_acc])
        return tuple(cnts)

    zero = jnp.zeros((SUBLANES, tq), I32)

    def bit_step(b, state):
        thr_u, n_ge = state
        cand_u = thr_u | lax.shift_left(jnp.int32(1), 31 - b)
        cand = candidate(cand_u)
        cnts = lax.fori_loop(0, j + 1, functools.partial(count_tile, cand=cand), (zero,) * n_acc)
        cnt = (cnts[0] + cnts[1]) + (cnts[2] + cnts[3])
        total = jnp.sum(cnt, axis=0, keepdims=True) + jnp.where(cand <= NEG, n_beyond, 0)
        keep = total >= n_sel
        return jnp.where(keep, cand_u, thr_u), jnp.where(keep, total, n_ge)

    thr_u, n_ge = lax.fori_loop(0, 32, bit_step,
                                (jnp.zeros((1, tq), I32), jnp.full((1, tq), seq_len, I32)))
    thr = candidate(thr_u)

    sub_iota = lax.broadcasted_iota(I32, (SUBLANES, tq), 0)

    def rows_of(c):
        r0 = pl.multiple_of(c * tq, tq)
        sc = scores[pl.ds(r0, tq), :]
        return [(sc[i * SUBLANES:(i + 1) * SUBLANES], r0 + i * SUBLANES + sub_iota)
                for i in range(tq // SUBLANES)]

    cut_ref[...] = jnp.full((1, tq), seq_len, I32)

    @pl.when(jnp.max(n_ge) > n_sel)
    def _():
        def above_tile(c, gt):
            for rows, _ in rows_of(c):
                gt = jnp.where(rows > thr, gt + 1, gt)
            return gt

        gt = lax.fori_loop(0, j + 1, above_tile, zero)
        n_gt = jnp.sum(gt, axis=0, keepdims=True) + jnp.where(thr < NEG, n_beyond, 0)
        need = n_sel - n_gt
        n_bits = (seq_len - 1).bit_length()

        def eq_tile(c, cnt, cand):
            for rows, pos in rows_of(c):
                cnt = jnp.where(jnp.logical_and(rows == thr, pos < cand), cnt + 1, cnt)
            return cnt

        def index_step(b, c0):
            cand = c0 | lax.shift_left(jnp.int32(1), n_bits - 1 - b)
            cnt = lax.fori_loop(0, j + 1, functools.partial(eq_tile, cand=cand), zero)
            below = jnp.sum(cnt, axis=0, keepdims=True) + jnp.where(
                thr == NEG, jnp.clip(cand - (j + 1) * tq, 0, n_beyond), 0)
            return jnp.where(below < need, cand, c0)

        cut_ref[...] = lax.fori_loop(0, n_bits, index_step, jnp.zeros((1, tq), I32))

    cut = cut_ref[...]

    m_ref[...] = jnp.full(m_ref.shape, NEG, F32)
    l_ref[...] = jnp.zeros(l_ref.shape, F32)
    acc_ref[...] = jnp.zeros(acc_ref.shape, F32)
    rep = N_HEADS // N_KV_HEADS

    q_row = q_pos[0:1, :]
    tie_row = jnp.minimum(cut, q_row)

    def logits(kt, slot):
        r0 = tile_start(kt)
        sc = scores[pl.ds(r0, tk), :]
        last = jnp.where(sc > thr, q_row, jnp.where(sc == thr, tie_row, -1))
        bias = jnp.where(r0 + k_iota <= last, 0.0, NEG)
        for g in range(N_KV_HEADS):
            k_t = k_ref[g, pl.ds(r0, tk), :]
            for h in range(g * rep, (g + 1) * rep):
                q_h = qt_ref[h * HEAD_DIM:(h + 1) * HEAD_DIM, :]
                lg_ref[slot, h] = jnp.dot(k_t, q_h, preferred_element_type=F32) + bias

    ones_rows = jnp.ones((2 * SUBLANES, tk), BF16)

    def softmax_values(kt, slot):
        r0 = tile_start(kt)
        for g in range(N_KV_HEADS):
            v_t = jnp.concatenate([vt_ref[g * HEAD_DIM:(g + 1) * HEAD_DIM, pl.ds(r0, tk)], ones_rows], axis=0)
            for h in range(g * rep, (g + 1) * rep):
                lg = lg_ref[slot, h]
                m_old = m_ref[h]
                m_new = jnp.maximum(m_old, jnp.max(lg, axis=0, keepdims=True))
                p = jnp.exp2(lg - m_new)
                alpha = jnp.exp2(m_old - m_new)
                pv = jnp.dot(v_t, p.astype(BF16), preferred_element_type=F32)
                l_ref[h] = alpha * l_ref[h] + pv[HEAD_DIM:HEAD_DIM + 1, :]
                acc_ref[h] = alpha * acc_ref[h] + pv[:HEAD_DIM, :]
                m_ref[h] = m_new

    logits(0, 0)

    def tile_pair(i, c, with_next):
        logits(2 * i + 1, 1)
        softmax_values(2 * i, 0)
        logits(jnp.minimum(2 * i + 2, n_tot - 1), 0)
        softmax_values(2 * i + 1, 1)
        if with_next:
            for kt in (2 * i, 2 * i + 1):
                score_tile(kt, qit_next_ref, wit_next_ref, scores_next, q_pos + tq)
        return c

    @pl.when(j < last_block)
    def _():
        lax.fori_loop(0, n_tot // 2, functools.partial(tile_pair, with_next=True), 0)

    @pl.when(j == last_block)
    def _():
        lax.fori_loop(0, n_tot // 2, functools.partial(tile_pair, with_next=False), 0)

    outs = [acc_ref[h] / l_ref[h] for h in range(N_HEADS)]
    y_ref[...] = jnp.concatenate(outs, axis=0).T.astype(BF16)


def _attention(qt, qit, wit, k, ki, vt, batch, seq_len):
    n = ki.shape[0]
    tq = AT_TQ
    nq = seq_len // tq
    n_sel = min(TOPK, seq_len // 4)
    qcol = lambda b, j: (0, b * nq + j)
    qcol_next = lambda b, j: (0, b * nq + jnp.minimum(j + 1, nq - 1))
    return pl.pallas_call(
        functools.partial(_attn_kernel, seq_len=seq_len, n_sel=n_sel),
        grid=(batch, nq),
        in_specs=[
            pl.BlockSpec((D_ATTN, tq), qcol),
            pl.BlockSpec((IDX_HEADS * IDX_DIM, tq), qcol),
            pl.BlockSpec((IDX_HEADS, tq), qcol),
            pl.BlockSpec((IDX_HEADS * IDX_DIM, tq), qcol_next),
            pl.BlockSpec((IDX_HEADS, tq), qcol_next),
            pl.BlockSpec((N_KV_HEADS, seq_len, HEAD_DIM), lambda b, j: (0, b, 0)),
            pl.BlockSpec((seq_len, IDX_DIM), lambda b, j: (b, 0)),
            pl.BlockSpec((D_KV, seq_len), lambda b, j: (0, b)),
        ],
        out_specs=pl.BlockSpec((tq, D_ATTN), lambda b, j: (b * nq + j, 0)),
        out_shape=jax.ShapeDtypeStruct((n, D_ATTN), BF16),
        scratch_shapes=[pltpu.VMEM((2, seq_len, tq), F32),
                        pltpu.VMEM((N_HEADS, 1, tq), F32),
                        pltpu.VMEM((N_HEADS, 1, tq), F32),
                        pltpu.VMEM((N_HEADS, HEAD_DIM, tq), F32),
                        pltpu.VMEM((2, N_HEADS, AT_TK, tq), F32),
                        pltpu.VMEM((1, tq), I32)],
        compiler_params=_cparams(("arbitrary", "arbitrary")),
        name="attn",
    )(qt, qit, wit, qit, wit, k, ki, vt)


MIX_TM = 512


def _split_bf16(a):
    hi = a.astype(BF16)
    return hi, (a - hi.astype(F32)).astype(BF16)


def _mix_kernel(x_ref, ys_ref, ya_ref, gs_ref, ga_ref, wps_ref, wpa_ref, wo_ref, g2_ref,
                wr_hi_ref, wr_lo_ref, br_ref, x1_ref, h_ref, idx_ref, gate_ref):
    ps = jnp.dot(ys_ref[...], wps_ref[...], preferred_element_type=F32)
    pa = jnp.dot(ya_ref[...], wpa_ref[...], preferred_element_type=F32)
    mixed = gs_ref[...].astype(F32) * ps + ga_ref[...].astype(F32) * pa
    x1 = x_ref[...] + jnp.dot(mixed.astype(BF16), wo_ref[...], preferred_element_type=F32)
    h = x1 * lax.rsqrt(jnp.mean(x1 * x1, axis=-1, keepdims=True) + RMS_EPS) * g2_ref[...]
    tm = x1.shape[0]
    for j in range(D_MODEL // LANES):
        x1_ref[pl.ds(j, tm, stride=D_MODEL // LANES), :] = x1[:, j * LANES:(j + 1) * LANES]
        h_ref[pl.ds(j, tm, stride=D_MODEL // LANES), :] = h[:, j * LANES:(j + 1) * LANES]
    h_hi, h_lo = _split_bf16(h)
    logits = (jnp.dot(h_hi, wr_hi_ref[...], preferred_element_type=F32)
              + jnp.dot(h_hi, wr_lo_ref[...], preferred_element_type=F32)
              + jnp.dot(h_lo, wr_hi_ref[...], preferred_element_type=F32)) + br_ref[...]
    lane = lax.broadcasted_iota(I32, logits.shape, 1)
    rest = logits
    firsts, vals = [], []
    for _ in range(TOP_K_EXPERTS):
        m = jnp.max(rest, axis=-1, keepdims=True)
        first = jnp.minimum(jnp.min(jnp.where(rest == m, lane, N_EXPERTS), axis=-1, keepdims=True),
                            N_EXPERTS - 1)
        firsts.append(first)
        vals.append(m)
        rest = jnp.where(lane == first, -jnp.inf, rest)
    es = [jnp.exp(v - vals[0]) for v in vals]
    denom = es[0] + es[1] + es[2] + es[3]
    idx_ref[...] = jnp.concatenate(firsts, axis=1)
    gate_ref[...] = jnp.concatenate([e / denom for e in es], axis=1)


def _mix(x2, y_ssm, y_att, gs, ga, w_proj_ssm, w_proj_attn, w_out, norm2_g, w_router, b_router):
    n = x2.shape[0]
    tm = min(MIX_TM, n)
    chunks = D_MODEL // LANES
    row = lambda i: (i, 0)
    wr_hi, wr_lo = _split_bf16(w_router)
    consts = [w_proj_ssm.astype(BF16), w_proj_attn.astype(BF16), w_out.astype(BF16),
              norm2_g.reshape(1, D_MODEL), wr_hi, wr_lo, b_router.reshape(1, N_EXPERTS)]
    return pl.pallas_call(
        _mix_kernel,
        grid=(n // tm,),
        in_specs=[pl.BlockSpec((tm, D_MODEL), row), pl.BlockSpec((tm, D_SSM), row),
                  pl.BlockSpec((tm, D_ATTN), row), pl.BlockSpec((tm, D_MODEL), row),
                  pl.BlockSpec((tm, D_MODEL), row)] + [_const_spec(c.shape) for c in consts],
        out_specs=(pl.BlockSpec((tm * chunks, LANES), row), pl.BlockSpec((tm * chunks, LANES), row),
                   pl.BlockSpec((tm, TOP_K_EXPERTS), row), pl.BlockSpec((tm, TOP_K_EXPERTS), row)),
        out_shape=(jax.ShapeDtypeStruct((n * chunks, LANES), F32), jax.ShapeDtypeStruct((n * chunks, LANES), F32),
                   jax.ShapeDtypeStruct((n, TOP_K_EXPERTS), I32), jax.ShapeDtypeStruct((n, TOP_K_EXPERTS), F32)),
        compiler_params=_cparams(("arbitrary",)),
        name="mix",
    )(x2, y_ssm, y_att, gs, ga, *consts)


MOE_TT = 2048
MOE_RB = 288
ROW_CHUNKS = D_MODEL // LANES
assert ROW_CHUNKS == SUBLANES
MOE_PITCH = MOE_RB + SUBLANES


def _moe_kernel(cnt_ref, off_ref, tok_ref, gate_ref, h_ref, win_ref, bin_ref, wout_ref, bout_ref,
                out_ref, xa_ref, xb_ref, xx_ref, ya_ref, yb_ref, yx_ref):
    i, e = pl.program_id(0), pl.program_id(1)
    trash = out_ref.shape[0] // SUBLANES - 1
    here = i * N_EXPERTS + e
    n, start = cnt_ref[here], off_ref[here]
    start_next = off_ref[i * N_EXPERTS + jnp.minimum(e + 1, N_EXPERTS - 1)]
    prev = i * N_EXPERTS + jnp.maximum(e - 1, 0)
    start_prev = off_ref[prev]
    n_prev = jnp.where(e > 0, jnp.minimum(cnt_ref[prev], MOE_RB), 0)

    def slab(t):
        return pl.ds(pl.multiple_of(t * SUBLANES, SUBLANES), SUBLANES)

    def tile_rows(r):
        return pl.ds(r, SUBLANES, stride=MOE_PITCH)

    def gather_row(base, r, x_ref):
        x_ref[tile_rows(r), :] = h_ref[slab(tok_ref[0, 0, base + r]), :]

    def gather_loop(base, x_ref):
        def body(r8, c):
            for rr in range(SUBLANES):
                gather_row(base, r8 * SUBLANES + rr, x_ref)
            return c

        lax.fori_loop(0, MOE_RB // SUBLANES, body, 0)

    def expert_mlp(x_ref, y_ref):
        xg = jnp.concatenate([x_ref[j * MOE_PITCH:j * MOE_PITCH + MOE_RB, :] for j in range(ROW_CHUNKS)],
                             axis=1).astype(BF16)
        z = jnp.dot(xg, win_ref[0], preferred_element_type=F32) + bin_ref[0]
        half = MXU_COLS // 2
        groups = range(2 * D_EXPERT // MXU_COLS)
        zg = jnp.concatenate([z[:, c * MXU_COLS:c * MXU_COLS + half] for c in groups], axis=1)
        zl = jnp.concatenate([z[:, c * MXU_COLS + half:(c + 1) * MXU_COLS] for c in groups], axis=1)
        zg = jnp.minimum(zg, SWIGLU_LIMIT)
        zl = jnp.clip(zl, -SWIGLU_LIMIT, SWIGLU_LIMIT)
        act = zg * jax.nn.sigmoid(SWIGLU_ALPHA * zg) * (zl + 1.0)
        y = jnp.dot(act.astype(BF16), wout_ref[0], preferred_element_type=F32) + bout_ref[0]
        for j in range(ROW_CHUNKS):
            y_ref[j * MOE_PITCH:j * MOE_PITCH + MOE_RB, :] = y[:, j * LANES:(j + 1) * LANES]

    def updated(base, r, n_valid, y_ref):
        t = jnp.where(r < n_valid, tok_ref[0, 0, base + r], trash)
        return t, out_ref[slab(t), :] + gate_ref[0, 0, base + r] * y_ref[tile_rows(r), :]

    def scatter_group(base, r0, n_valid, y_ref):
        rows = [updated(base, r0 + rr, n_valid, y_ref) for rr in range(SUBLANES)]
        for t, v in rows:
            out_ref[slab(t), :] = v

    def scatter_loop(base, n_valid, y_ref):
        def body(r8, c):
            scatter_group(base, r8 * SUBLANES, n_valid, y_ref)
            return c

        lax.fori_loop(0, (n_valid + SUBLANES - 1) // SUBLANES, body, 0)

    @pl.when(e == 0)
    def _():
        out_ref[...] = jnp.zeros(out_ref.shape, F32)
        gather_loop(start, xa_ref)

    @pl.when(jnp.logical_and(i == 0, e == 0))
    def _():
        yb_ref[...] = jnp.zeros(yb_ref.shape, F32)

    def pipelined(x_cur, x_next, y_cur, y_prev):
        for r in range(MOE_RB):
            gather_row(start_next, r, x_next)
        expert_mlp(x_cur, y_cur)
        for r0 in range(0, MOE_RB, SUBLANES):
            scatter_group(start_prev, r0, n_prev, y_prev)

    @pl.when(e % 2 == 0)
    def _():
        pipelined(xa_ref, xb_ref, ya_ref, yb_ref)

    @pl.when(e % 2 == 1)
    def _():
        pipelined(xb_ref, xa_ref, yb_ref, ya_ref)

    def extra_block(b, c):
        base = start + b * MOE_RB
        gather_loop(base, xx_ref)
        expert_mlp(xx_ref, yx_ref)
        scatter_loop(base, jnp.minimum(MOE_RB, n - b * MOE_RB), yx_ref)
        return c

    lax.fori_loop(1, (n + MOE_RB - 1) // MOE_RB, extra_block, 0)

    @pl.when(e == N_EXPERTS - 1)
    def _():
        scatter_loop(start, jnp.minimum(n, MOE_RB), yb_ref)


FINAL_TM = 512


def _final_kernel(x1_ref, moe_ref, g_ref, o_ref):
    tm = o_ref.shape[0]
    v = jnp.concatenate([x1_ref[pl.ds(j, tm, stride=ROW_CHUNKS), :] + moe_ref[0, pl.ds(j, tm, stride=ROW_CHUNKS), :]
                         for j in range(ROW_CHUNKS)], axis=1)
    o_ref[...] = v * lax.rsqrt(jnp.mean(v * v, axis=-1, keepdims=True) + RMS_EPS) * g_ref[...]


MXU_COLS = 256


def _expert_weights_kernel(win_ref, wout_ref, perm_ref, winp_ref, woutb_ref):
    w = win_ref[0].astype(BF16)
    for c in range(w.shape[1] // MXU_COLS):
        cols = slice(c * MXU_COLS, (c + 1) * MXU_COLS)
        winp_ref[0, :, cols] = jnp.dot(w[:, cols], perm_ref[...], preferred_element_type=F32).astype(BF16)
    woutb_ref[0] = wout_ref[0].astype(BF16)


def _expert_weights(w_moe_in, w_moe_out):
    half = MXU_COLS // 2
    src = jnp.concatenate([jnp.arange(half) * 2, jnp.arange(half) * 2 + 1])
    perm = (jnp.arange(MXU_COLS)[:, None] == src[None, :]).astype(BF16)
    per_expert = lambda a: pl.BlockSpec((1,) + a.shape[1:], lambda e: (e, 0, 0))
    return pl.pallas_call(
        _expert_weights_kernel,
        grid=(N_EXPERTS,),
        in_specs=[per_expert(w_moe_in), per_expert(w_moe_out), _const_spec(perm.shape)],
        out_specs=(per_expert(w_moe_in), per_expert(w_moe_out)),
        out_shape=(jax.ShapeDtypeStruct(w_moe_in.shape, BF16), jax.ShapeDtypeStruct(w_moe_out.shape, BF16)),
        compiler_params=_cparams(("arbitrary",)),
        name="expert_weights",
    )(w_moe_in, w_moe_out, perm)


def _moe(h, x1, idx4, gate4, w_moe_in, b_moe_in, w_moe_out, b_moe_out, norm_f_g):
    n = h.shape[0] // ROW_CHUNKS
    tt = min(MOE_TT, n)
    n_tiles = n // tt
    pairs = TOP_K_EXPERTS * tt
    cap = pairs + MOE_RB

    local = (jnp.arange(n, dtype=I32) % tt)[:, None]
    keys = (idx4 * tt + local).reshape(n_tiles, pairs)
    keys, gate = lax.sort((keys, gate4.reshape(n_tiles, pairs)), dimension=1, num_keys=1)
    bounds = jnp.arange(N_EXPERTS + 1, dtype=I32) * tt
    below = jnp.sum(keys[:, :, None] < bounds[None, None, :], axis=1).astype(I32)
    off, cnt = below[:, :-1], below[:, 1:] - below[:, :-1]
    pad = ((0, 0), (0, cap - pairs))
    tok = jnp.pad(keys % tt, pad).reshape(n_tiles, 1, cap)
    gate = jnp.pad(gate, pad).reshape(n_tiles, 1, cap)

    win, wout = _expert_weights(w_moe_in, w_moe_out)
    half = MXU_COLS // 2
    bin_ = jnp.swapaxes(b_moe_in.reshape(N_EXPERTS, -1, half, 2), 2, 3).reshape(N_EXPERTS, 1, 2 * D_EXPERT)
    bout = b_moe_out.reshape(N_EXPERTS, 1, D_MODEL)

    out_rows = (tt + 1) * ROW_CHUNKS
    tile1 = pl.BlockSpec((tt * ROW_CHUNKS, LANES), lambda i, e, *_: (i, 0), pipeline_mode=pl.Buffered(1))
    smem_list = pl.BlockSpec((1, 1, cap), lambda i, e, *_: (i, 0, 0), memory_space=pltpu.SMEM)
    per_expert = lambda shape: pl.BlockSpec((1,) + shape, lambda i, e, *_: (e, 0, 0))
    row_tile = pltpu.VMEM((ROW_CHUNKS * MOE_PITCH, LANES), F32)
    grid_spec = pltpu.PrefetchScalarGridSpec(
        num_scalar_prefetch=2,
        grid=(n_tiles, N_EXPERTS),
        in_specs=[smem_list, smem_list, tile1,
                  per_expert((D_MODEL, 2 * D_EXPERT)), per_expert((1, 2 * D_EXPERT)),
                  per_expert((D_EXPERT, D_MODEL)), per_expert((1, D_MODEL))],
        out_specs=pl.BlockSpec((out_rows, LANES), lambda i, e, *_: (i, 0)),
        scratch_shapes=[row_tile] * 6,
    )
    moe_out = pl.pallas_call(
        _moe_kernel,
        grid_spec=grid_spec,
        out_shape=jax.ShapeDtypeStruct((n_tiles * out_rows, LANES), F32),
        compiler_params=_cparams(("arbitrary", "arbitrary")),
        name="moe",
    )(cnt.reshape(-1), off.reshape(-1).astype(I32), tok, gate, h, win, bin_, wout, bout)

    tm = min(FINAL_TM, tt)
    per_tile = tt // tm
    return pl.pallas_call(
        _final_kernel,
        grid=(n_tiles, per_tile),
        in_specs=[pl.BlockSpec((tm * ROW_CHUNKS, LANES), lambda i, j: (i * per_tile + j, 0)),
                  pl.BlockSpec((1, tm * ROW_CHUNKS, LANES), lambda i, j: (i, j, 0)),
                  _const_spec((1, D_MODEL))],
        out_specs=pl.BlockSpec((tm, D_MODEL), lambda i, j: (i * per_tile + j, 0)),
        out_shape=jax.ShapeDtypeStruct((n, D_MODEL), F32),
        compiler_params=_cparams(("arbitrary", "arbitrary")),
        name="final_norm",
    )(x1, moe_out.reshape(n_tiles, out_rows, LANES), norm_f_g.reshape(1, D_MODEL))


def kernel(x, norm1_g, w_in, b_gate, ssm_lam_re, ssm_lam_im, ssm_log_dt, ssm_b_re, ssm_b_im, ssm_c_re, ssm_c_im, ssm_d, w_glu, b_glu, w_proj_ssm, w_proj_attn, w_out, norm2_g, w_router, b_router, w_moe_in, b_moe_in, w_moe_out, b_moe_out, norm_f_g):
    bsz, seq_len, _ = x.shape
    n = bsz * seq_len
    x2 = x.reshape(n, D_MODEL)
    u, qt, k, vt, qit, ki, wit, gs, ga = _in_proj(x2, norm1_g[0], w_in[0], b_gate[0], seq_len)
    y_ssm = _s5(u, bsz, seq_len, ssm_lam_re[0], ssm_lam_im[0], ssm_log_dt[0], ssm_b_re[0], ssm_b_im[0],
                ssm_c_re[0], ssm_c_im[0], ssm_d[0], w_glu[0], b_glu[0])
    y_att = _attention(qt, qit, wit, k, ki, vt, bsz, seq_len)
    x1, h, idx4, gate4 = _mix(x2, y_ssm, y_att, gs, ga, w_proj_ssm[0], w_proj_attn[0], w_out[0], norm2_g[0],
                              w_router[0], b_router[0])
    out = _moe(h, x1, idx4, gate4, w_moe_in[0], b_moe_in[0], w_moe_out[0], b_moe_out[0], norm_f_g)
    return out.reshape(x.shape)
```

```python
import functools
import math

import numpy as np
import jax
import jax.numpy as jnp
from jax import lax
from jax.experimental import pallas as pl
from jax.experimental.pallas import tpu as pltpu

D_MODEL = 1024
D_SSM = 512
SSM_GROUP = 16
N_SSM_GROUPS = 32
SSM_STATE = 64
N_HEADS = 8
N_KV_HEADS = 2
HEAD_DIM = 64
D_ATTN = N_HEADS * HEAD_DIM
D_KV = N_KV_HEADS * HEAD_DIM
IDX_HEADS = 16
IDX_DIM = 64
TOPK = 256
ROPE_THETA = 10000.0
N_EXPERTS = 32
TOP_K_EXPERTS = 4
D_EXPERT = D_MODEL
SWIGLU_LIMIT = 7.0
SWIGLU_ALPHA = 1.702
RMS_EPS = 1e-5
NEG = -1e30

LANES = 128
SUBLANES = 8
VMEM_LIMIT = 56 * 1024 * 1024

F32 = jnp.float32
BF16 = jnp.bfloat16
I32 = jnp.int32


def _cparams(sem):
    return pltpu.CompilerParams(dimension_semantics=sem, vmem_limit_bytes=VMEM_LIMIT)


def _const_spec(shape):
    nd = len(shape)
    return pl.BlockSpec(shape, lambda *_: (0,) * nd)


IN_TM = 512


def _in_proj_kernel(x_ref, g_ref, cos_ref, sin_ref, bg_ref,
                    wu_ref, wq_ref, wkv_ref, wqi_ref, wkw_ref, wgs_ref, wga_ref,
                    u_ref, qt_ref, k_ref, vt_ref, qit_ref, ki_ref, wit_ref, gs_ref, ga_ref):
    x = x_ref[...]
    xn = x * lax.rsqrt(jnp.mean(x * x, axis=-1, keepdims=True) + RMS_EPS) * g_ref[...]
    xb = xn.astype(BF16)

    def mm(w_ref):
        return jnp.dot(xb, w_ref[...], preferred_element_type=F32)

    cos = cos_ref[...]
    sin = sin_ref[...]
    lane = lax.broadcasted_iota(I32, cos.shape, 1)
    first_half = lane % HEAD_DIM < HEAD_DIM // 2

    def rope(a):
        cols = []
        for c in range(a.shape[1] // LANES):
            blk = a[:, c * LANES:(c + 1) * LANES]
            partner = jnp.where(first_half, pltpu.roll(blk, LANES - HEAD_DIM // 2, 1),
                                pltpu.roll(blk, HEAD_DIM // 2, 1))
            cols.append(blk * cos + partner * sin)
        return jnp.concatenate(cols, axis=1)

    u_ref[...] = mm(wu_ref)
    q = rope(mm(wq_ref)) * (HEAD_DIM ** -0.5 * math.log2(math.e))
    qt_ref[...] = q.T.astype(BF16)
    kv = mm(wkv_ref)
    k = rope(kv[:, :D_KV])
    for g in range(N_KV_HEADS):
        k_ref[g] = k[:, g * HEAD_DIM:(g + 1) * HEAD_DIM].astype(BF16)
    vt_ref[...] = kv[:, D_KV:].T.astype(BF16)
    kw = mm(wkw_ref)
    ki_ref[...] = rope(kw[:, :LANES])[:, :IDX_DIM].astype(BF16)
    qi = rope(mm(wqi_ref)) * (IDX_DIM ** -0.5)
    qit_ref[...] = qi.T.astype(BF16)
    wit_ref[...] = (kw[:, LANES:2 * LANES].T)[:IDX_HEADS] * (IDX_HEADS ** -0.5)
    gs_ref[...] = jax.nn.sigmoid(mm(wgs_ref) + bg_ref[0:1, :]).astype(BF16)
    ga_ref[...] = jax.nn.sigmoid(mm(wga_ref) + bg_ref[1:2, :]).astype(BF16)


def _in_proj(x2, norm1_g, w_in, b_gate, seq_len):
    n = x2.shape[0]
    tm = min(IN_TM, seq_len)
    o = 0
    parts = []
    for width in (D_SSM, D_ATTN, D_KV, D_KV, IDX_HEADS * IDX_DIM, IDX_DIM, IDX_HEADS, D_MODEL, D_MODEL):
        parts.append(w_in[:, o:o + width])
        o += width
    wu, wq, wk, wv, wqi, wki, wwi, wgs, wga = parts
    wkv = jnp.concatenate([wk, wv], axis=1)
    wkw = jnp.concatenate([wki, jnp.zeros((D_MODEL, LANES - IDX_DIM), F32),
                           wwi, jnp.zeros((D_MODEL, LANES - IDX_HEADS), F32)], axis=1)
    weights = [w.astype(BF16) for w in (wu, wq, wkv, wqi, wkw, wgs, wga)]

    half = HEAD_DIM // 2
    inv = ROPE_THETA ** (-jnp.arange(half, dtype=F32) / half)
    ang = jnp.arange(seq_len, dtype=F32)[:, None] * inv[None, :]
    cos = jnp.tile(jnp.cos(ang), (1, 4))
    sin = jnp.tile(jnp.concatenate([-jnp.sin(ang), jnp.sin(ang)], axis=1), (1, 2))

    tiles_per_seq = seq_len // tm
    row = lambda i: (i, 0)
    col = lambda i: (0, i)
    out_shapes = (
        jax.ShapeDtypeStruct((n, D_SSM), F32),
        jax.ShapeDtypeStruct((D_ATTN, n), BF16),
        jax.ShapeDtypeStruct((N_KV_HEADS, n, HEAD_DIM), BF16),
        jax.ShapeDtypeStruct((D_KV, n), BF16),
        jax.ShapeDtypeStruct((IDX_HEADS * IDX_DIM, n), BF16),
        jax.ShapeDtypeStruct((n, IDX_DIM), BF16),
        jax.ShapeDtypeStruct((IDX_HEADS, n), F32),
        jax.ShapeDtypeStruct((n, D_MODEL), BF16),
        jax.ShapeDtypeStruct((n, D_MODEL), BF16),
    )
    out_specs = (
        pl.BlockSpec((tm, D_SSM), row),
        pl.BlockSpec((D_ATTN, tm), col),
        pl.BlockSpec((N_KV_HEADS, tm, HEAD_DIM), lambda i: (0, i, 0)),
        pl.BlockSpec((D_KV, tm), col),
        pl.BlockSpec((IDX_HEADS * IDX_DIM, tm), col),
        pl.BlockSpec((tm, IDX_DIM), row),
        pl.BlockSpec((IDX_HEADS, tm), col),
        pl.BlockSpec((tm, D_MODEL), row),
        pl.BlockSpec((tm, D_MODEL), row),
    )
    in_specs = [
        pl.BlockSpec((tm, D_MODEL), row),
        _const_spec((1, D_MODEL)),
        pl.BlockSpec((tm, LANES), lambda i: (i % tiles_per_seq, 0)),
        pl.BlockSpec((tm, LANES), lambda i: (i % tiles_per_seq, 0)),
        _const_spec((2, D_MODEL)),
    ] + [_const_spec(w.shape) for w in weights]
    return pl.pallas_call(
        _in_proj_kernel,
        grid=(n // tm,),
        in_specs=in_specs,
        out_specs=out_specs,
        out_shape=out_shapes,
        compiler_params=_cparams(("arbitrary",)),
        name="in_proj",
    )(x2, norm1_g.reshape(1, D_MODEL), cos, sin, b_gate, *weights)


S5_T = 512
S5_CHUNK = 512
N_STATE = N_SSM_GROUPS * SSM_STATE
S5_HALF_IN = D_SSM // 2
S5_HALF_ST = N_STATE // 2


def _cmul(ar, ai, br, bi):
    return ar * br - ai * bi, ar * bi + ai * br


def _s5_kernel(u_ref, lr_ref, li_ref, ldt_ref, bre_ref, bim_ref, cre_ref, cim_ref, d_ref,
               wglu_ref, bglu_ref, y_ref,
               bb_ref, cc_ref, pw_re_ref, pw_im_ref, st_re_ref, st_im_ref, carry_re_ref, carry_im_ref):
    first = jnp.logical_and(pl.program_id(0) == 0, pl.program_id(1) == 0)

    @pl.when(first)
    def _():
        lr, li = lr_ref[...], li_ref[...]
        dt = jnp.exp(ldt_ref[...])
        mag = jnp.exp(lr * dt)
        ar = mag * jnp.cos(li * dt)
        ai = mag * jnp.sin(li * dt)
        den = lr * lr + li * li
        zr = ((ar - 1.0) * lr + ai * li) / den
        zi = (ai * lr - (ar - 1.0) * li) / den
        for h in range(2):
            rows = slice(h * S5_HALF_IN, (h + 1) * S5_HALF_IN)
            cols = slice(h * S5_HALF_ST, (h + 1) * S5_HALF_ST)
            bre, bim = bre_ref[rows, cols], bim_ref[rows, cols]
            bb_ref[h, :, :S5_HALF_ST] = (zr[:, cols] * bre - zi[:, cols] * bim).astype(BF16)
            bb_ref[h, :, S5_HALF_ST:] = (zr[:, cols] * bim + zi[:, cols] * bre).astype(BF16)
            cc_ref[h, :S5_HALF_ST, :] = cre_ref[cols, rows].astype(BF16)
            cc_ref[h, S5_HALF_ST:, :] = (-cim_ref[cols, rows]).astype(BF16)
        pr, pi = ar, ai
        pw_re_ref[0:1, :] = pr
        pw_im_ref[0:1, :] = pi
        for n in range(1, SUBLANES):
            pr, pi = _cmul(pr, pi, ar, ai)
            pw_re_ref[n:n + 1, :] = pr
            pw_im_ref[n:n + 1, :] = pi

    @pl.when(pl.program_id(1) == 0)
    def _():
        carry_re_ref[...] = jnp.zeros_like(carry_re_ref)
        carry_im_ref[...] = jnp.zeros_like(carry_im_ref)

    u = u_ref[...]
    ub = u.astype(BF16)
    for h in range(2):
        bu = jnp.dot(ub[:, h * S5_HALF_IN:(h + 1) * S5_HALF_IN], bb_ref[h], preferred_element_type=F32)
        st_re_ref[:, h * S5_HALF_ST:(h + 1) * S5_HALF_ST] = bu[:, :S5_HALF_ST]
        st_im_ref[:, h * S5_HALF_ST:(h + 1) * S5_HALF_ST] = bu[:, S5_HALF_ST:]

    t_len = u.shape[0]
    row = lax.broadcasted_iota(I32, (SUBLANES, S5_CHUNK), 0)
    for c in range(N_STATE // S5_CHUNK):
        lanes = slice(c * S5_CHUNK, (c + 1) * S5_CHUNK)
        p_re, p_im = pw_re_ref[:, lanes], pw_im_ref[:, lanes]
        steps = []
        for d in (1, 2, 4):
            a_re = jnp.where(row >= d, jnp.broadcast_to(p_re[d - 1:d, :], row.shape), 0.0)
            a_im = jnp.where(row >= d, jnp.broadcast_to(p_im[d - 1:d, :], row.shape), 0.0)
            steps.append((d, a_re, a_im))

        def block(i, carry):
            c_re, c_im = carry
            r0 = pl.multiple_of(i * SUBLANES, SUBLANES)
            xr = st_re_ref[pl.ds(r0, SUBLANES), lanes]
            xi = st_im_ref[pl.ds(r0, SUBLANES), lanes]
            for d, a_re, a_im in steps:
                sr = pltpu.roll(xr, d, 0)
                si = pltpu.roll(xi, d, 0)
                tr, ti = _cmul(a_re, a_im, sr, si)
                xr, xi = xr + tr, xi + ti
            tr, ti = _cmul(p_re, p_im, jnp.broadcast_to(c_re, xr.shape), jnp.broadcast_to(c_im, xi.shape))
            xr, xi = xr + tr, xi + ti
            st_re_ref[pl.ds(r0, SUBLANES), lanes] = xr
            st_im_ref[pl.ds(r0, SUBLANES), lanes] = xi
            return xr[SUBLANES - 1:, :], xi[SUBLANES - 1:, :]

        def two_blocks(j, carry):
            return block(2 * j + 1, block(2 * j, carry))

        c_re, c_im = lax.fori_loop(0, t_len // (2 * SUBLANES), two_blocks,
                                   (carry_re_ref[:, lanes], carry_im_ref[:, lanes]))
        carry_re_ref[:, lanes] = c_re
        carry_im_ref[:, lanes] = c_im

    ys = []
    for h in range(2):
        cols = slice(h * S5_HALF_ST, (h + 1) * S5_HALF_ST)
        xs = jnp.concatenate([st_re_ref[:, cols], st_im_ref[:, cols]], axis=1).astype(BF16)
        ys.append(jnp.dot(xs, cc_ref[h], preferred_element_type=F32))
    y = jnp.concatenate(ys, axis=1) + d_ref[...] * u
    y = jax.nn.gelu(y)
    gate = jnp.dot(y.astype(BF16), wglu_ref[...], preferred_element_type=F32) + bglu_ref[...]
    y_ref[...] = (y * jax.nn.sigmoid(gate)).astype(BF16)


def _s5(u, batch, seq_len, lam_re, lam_im, log_dt, b_re, b_im, c_re, c_im, d_skip, w_glu, b_glu):
    n = u.shape[0]
    t = min(S5_T, seq_len)
    g, p, hh = N_SSM_GROUPS, SSM_STATE, SSM_GROUP
    same_group = (jnp.arange(D_SSM)[:, None] // hh) == (jnp.arange(N_STATE)[None, :] // p)

    def b_blockdiag(b):
        rows = jnp.transpose(b, (0, 2, 1)).reshape(D_SSM, p)
        return jnp.where(same_group, jnp.tile(rows, (1, g)), 0.0)

    def c_blockdiag(c):
        rows = jnp.transpose(c, (0, 2, 1)).reshape(N_STATE, hh)
        return jnp.where(same_group.T, jnp.tile(rows, (1, g)), 0.0)

    bre_bd, bim_bd = b_blockdiag(b_re), b_blockdiag(b_im)
    cre_bd, cim_bd = c_blockdiag(c_re), c_blockdiag(c_im)
    lr = lam_re.reshape(1, N_STATE)
    li = lam_im.reshape(1, N_STATE)
    ldt = jnp.repeat(log_dt, p).reshape(1, N_STATE)
    tiles = seq_len // t
    consts = [lr, li, ldt, bre_bd, bim_bd, cre_bd, cim_bd, d_skip.reshape(1, D_SSM),
              w_glu.astype(BF16), b_glu.reshape(1, D_SSM)]
    return pl.pallas_call(
        _s5_kernel,
        grid=(batch, tiles),
        in_specs=[pl.BlockSpec((t, D_SSM), lambda b, i: (b * tiles + i, 0))]
        + [_const_spec(c.shape) for c in consts],
        out_specs=pl.BlockSpec((t, D_SSM), lambda b, i: (b * tiles + i, 0)),
        out_shape=jax.ShapeDtypeStruct((n, D_SSM), BF16),
        scratch_shapes=[
            pltpu.VMEM((2, S5_HALF_IN, 2 * S5_HALF_ST), BF16),
            pltpu.VMEM((2, 2 * S5_HALF_ST, S5_HALF_IN), BF16),
            pltpu.VMEM((SUBLANES, N_STATE), F32),
            pltpu.VMEM((SUBLANES, N_STATE), F32),
            pltpu.VMEM((t, N_STATE), F32),
            pltpu.VMEM((t, N_STATE), F32),
            pltpu.VMEM((1, N_STATE), F32),
            pltpu.VMEM((1, N_STATE), F32),
        ],
        compiler_params=_cparams(("arbitrary", "arbitrary")),
        name="s5",
    )(u, *consts)


AT_TQ = 256
AT_TK = 128
INT_MIN = -2 ** 31


def _attn_kernel(qt_ref, qit_ref, wit_ref, qit_next_ref, wit_next_ref, k_ref, ki_ref, vt_ref, y_ref,
                 score_ref, m_ref, l_ref, acc_ref, lg_ref, cut_ref, *, seq_len, n_sel):
    j = pl.program_id(1)
    last_block = pl.num_programs(1) - 1
    tq, tk = AT_TQ, AT_TK
    n_diag = tq // tk
    n_full = j * n_diag
    n_tot = n_full + n_diag
    q_pos = j * tq + lax.broadcasted_iota(I32, (tk, tq), 1)
    k_iota = lax.broadcasted_iota(I32, (tk, tq), 0)
    scores = score_ref.at[j % 2]
    scores_next = score_ref.at[(j + 1) % 2]

    def tile_start(kt):
        return pl.multiple_of(kt * tk, tk)

    def score_tile(kt, q_ref, w_ref, dst, dst_pos):
        r0 = tile_start(kt)
        ki_t = ki_ref[pl.ds(r0, tk), :]
        acc = jnp.zeros((tk, tq), F32)
        for h in range(IDX_HEADS):
            s = jnp.dot(ki_t, q_ref[h * IDX_DIM:(h + 1) * IDX_DIM, :], preferred_element_type=F32)
            acc = acc + w_ref[h:h + 1, :] * jnp.maximum(s, 0.0)
        dst[pl.ds(r0, tk), :] = jnp.where(r0 + k_iota <= dst_pos, acc, NEG)

    for dd in range(n_diag):
        score_tile(n_full + dd, qit_ref, wit_ref, scores, q_pos)

    n_beyond = seq_len - (j + 1) * tq

    def candidate(u):
        s = u ^ INT_MIN
        return lax.bitcast_convert_type(jnp.where(s < 0, s ^ 0x7FFFFFFF, s), F32)

    n_acc = 4

    def count_tile(c, cnts, cand):
        r0 = pl.multiple_of(c * tq, tq)
        sc = scores[pl.ds(r0, tq), :]
        cnts = list(cnts)
        for i in range(tq // SUBLANES):
            rows = sc[i * SUBLANES:(i + 1) * SUBLANES]
            cnts[i % n_acc] = jnp.where(rows >= cand, cnts[i % n_acc] + 1, cnts[i % n_acc])
        return tuple(cnts)

    zero = jnp.zeros((SUBLANES, tq), I32)

    def bit_step(b, state):
        thr_u, n_ge = state
        cand_u = thr_u | lax.shift_left(jnp.int32(1), 31 - b)
        cand = candidate(cand_u)
        cnts = lax.fori_loop(0, j + 1, functools.partial(count_tile, cand=cand), (zero,) * n_acc)
        cnt = (cnts[0] + cnts[1]) + (cnts[2] + cnts[3])
        total = jnp.sum(cnt, axis=0, keepdims=True) + jnp.where(cand <= NEG, n_beyond, 0)
        keep = total >= n_sel
        return jnp.where(keep, cand_u, thr_u), jnp.where(keep, total, n_ge)

    thr_u, n_ge = lax.fori_loop(0, 32, bit_step,
                                (jnp.zeros((1, tq), I32), jnp.full((1, tq), seq_len, I32)))
    thr = candidate(thr_u)

    sub_iota = lax.broadcasted_iota(I32, (SUBLANES, tq), 0)

    def rows_of(c):
        r0 = pl.multiple_of(c * tq, tq)
        sc = scores[pl.ds(r0, tq), :]
        return [(sc[i * SUBLANES:(i + 1) * SUBLANES], r0 + i * SUBLANES + sub_iota)
                for i in range(tq // SUBLANES)]

    cut_ref[...] = jnp.full((1, tq), seq_len, I32)

    @pl.when(jnp.max(n_ge) > n_sel)
    def _():
        def above_tile(c, gt):
            for rows, _ in rows_of(c):
                gt = jnp.where(rows > thr, gt + 1, gt)
            return gt

        gt = lax.fori_loop(0, j + 1, above_tile, zero)
        n_gt = jnp.sum(gt, axis=0, keepdims=True) + jnp.where(thr < NEG, n_beyond, 0)
        need = n_sel - n_gt
        n_bits = (seq_len - 1).bit_length()

        def eq_tile(c, cnt, cand):
            for rows, pos in rows_of(c):
                cnt = jnp.where(jnp.logical_and(rows == thr, pos < cand), cnt + 1, cnt)
            return cnt

        def index_step(b, c0):
            cand = c0 | lax.shift_left(jnp.int32(1), n_bits - 1 - b)
            cnt = lax.fori_loop(0, j + 1, functools.partial(eq_tile, cand=cand), zero)
            below = jnp.sum(cnt, axis=0, keepdims=True) + jnp.where(
                thr == NEG, jnp.clip(cand - (j + 1) * tq, 0, n_beyond), 0)
            return jnp.where(below < need, cand, c0)

        cut_ref[...] = lax.fori_loop(0, n_bits, index_step, jnp.zeros((1, tq), I32))

    cut = cut_ref[...]

    m_ref[...] = jnp.full(m_ref.shape, NEG, F32)
    l_ref[...] = jnp.zeros(l_ref.shape, F32)
    acc_ref[...] = jnp.zeros(acc_ref.shape, F32)
    rep = N_HEADS // N_KV_HEADS

    q_row = q_pos[0:1, :]
    tie_row = jnp.minimum(cut, q_row)

    def logits(kt, slot):
        r0 = tile_start(kt)
        sc = scores[pl.ds(r0, tk), :]
        last = jnp.where(sc > thr, q_row, jnp.where(sc == thr, tie_row, -1))
        bias = jnp.where(r0 + k_iota <= last, 0.0, NEG)
        for g in range(N_KV_HEADS):
            k_t = k_ref[g, pl.ds(r0, tk), :]
            for h in range(g * rep, (g + 1) * rep):
                q_h = qt_ref[h * HEAD_DIM:(h + 1) * HEAD_DIM, :]
                lg_ref[slot, h] = jnp.dot(k_t, q_h, preferred_element_type=F32) + bias

    ones_rows = jnp.ones((2 * SUBLANES, tk), BF16)

    def softmax_values(kt, slot):
        r0 = tile_start(kt)
        for g in range(N_KV_HEADS):
            v_t = jnp.concatenate([vt_ref[g * HEAD_DIM:(g + 1) * HEAD_DIM, pl.ds(r0, tk)], ones_rows], axis=0)
            for h in range(g * rep, (g + 1) * rep):
                lg = lg_ref[slot, h]
                m_old = m_ref[h]
                m_new = jnp.maximum(m_old, jnp.max(lg, axis=0, keepdims=True))
                p = jnp.exp2(lg - m_new)
                alpha = jnp.exp2(m_old - m_new)
                pv = jnp.dot(v_t, p.astype(BF16), preferred_element_type=F32)
                l_ref[h] = alpha * l_ref[h] + pv[HEAD_DIM:HEAD_DIM + 1, :]
                acc_ref[h] = alpha * acc_ref[h] + pv[:HEAD_DIM, :]
                m_ref[h] = m_new

    logits(0, 0)

    def tile_pair(i, c, with_next):
        logits(2 * i + 1, 1)
        softmax_values(2 * i, 0)
        logits(jnp.minimum(2 * i + 2, n_tot - 1), 0)
        softmax_values(2 * i + 1, 1)
        if with_next:
            for kt in (2 * i, 2 * i + 1):
                score_tile(kt, qit_next_ref, wit_next_ref, scores_next, q_pos + tq)
        return c

    @pl.when(j < last_block)
    def _():
        lax.fori_loop(0, n_tot // 2, functools.partial(tile_pair, with_next=True), 0)

    @pl.when(j == last_block)
    def _():
        lax.fori_loop(0, n_tot // 2, functools.partial(tile_pair, with_next=False), 0)

    outs = [acc_ref[h] / l_ref[h] for h in range(N_HEADS)]
    y_ref[...] = jnp.concatenate(outs, axis=0).T.astype(BF16)


def _attention(qt, qit, wit, k, ki, vt, batch, seq_len):
    n = ki.shape[0]
    tq = AT_TQ
    nq = seq_len // tq
    n_sel = min(TOPK, seq_len // 4)
    qcol = lambda b, j: (0, b * nq + j)
    qcol_next = lambda b, j: (0, b * nq + jnp.minimum(j + 1, nq - 1))
    return pl.pallas_call(
        functools.partial(_attn_kernel, seq_len=seq_len, n_sel=n_sel),
        grid=(batch, nq),
        in_specs=[
            pl.BlockSpec((D_ATTN, tq), qcol),
            pl.BlockSpec((IDX_HEADS * IDX_DIM, tq), qcol),
            pl.BlockSpec((IDX_HEADS, tq), qcol),
            pl.BlockSpec((IDX_HEADS * IDX_DIM, tq), qcol_next),
            pl.BlockSpec((IDX_HEADS, tq), qcol_next),
            pl.BlockSpec((N_KV_HEADS, seq_len, HEAD_DIM), lambda b, j: (0, b, 0)),
            pl.BlockSpec((seq_len, IDX_DIM), lambda b, j: (b, 0)),
            pl.BlockSpec((D_KV, seq_len), lambda b, j: (0, b)),
        ],
        out_specs=pl.BlockSpec((tq, D_ATTN), lambda b, j: (b * nq + j, 0)),
        out_shape=jax.ShapeDtypeStruct((n, D_ATTN), BF16),
        scratch_shapes=[pltpu.VMEM((2, seq_len, tq), F32),
                        pltpu.VMEM((N_HEADS, 1, tq), F32),
                        pltpu.VMEM((N_HEADS, 1, tq), F32),
                        pltpu.VMEM((N_HEADS, HEAD_DIM, tq), F32),
                        pltpu.VMEM((2, N_HEADS, AT_TK, tq), F32),
                        pltpu.VMEM((1, tq), I32)],
        compiler_params=_cparams(("arbitrary", "arbitrary")),
        name="attn",
    )(qt, qit, wit, qit, wit, k, ki, vt)


MIX_TM = 512


def _split_bf16(a):
    hi = a.astype(BF16)
    return hi, (a - hi.astype(F32)).astype(BF16)


def _mix_kernel(x_ref, ys_ref, ya_ref, gs_ref, ga_ref, wps_ref, wpa_ref, wo_ref, g2_ref,
                wr_hi_ref, wr_lo_ref, br_ref, x1_ref, h_ref, idx_ref, gate_ref):
    ps = jnp.dot(ys_ref[...], wps_ref[...], preferred_element_type=F32)
    pa = jnp.dot(ya_ref[...], wpa_ref[...], preferred_element_type=F32)
    mixed = gs_ref[...].astype(F32) * ps + ga_ref[...].astype(F32) * pa
    x1 = x_ref[...] + jnp.dot(mixed.astype(BF16), wo_ref[...], preferred_element_type=F32)
    h = x1 * lax.rsqrt(jnp.mean(x1 * x1, axis=-1, keepdims=True) + RMS_EPS) * g2_ref[...]
    tm = x1.shape[0]
    for j in range(D_MODEL // LANES):
        x1_ref[pl.ds(j, tm, stride=D_MODEL // LANES), :] = x1[:, j * LANES:(j + 1) * LANES]
        h_ref[pl.ds(j, tm, stride=D_MODEL // LANES), :] = h[:, j * LANES:(j + 1) * LANES]
    h_hi, h_lo = _split_bf16(h)
    logits = (jnp.dot(h_hi, wr_hi_ref[...], preferred_element_type=F32)
              + jnp.dot(h_hi, wr_lo_ref[...], preferred_element_type=F32)
              + jnp.dot(h_lo, wr_hi_ref[...], preferred_element_type=F32)) + br_ref[...]
    lane = lax.broadcasted_iota(I32, logits.shape, 1)
    rest = logits
    firsts, vals = [], []
    for _ in range(TOP_K_EXPERTS):
        m = jnp.max(rest, axis=-1, keepdims=True)
        first = jnp.minimum(jnp.min(jnp.where(rest == m, lane, N_EXPERTS), axis=-1, keepdims=True),
                            N_EXPERTS - 1)
        firsts.append(first)
        vals.append(m)
        rest = jnp.where(lane == first, -jnp.inf, rest)
    es = [jnp.exp(v - vals[0]) for v in vals]
    denom = es[0] + es[1] + es[2] + es[3]
    wide = lax.broadcasted_iota(I32, (logits.shape[0], LANES), 1)
    idx_wide = jnp.zeros(wide.shape, I32)
    gate_wide = jnp.zeros(wide.shape, F32)
    for pick, (first, e) in enumerate(zip(firsts, es)):
        idx_wide = jnp.where(wide == pick, first, idx_wide)
        gate_wide = jnp.where(wide == pick, e / denom, gate_wide)
    idx_ref[...] = idx_wide.T[:SUBLANES]
    gate_ref[...] = gate_wide.T[:SUBLANES]


def _mix(x2, y_ssm, y_att, gs, ga, w_proj_ssm, w_proj_attn, w_out, norm2_g, w_router, b_router):
    n = x2.shape[0]
    tm = min(MIX_TM, n)
    chunks = D_MODEL // LANES
    row = lambda i: (i, 0)
    wr_hi, wr_lo = _split_bf16(w_router)
    consts = [w_proj_ssm.astype(BF16), w_proj_attn.astype(BF16), w_out.astype(BF16),
              norm2_g.reshape(1, D_MODEL), wr_hi, wr_lo, b_router.reshape(1, N_EXPERTS)]
    return pl.pallas_call(
        _mix_kernel,
        grid=(n // tm,),
        in_specs=[pl.BlockSpec((tm, D_MODEL), row), pl.BlockSpec((tm, D_SSM), row),
                  pl.BlockSpec((tm, D_ATTN), row), pl.BlockSpec((tm, D_MODEL), row),
                  pl.BlockSpec((tm, D_MODEL), row)] + [_const_spec(c.shape) for c in consts],
        out_specs=(pl.BlockSpec((tm * chunks, LANES), row), pl.BlockSpec((tm * chunks, LANES), row),
                   pl.BlockSpec((SUBLANES, tm), lambda i: (0, i)), pl.BlockSpec((SUBLANES, tm), lambda i: (0, i))),
        out_shape=(jax.ShapeDtypeStruct((n * chunks, LANES), F32), jax.ShapeDtypeStruct((n * chunks, LANES), F32),
                   jax.ShapeDtypeStruct((SUBLANES, n), I32), jax.ShapeDtypeStruct((SUBLANES, n), F32)),
        compiler_params=_cparams(("arbitrary",)),
        name="mix",
    )(x2, y_ssm, y_att, gs, ga, *consts)


MOE_TT = 2048
MOE_RB = 288
ROW_CHUNKS = D_MODEL // LANES
assert ROW_CHUNKS == SUBLANES
MOE_PITCH = MOE_RB + SUBLANES


def _moe_kernel(cnt_ref, off_ref, tok_ref, gate_ref, h_ref, win_ref, bin_ref, wout_ref, bout_ref,
                out_ref, xa_ref, xb_ref, xx_ref, ya_ref, yb_ref, yx_ref):
    i, e = pl.program_id(0), pl.program_id(1)
    trash = out_ref.shape[0] // SUBLANES - 1
    here = i * N_EXPERTS + e
    n, start = cnt_ref[here], off_ref[here]
    start_next = off_ref[i * N_EXPERTS + jnp.minimum(e + 1, N_EXPERTS - 1)]
    prev = i * N_EXPERTS + jnp.maximum(e - 1, 0)
    start_prev = off_ref[prev]
    n_prev = jnp.where(e > 0, jnp.minimum(cnt_ref[prev], MOE_RB), 0)

    def slab(t):
        return pl.ds(pl.multiple_of(t * SUBLANES, SUBLANES), SUBLANES)

    def tile_rows(r):
        return pl.ds(r, SUBLANES, stride=MOE_PITCH)

    def gather_row(base, r, x_ref):
        x_ref[tile_rows(r), :] = h_ref[slab(tok_ref[0, 0, base + r]), :]

    def gather_loop(base, x_ref):
        def body(r8, c):
            for rr in range(SUBLANES):
                gather_row(base, r8 * SUBLANES + rr, x_ref)
            return c

        lax.fori_loop(0, MOE_RB // SUBLANES, body, 0)

    def expert_mlp(x_ref, y_ref):
        xg = jnp.concatenate([x_ref[j * MOE_PITCH:j * MOE_PITCH + MOE_RB, :] for j in range(ROW_CHUNKS)],
                             axis=1).astype(BF16)
        z = jnp.dot(xg, win_ref[0], preferred_element_type=F32) + bin_ref[0]
        half = MXU_COLS // 2
        groups = range(2 * D_EXPERT // MXU_COLS)
        zg = jnp.concatenate([z[:, c * MXU_COLS:c * MXU_COLS + half] for c in groups], axis=1)
        zl = jnp.concatenate([z[:, c * MXU_COLS + half:(c + 1) * MXU_COLS] for c in groups], axis=1)
        zg = jnp.minimum(zg, SWIGLU_LIMIT)
        zl = jnp.clip(zl, -SWIGLU_LIMIT, SWIGLU_LIMIT)
        act = zg * jax.nn.sigmoid(SWIGLU_ALPHA * zg) * (zl + 1.0)
        y = jnp.dot(act.astype(BF16), wout_ref[0], preferred_element_type=F32) + bout_ref[0]
        for j in range(ROW_CHUNKS):
            y_ref[j * MOE_PITCH:j * MOE_PITCH + MOE_RB, :] = y[:, j * LANES:(j + 1) * LANES]

    def updated(base, r, n_valid, y_ref):
        t = jnp.where(r < n_valid, tok_ref[0, 0, base + r], trash)
        return t, out_ref[slab(t), :] + gate_ref[0, 0, base + r] * y_ref[tile_rows(r), :]

    def scatter_group(base, r0, n_valid, y_ref):
        rows = [updated(base, r0 + rr, n_valid, y_ref) for rr in range(SUBLANES)]
        for t, v in rows:
            out_ref[slab(t), :] = v

    def scatter_loop(base, n_valid, y_ref):
        def body(r8, c):
            scatter_group(base, r8 * SUBLANES, n_valid, y_ref)
            return c

        lax.fori_loop(0, (n_valid + SUBLANES - 1) // SUBLANES, body, 0)

    @pl.when(e == 0)
    def _():
        out_ref[...] = jnp.zeros(out_ref.shape, F32)
        gather_loop(start, xa_ref)

    @pl.when(jnp.logical_and(i == 0, e == 0))
    def _():
        yb_ref[...] = jnp.zeros(yb_ref.shape, F32)

    def pipelined(x_cur, x_next, y_cur, y_prev):
        for r in range(MOE_RB):
            gather_row(start_next, r, x_next)
        expert_mlp(x_cur, y_cur)
        for r0 in range(0, MOE_RB, SUBLANES):
            scatter_group(start_prev, r0, n_prev, y_prev)

    @pl.when(e % 2 == 0)
    def _():
        pipelined(xa_ref, xb_ref, ya_ref, yb_ref)

    @pl.when(e % 2 == 1)
    def _():
        pipelined(xb_ref, xa_ref, yb_ref, ya_ref)

    def extra_block(b, c):
        base = start + b * MOE_RB
        gather_loop(base, xx_ref)
        expert_mlp(xx_ref, yx_ref)
        scatter_loop(base, jnp.minimum(MOE_RB, n - b * MOE_RB), yx_ref)
        return c

    lax.fori_loop(1, (n + MOE_RB - 1) // MOE_RB, extra_block, 0)

    @pl.when(e == N_EXPERTS - 1)
    def _():
        scatter_loop(start, jnp.minimum(n, MOE_RB), yb_ref)


FINAL_TM = 512


def _final_kernel(x1_ref, moe_ref, g_ref, o_ref):
    tm = o_ref.shape[0]
    v = jnp.concatenate([x1_ref[pl.ds(j, tm, stride=ROW_CHUNKS), :] + moe_ref[0, pl.ds(j, tm, stride=ROW_CHUNKS), :]
                         for j in range(ROW_CHUNKS)], axis=1)
    o_ref[...] = v * lax.rsqrt(jnp.mean(v * v, axis=-1, keepdims=True) + RMS_EPS) * g_ref[...]


MXU_COLS = 256


def _expert_weights_kernel(win_ref, wout_ref, perm_ref, winp_ref, woutb_ref):
    w = win_ref[0].astype(BF16)
    for c in range(w.shape[1] // MXU_COLS):
        cols = slice(c * MXU_COLS, (c + 1) * MXU_COLS)
        winp_ref[0, :, cols] = jnp.dot(w[:, cols], perm_ref[...], preferred_element_type=F32).astype(BF16)
    woutb_ref[0] = wout_ref[0].astype(BF16)


def _expert_weights(w_moe_in, w_moe_out):
    half = MXU_COLS // 2
    src = jnp.concatenate([jnp.arange(half) * 2, jnp.arange(half) * 2 + 1])
    perm = (jnp.arange(MXU_COLS)[:, None] == src[None, :]).astype(BF16)
    per_expert = lambda a: pl.BlockSpec((1,) + a.shape[1:], lambda e: (e, 0, 0))
    return pl.pallas_call(
        _expert_weights_kernel,
        grid=(N_EXPERTS,),
        in_specs=[per_expert(w_moe_in), per_expert(w_moe_out), _const_spec(perm.shape)],
        out_specs=(per_expert(w_moe_in), per_expert(w_moe_out)),
        out_shape=(jax.ShapeDtypeStruct(w_moe_in.shape, BF16), jax.ShapeDtypeStruct(w_moe_out.shape, BF16)),
        compiler_params=_cparams(("arbitrary",)),
        name="expert_weights",
    )(w_moe_in, w_moe_out, perm)


def _moe(h, x1, idx4, gate4, w_moe_in, b_moe_in, w_moe_out, b_moe_out, norm_f_g):
    n = h.shape[0] // ROW_CHUNKS
    tt = min(MOE_TT, n)
    n_tiles = n // tt
    pairs = TOP_K_EXPERTS * tt
    cap = pairs + MOE_RB

    def per_tile(a):
        return jnp.transpose(a[:TOP_K_EXPERTS].reshape(TOP_K_EXPERTS, n_tiles, tt), (1, 0, 2)).reshape(n_tiles, pairs)

    local = jnp.arange(n, dtype=I32) % tt
    keys, gate = lax.sort((per_tile(idx4 * tt + local[None, :]), per_tile(gate4)), dimension=1, num_keys=1)
    bounds = jnp.arange(N_EXPERTS + 1, dtype=I32) * tt
    below = jnp.sum(keys[:, :, None] < bounds[None, None, :], axis=1).astype(I32)
    off, cnt = below[:, :-1], below[:, 1:] - below[:, :-1]
    pad = ((0, 0), (0, cap - pairs))
    tok = jnp.pad(keys % tt, pad).reshape(n_tiles, 1, cap)
    gate = jnp.pad(gate, pad).reshape(n_tiles, 1, cap)

    win, wout = _expert_weights(w_moe_in, w_moe_out)
    half = MXU_COLS // 2
    bin_ = jnp.swapaxes(b_moe_in.reshape(N_EXPERTS, -1, half, 2), 2, 3).reshape(N_EXPERTS, 1, 2 * D_EXPERT)
    bout = b_moe_out.reshape(N_EXPERTS, 1, D_MODEL)

    out_rows = (tt + 1) * ROW_CHUNKS
    tile1 = pl.BlockSpec((tt * ROW_CHUNKS, LANES), lambda i, e, *_: (i, 0), pipeline_mode=pl.Buffered(1))
    smem_list = pl.BlockSpec((1, 1, cap), lambda i, e, *_: (i, 0, 0), memory_space=pltpu.SMEM)
    per_expert = lambda shape: pl.BlockSpec((1,) + shape, lambda i, e, *_: (e, 0, 0))
    row_tile = pltpu.VMEM((ROW_CHUNKS * MOE_PITCH, LANES), F32)
    grid_spec = pltpu.PrefetchScalarGridSpec(
        num_scalar_prefetch=2,
        grid=(n_tiles, N_EXPERTS),
        in_specs=[smem_list, smem_list, tile1,
                  per_expert((D_MODEL, 2 * D_EXPERT)), per_expert((1, 2 * D_EXPERT)),
                  per_expert((D_EXPERT, D_MODEL)), per_expert((1, D_MODEL))],
        out_specs=pl.BlockSpec((out_rows, LANES), lambda i, e, *_: (i, 0)),
        scratch_shapes=[row_tile] * 6,
    )
    moe_out = pl.pallas_call(
        _moe_kernel,
        grid_spec=grid_spec,
        out_shape=jax.ShapeDtypeStruct((n_tiles * out_rows, LANES), F32),
        compiler_params=_cparams(("arbitrary", "arbitrary")),
        name="moe",
    )(cnt.reshape(-1), off.reshape(-1).astype(I32), tok, gate, h, win, bin_, wout, bout)

    tm = min(FINAL_TM, tt)
    per_tile = tt // tm
    return pl.pallas_call(
        _final_kernel,
        grid=(n_tiles, per_tile),
        in_specs=[pl.BlockSpec((tm * ROW_CHUNKS, LANES), lambda i, j: (i * per_tile + j, 0)),
                  pl.BlockSpec((1, tm * ROW_CHUNKS, LANES), lambda i, j: (i, j, 0)),
                  _const_spec((1, D_MODEL))],
        out_specs=pl.BlockSpec((tm, D_MODEL), lambda i, j: (i * per_tile + j, 0)),
        out_shape=jax.ShapeDtypeStruct((n, D_MODEL), F32),
        compiler_params=_cparams(("arbitrary", "arbitrary")),
        name="final_norm",
    )(x1, moe_out.reshape(n_tiles, out_rows, LANES), norm_f_g.reshape(1, D_MODEL))


def kernel(x, norm1_g, w_in, b_gate, ssm_lam_re, ssm_lam_im, ssm_log_dt, ssm_b_re, ssm_b_im, ssm_c_re, ssm_c_im, ssm_d, w_glu, b_glu, w_proj_ssm, w_proj_attn, w_out, norm2_g, w_router, b_router, w_moe_in, b_moe_in, w_moe_out, b_moe_out, norm_f_g):
    bsz, seq_len, _ = x.shape
    n = bsz * seq_len
    x2 = x.reshape(n, D_MODEL)
    u, qt, k, vt, qit, ki, wit, gs, ga = _in_proj(x2, norm1_g[0], w_in[0], b_gate[0], seq_len)
    y_ssm = _s5(u, bsz, seq_len, ssm_lam_re[0], ssm_lam_im[0], ssm_log_dt[0], ssm_b_re[0], ssm_b_im[0],
                ssm_c_re[0], ssm_c_im[0], ssm_d[0], w_glu[0], b_glu[0])
    y_att = _attention(qt, qit, wit, k, ki, vt, bsz, seq_len)
    x1, h, idx4, gate4 = _mix(x2, y_ssm, y_att, gs, ga, w_proj_ssm[0], w_proj_attn[0], w_out[0], norm2_g[0],
                              w_router[0], b_router[0])
    out = _moe(h, x1, idx4, gate4, w_moe_in[0], b_moe_in[0], w_moe_out[0], b_moe_out[0], norm_f_g)
    return out.reshape(x.shape)
```

```python
import functools
import math

import jax
import jax.numpy as jnp
from jax import lax
from jax.experimental import pallas as pl
from jax.experimental.pallas import tpu as pltpu

D_MODEL = 1024
D_SSM = 512
SSM_GROUP = 16
N_SSM_GROUPS = 32
SSM_STATE = 64
N_HEADS = 8
N_KV_HEADS = 2
HEAD_DIM = 64
D_ATTN = N_HEADS * HEAD_DIM
D_KV = N_KV_HEADS * HEAD_DIM
IDX_HEADS = 16
IDX_DIM = 64
TOPK = 256
ROPE_THETA = 10000.0
N_EXPERTS = 32
TOP_K_EXPERTS = 4
D_EXPERT = D_MODEL
SWIGLU_LIMIT = 7.0
SWIGLU_ALPHA = 1.702
RMS_EPS = 1e-5
NEG = -1e30

LANES = 128
SUBLANES = 8
VMEM_LIMIT = 56 * 1024 * 1024

F32 = jnp.float32
BF16 = jnp.bfloat16
I32 = jnp.int32


def _cparams(sem):
    return pltpu.CompilerParams(dimension_semantics=sem, vmem_limit_bytes=VMEM_LIMIT)


def _const_spec(shape):
    nd = len(shape)
    return pl.BlockSpec(shape, lambda *_: (0,) * nd)


IN_TM = 512


def _in_proj_kernel(x_ref, g_ref, cos_ref, sin_ref, bg_ref,
                    wu_ref, wq_ref, wkv_ref, wqi_ref, wkw_ref, wgs_ref, wga_ref,
                    u_ref, qt_ref, k_ref, vt_ref, qit_ref, ki_ref, wit_ref, gs_ref, ga_ref):
    x = x_ref[...]
    xn = x * lax.rsqrt(jnp.mean(x * x, axis=-1, keepdims=True) + RMS_EPS) * g_ref[...]
    xb = xn.astype(BF16)

    def mm(w_ref):
        return jnp.dot(xb, w_ref[...], preferred_element_type=F32)

    cos = cos_ref[...]
    sin = sin_ref[...]
    lane = lax.broadcasted_iota(I32, cos.shape, 1)
    first_half = lane % HEAD_DIM < HEAD_DIM // 2

    def rope(a):
        cols = []
        for c in range(a.shape[1] // LANES):
            blk = a[:, c * LANES:(c + 1) * LANES]
            partner = jnp.where(first_half, pltpu.roll(blk, LANES - HEAD_DIM // 2, 1),
                                pltpu.roll(blk, HEAD_DIM // 2, 1))
            cols.append(blk * cos + partner * sin)
        return jnp.concatenate(cols, axis=1)

    u_ref[...] = mm(wu_ref)
    q = rope(mm(wq_ref)) * (HEAD_DIM ** -0.5 * math.log2(math.e))
    qt_ref[...] = q.T.astype(BF16)
    kv = mm(wkv_ref)
    k = rope(kv[:, :D_KV])
    for g in range(N_KV_HEADS):
        k_ref[g] = k[:, g * HEAD_DIM:(g + 1) * HEAD_DIM].astype(BF16)
    vt_ref[...] = kv[:, D_KV:].T.astype(BF16)
    kw = mm(wkw_ref)
    ki_ref[...] = rope(kw[:, :LANES])[:, :IDX_DIM].astype(BF16)
    qi = rope(mm(wqi_ref)) * (IDX_DIM ** -0.5)
    qit_ref[...] = qi.T.astype(BF16)
    wit_ref[...] = (kw[:, LANES:2 * LANES].T)[:IDX_HEADS] * (IDX_HEADS ** -0.5)
    gs_ref[...] = jax.nn.sigmoid(mm(wgs_ref) + bg_ref[0:1, :]).astype(BF16)
    ga_ref[...] = jax.nn.sigmoid(mm(wga_ref) + bg_ref[1:2, :]).astype(BF16)


def _in_proj(x2, norm1_g, w_in, b_gate, seq_len):
    n = x2.shape[0]
    tm = min(IN_TM, seq_len)
    o = 0
    parts = []
    for width in (D_SSM, D_ATTN, D_KV, D_KV, IDX_HEADS * IDX_DIM, IDX_DIM, IDX_HEADS, D_MODEL, D_MODEL):
        parts.append(w_in[:, o:o + width])
        o += width
    wu, wq, wk, wv, wqi, wki, wwi, wgs, wga = parts
    wkv = jnp.concatenate([wk, wv], axis=1)
    wkw = jnp.concatenate([wki, jnp.zeros((D_MODEL, LANES - IDX_DIM), F32),
                           wwi, jnp.zeros((D_MODEL, LANES - IDX_HEADS), F32)], axis=1)
    weights = [w.astype(BF16) for w in (wu, wq, wkv, wqi, wkw, wgs, wga)]

    half = HEAD_DIM // 2
    inv = ROPE_THETA ** (-jnp.arange(half, dtype=F32) / half)
    ang = jnp.arange(seq_len, dtype=F32)[:, None] * inv[None, :]
    cos = jnp.tile(jnp.cos(ang), (1, 4))
    sin = jnp.tile(jnp.concatenate([-jnp.sin(ang), jnp.sin(ang)], axis=1), (1, 2))

    tiles_per_seq = seq_len // tm
    row = lambda i: (i, 0)
    col = lambda i: (0, i)
    out_shapes = (
        jax.ShapeDtypeStruct((n, D_SSM), F32),
        jax.ShapeDtypeStruct((D_ATTN, n), BF16),
        jax.ShapeDtypeStruct((N_KV_HEADS, n, HEAD_DIM), BF16),
        jax.ShapeDtypeStruct((D_KV, n), BF16),
        jax.ShapeDtypeStruct((IDX_HEADS * IDX_DIM, n), BF16),
        jax.ShapeDtypeStruct((n, IDX_DIM), BF16),
        jax.ShapeDtypeStruct((IDX_HEADS, n), F32),
        jax.ShapeDtypeStruct((n, D_MODEL), BF16),
        jax.ShapeDtypeStruct((n, D_MODEL), BF16),
    )
    out_specs = (
        pl.BlockSpec((tm, D_SSM), row),
        pl.BlockSpec((D_ATTN, tm), col),
        pl.BlockSpec((N_KV_HEADS, tm, HEAD_DIM), lambda i: (0, i, 0)),
        pl.BlockSpec((D_KV, tm), col),
        pl.BlockSpec((IDX_HEADS * IDX_DIM, tm), col),
        pl.BlockSpec((tm, IDX_DIM), row),
        pl.BlockSpec((IDX_HEADS, tm), col),
        pl.BlockSpec((tm, D_MODEL), row),
        pl.BlockSpec((tm, D_MODEL), row),
    )
    in_specs = [
        pl.BlockSpec((tm, D_MODEL), row),
        _const_spec((1, D_MODEL)),
        pl.BlockSpec((tm, LANES), lambda i: (i % tiles_per_seq, 0)),
        pl.BlockSpec((tm, LANES), lambda i: (i % tiles_per_seq, 0)),
        _const_spec((2, D_MODEL)),
    ] + [_const_spec(w.shape) for w in weights]
    return pl.pallas_call(
        _in_proj_kernel,
        grid=(n // tm,),
        in_specs=in_specs,
        out_specs=out_specs,
        out_shape=out_shapes,
        compiler_params=_cparams(("arbitrary",)),
        name="in_proj",
    )(x2, norm1_g.reshape(1, D_MODEL), cos, sin, b_gate, *weights)


S5_T = 512
S5_CHUNK = 512
N_STATE = N_SSM_GROUPS * SSM_STATE
S5_HALF_IN = D_SSM // 2
S5_HALF_ST = N_STATE // 2


def _cmul(ar, ai, br, bi):
    return ar * br - ai * bi, ar * bi + ai * br


def _s5_kernel(u_ref, lr_ref, li_ref, ldt_ref, bre_ref, bim_ref, cre_ref, cim_ref, d_ref,
               wglu_ref, bglu_ref, y_ref,
               bb_ref, cc_ref, pw_re_ref, pw_im_ref, st_re_ref, st_im_ref, carry_re_ref, carry_im_ref):
    first = jnp.logical_and(pl.program_id(0) == 0, pl.program_id(1) == 0)

    @pl.when(first)
    def _():
        lr, li = lr_ref[...], li_ref[...]
        dt = jnp.exp(ldt_ref[...])
        mag = jnp.exp(lr * dt)
        ar = mag * jnp.cos(li * dt)
        ai = mag * jnp.sin(li * dt)
        den = lr * lr + li * li
        zr = ((ar - 1.0) * lr + ai * li) / den
        zi = (ai * lr - (ar - 1.0) * li) / den
        for h in range(2):
            rows = slice(h * S5_HALF_IN, (h + 1) * S5_HALF_IN)
            cols = slice(h * S5_HALF_ST, (h + 1) * S5_HALF_ST)
            bre, bim = bre_ref[rows, cols], bim_ref[rows, cols]
            bb_ref[h, :, :S5_HALF_ST] = (zr[:, cols] * bre - zi[:, cols] * bim).astype(BF16)
            bb_ref[h, :, S5_HALF_ST:] = (zr[:, cols] * bim + zi[:, cols] * bre).astype(BF16)
            cc_ref[h, :S5_HALF_ST, :] = cre_ref[cols, rows].astype(BF16)
            cc_ref[h, S5_HALF_ST:, :] = (-cim_ref[cols, rows]).astype(BF16)
        pr, pi = ar, ai
        pw_re_ref[0:1, :] = pr
        pw_im_ref[0:1, :] = pi
        for n in range(1, SUBLANES):
            pr, pi = _cmul(pr, pi, ar, ai)
            pw_re_ref[n:n + 1, :] = pr
            pw_im_ref[n:n + 1, :] = pi

    @pl.when(pl.program_id(1) == 0)
    def _():
        carry_re_ref[...] = jnp.zeros_like(carry_re_ref)
        carry_im_ref[...] = jnp.zeros_like(carry_im_ref)

    u = u_ref[...]
    ub = u.astype(BF16)
    for h in range(2):
        bu = jnp.dot(ub[:, h * S5_HALF_IN:(h + 1) * S5_HALF_IN], bb_ref[h], preferred_element_type=F32)
        st_re_ref[:, h * S5_HALF_ST:(h + 1) * S5_HALF_ST] = bu[:, :S5_HALF_ST]
        st_im_ref[:, h * S5_HALF_ST:(h + 1) * S5_HALF_ST] = bu[:, S5_HALF_ST:]

    t_len = u.shape[0]
    row = lax.broadcasted_iota(I32, (SUBLANES, S5_CHUNK), 0)
    for c in range(N_STATE // S5_CHUNK):
        lanes = slice(c * S5_CHUNK, (c + 1) * S5_CHUNK)
        p_re, p_im = pw_re_ref[:, lanes], pw_im_ref[:, lanes]
        steps = []
        for d in (1, 2, 4):
            a_re = jnp.where(row >= d, jnp.broadcast_to(p_re[d - 1:d, :], row.shape), 0.0)
            a_im = jnp.where(row >= d, jnp.broadcast_to(p_im[d - 1:d, :], row.shape), 0.0)
            steps.append((d, a_re, a_im))

        def block(i, carry):
            c_re, c_im = carry
            r0 = pl.multiple_of(i * SUBLANES, SUBLANES)
            xr = st_re_ref[pl.ds(r0, SUBLANES), lanes]
            xi = st_im_ref[pl.ds(r0, SUBLANES), lanes]
            for d, a_re, a_im in steps:
                sr = pltpu.roll(xr, d, 0)
                si = pltpu.roll(xi, d, 0)
                tr, ti = _cmul(a_re, a_im, sr, si)
                xr, xi = xr + tr, xi + ti
            tr, ti = _cmul(p_re, p_im, jnp.broadcast_to(c_re, xr.shape), jnp.broadcast_to(c_im, xi.shape))
            xr, xi = xr + tr, xi + ti
            st_re_ref[pl.ds(r0, SUBLANES), lanes] = xr
            st_im_ref[pl.ds(r0, SUBLANES), lanes] = xi
            return xr[SUBLANES - 1:, :], xi[SUBLANES - 1:, :]

        def two_blocks(j, carry):
            return block(2 * j + 1, block(2 * j, carry))

        c_re, c_im = lax.fori_loop(0, t_len // (2 * SUBLANES), two_blocks,
                                   (carry_re_ref[:, lanes], carry_im_ref[:, lanes]))
        carry_re_ref[:, lanes] = c_re
        carry_im_ref[:, lanes] = c_im

    ys = []
    for h in range(2):
        cols = slice(h * S5_HALF_ST, (h + 1) * S5_HALF_ST)
        xs = jnp.concatenate([st_re_ref[:, cols], st_im_ref[:, cols]], axis=1).astype(BF16)
        ys.append(jnp.dot(xs, cc_ref[h], preferred_element_type=F32))
    y = jnp.concatenate(ys, axis=1) + d_ref[...] * u
    y = jax.nn.gelu(y)
    gate = jnp.dot(y.astype(BF16), wglu_ref[...], preferred_element_type=F32) + bglu_ref[...]
    y_ref[...] = (y * jax.nn.sigmoid(gate)).astype(BF16)


def _s5(u, batch, seq_len, lam_re, lam_im, log_dt, b_re, b_im, c_re, c_im, d_skip, w_glu, b_glu):
    n = u.shape[0]
    t = min(S5_T, seq_len)
    g, p, hh = N_SSM_GROUPS, SSM_STATE, SSM_GROUP
    same_group = (jnp.arange(D_SSM)[:, None] // hh) == (jnp.arange(N_STATE)[None, :] // p)

    def b_blockdiag(b):
        rows = jnp.transpose(b, (0, 2, 1)).reshape(D_SSM, p)
        return jnp.where(same_group, jnp.tile(rows, (1, g)), 0.0)

    def c_blockdiag(c):
        rows = jnp.transpose(c, (0, 2, 1)).reshape(N_STATE, hh)
        return jnp.where(same_group.T, jnp.tile(rows, (1, g)), 0.0)

    bre_bd, bim_bd = b_blockdiag(b_re), b_blockdiag(b_im)
    cre_bd, cim_bd = c_blockdiag(c_re), c_blockdiag(c_im)
    lr = lam_re.reshape(1, N_STATE)
    li = lam_im.reshape(1, N_STATE)
    ldt = jnp.repeat(log_dt, p).reshape(1, N_STATE)
    tiles = seq_len // t
    consts = [lr, li, ldt, bre_bd, bim_bd, cre_bd, cim_bd, d_skip.reshape(1, D_SSM),
              w_glu.astype(BF16), b_glu.reshape(1, D_SSM)]
    return pl.pallas_call(
        _s5_kernel,
        grid=(batch, tiles),
        in_specs=[pl.BlockSpec((t, D_SSM), lambda b, i: (b * tiles + i, 0))]
        + [_const_spec(c.shape) for c in consts],
        out_specs=pl.BlockSpec((t, D_SSM), lambda b, i: (b * tiles + i, 0)),
        out_shape=jax.ShapeDtypeStruct((n, D_SSM), BF16),
        scratch_shapes=[
            pltpu.VMEM((2, S5_HALF_IN, 2 * S5_HALF_ST), BF16),
            pltpu.VMEM((2, 2 * S5_HALF_ST, S5_HALF_IN), BF16),
            pltpu.VMEM((SUBLANES, N_STATE), F32),
            pltpu.VMEM((SUBLANES, N_STATE), F32),
            pltpu.VMEM((t, N_STATE), F32),
            pltpu.VMEM((t, N_STATE), F32),
            pltpu.VMEM((1, N_STATE), F32),
            pltpu.VMEM((1, N_STATE), F32),
        ],
        compiler_params=_cparams(("arbitrary", "arbitrary")),
        name="s5",
    )(u, *consts)


AT_TQ = 256
AT_TK = 128
INT_MIN = -2 ** 31


def _attn_kernel(qt_ref, qit_ref, wit_ref, qit_next_ref, wit_next_ref, k_ref, ki_ref, vt_ref, y_ref,
                 score_ref, m_ref, l_ref, acc_ref, lg_ref, cut_ref, *, seq_len, n_sel):
    j = pl.program_id(1)
    last_block = pl.num_programs(1) - 1
    tq, tk = AT_TQ, AT_TK
    n_diag = tq // tk
    n_full = j * n_diag
    n_tot = n_full + n_diag
    q_pos = j * tq + lax.broadcasted_iota(I32, (tk, tq), 1)
    k_iota = lax.broadcasted_iota(I32, (tk, tq), 0)
    scores = score_ref.at[j % 2]
    scores_next = score_ref.at[(j + 1) % 2]

    def tile_start(kt):
        return pl.multiple_of(kt * tk, tk)

    def score_tile(kt, q_ref, w_ref, dst, dst_pos):
        r0 = tile_start(kt)
        ki_t = ki_ref[pl.ds(r0, tk), :]
        acc = jnp.zeros((tk, tq), F32)
        for h in range(IDX_HEADS):
            s = jnp.dot(ki_t, q_ref[h * IDX_DIM:(h + 1) * IDX_DIM, :], preferred_element_type=F32)
            acc = acc + w_ref[h:h + 1, :] * jnp.maximum(s, 0.0)
        dst[pl.ds(r0, tk), :] = jnp.where(r0 + k_iota <= dst_pos, acc, NEG)

    for dd in range(n_diag):
        score_tile(n_full + dd, qit_ref, wit_ref, scores, q_pos)

    n_beyond = seq_len - (j + 1) * tq

    def candidate(u):
        s = u ^ INT_MIN
        return lax.bitcast_convert_type(jnp.where(s < 0, s ^ 0x7FFFFFFF, s), F32)

    n_acc = 4

    def count_tile(c, cnts, cand):
        r0 = pl.multiple_of(c * tq, tq)
        sc = scores[pl.ds(r0, tq), :]
        cnts = list(cnts)
        for i in range(tq // SUBLANES):
            rows = sc[i * SUBLANES:(i + 1) * SUBLANES]
            cnts[i % n_acc] = jnp.where(rows >= cand, cnts[i % n_acc] + 1, cnts[i % n_acc])
        return tuple(cnts)

    zero = jnp.zeros((SUBLANES, tq), I32)

    def bit_step(b, state):
        thr_u, n_ge = state
        cand_u = thr_u | lax.shift_left(jnp.int32(1), 31 - b)
        cand = candidate(cand_u)
        cnts = lax.fori_loop(0, j + 1, functools.partial(count_tile, cand=cand), (zero,) * n_acc)
        cnt = (cnts[0] + cnts[1]) + (cnts[2] + cnts[3])
        total = jnp.sum(cnt, axis=0, keepdims=True) + jnp.where(cand <= NEG, n_beyond, 0)
        keep = total >= n_sel
        return jnp.where(keep, cand_u, thr_u), jnp.where(keep, total, n_ge)

    thr_u, n_ge = lax.fori_loop(0, 32, bit_step,
                                (jnp.zeros((1, tq), I32), jnp.full((1, tq), seq_len, I32)))
    thr = candidate(thr_u)

    sub_iota = lax.broadcasted_iota(I32, (SUBLANES, tq), 0)

    def rows_of(c):
        r0 = pl.multiple_of(c * tq, tq)
        sc = scores[pl.ds(r0, tq), :]
        return [(sc[i * SUBLANES:(i + 1) * SUBLANES], r0 + i * SUBLANES + sub_iota)
                for i in range(tq // SUBLANES)]

    cut_ref[...] = jnp.full((1, tq), seq_len, I32)

    @pl.when(jnp.max(n_ge) > n_sel)
    def _():
        def above_tile(c, gt):
            for rows, _ in rows_of(c):
                gt = jnp.where(rows > thr, gt + 1, gt)
            return gt

        gt = lax.fori_loop(0, j + 1, above_tile, zero)
        n_gt = jnp.sum(gt, axis=0, keepdims=True) + jnp.where(thr < NEG, n_beyond, 0)
        need = n_sel - n_gt
        n_bits = (seq_len - 1).bit_length()

        def eq_tile(c, cnt, cand):
            for rows, pos in rows_of(c):
                cnt = jnp.where(jnp.logical_and(rows == thr, pos < cand), cnt + 1, cnt)
            return cnt

        def index_step(b, c0):
            cand = c0 | lax.shift_left(jnp.int32(1), n_bits - 1 - b)
            cnt = lax.fori_loop(0, j + 1, functools.partial(eq_tile, cand=cand), zero)
            below = jnp.sum(cnt, axis=0, keepdims=True) + jnp.where(
                thr == NEG, jnp.clip(cand - (j + 1) * tq, 0, n_beyond), 0)
            return jnp.where(below < need, cand, c0)

        cut_ref[...] = lax.fori_loop(0, n_bits, index_step, jnp.zeros((1, tq), I32))

    cut = cut_ref[...]

    m_ref[...] = jnp.full(m_ref.shape, NEG, F32)
    l_ref[...] = jnp.zeros(l_ref.shape, F32)
    acc_ref[...] = jnp.zeros(acc_ref.shape, F32)
    rep = N_HEADS // N_KV_HEADS

    q_row = q_pos[0:1, :]
    tie_row = jnp.minimum(cut, q_row)

    def logits(kt, slot):
        r0 = tile_start(kt)
        sc = scores[pl.ds(r0, tk), :]
        last = jnp.where(sc > thr, q_row, jnp.where(sc == thr, tie_row, -1))
        bias = jnp.where(r0 + k_iota <= last, 0.0, NEG)
        for g in range(N_KV_HEADS):
            k_t = k_ref[g, pl.ds(r0, tk), :]
            for h in range(g * rep, (g + 1) * rep):
                q_h = qt_ref[h * HEAD_DIM:(h + 1) * HEAD_DIM, :]
                lg_ref[slot, h] = jnp.dot(k_t, q_h, preferred_element_type=F32) + bias

    ones_rows = jnp.ones((2 * SUBLANES, tk), BF16)

    def softmax_values(kt, slot):
        r0 = tile_start(kt)
        for g in range(N_KV_HEADS):
            v_t = jnp.concatenate([vt_ref[g * HEAD_DIM:(g + 1) * HEAD_DIM, pl.ds(r0, tk)], ones_rows], axis=0)
            for h in range(g * rep, (g + 1) * rep):
                lg = lg_ref[slot, h]
                m_old = m_ref[h]
                m_new = jnp.maximum(m_old, jnp.max(lg, axis=0, keepdims=True))
                p = jnp.exp2(lg - m_new)
                alpha = jnp.exp2(m_old - m_new)
                pv = jnp.dot(v_t, p.astype(BF16), preferred_element_type=F32)
                l_ref[h] = alpha * l_ref[h] + pv[HEAD_DIM:HEAD_DIM + 1, :]
                acc_ref[h] = alpha * acc_ref[h] + pv[:HEAD_DIM, :]
                m_ref[h] = m_new

    logits(0, 0)

    def tile_pair(i, c, with_next):
        logits(2 * i + 1, 1)
        softmax_values(2 * i, 0)
        logits(jnp.minimum(2 * i + 2, n_tot - 1), 0)
        softmax_values(2 * i + 1, 1)
        if with_next:
            for kt in (2 * i, 2 * i + 1):
                score_tile(kt, qit_next_ref, wit_next_ref, scores_next, q_pos + tq)
        return c

    @pl.when(j < last_block)
    def _():
        lax.fori_loop(0, n_tot // 2, functools.partial(tile_pair, with_next=True), 0)

    @pl.when(j == last_block)
    def _():
        lax.fori_loop(0, n_tot // 2, functools.partial(tile_pair, with_next=False), 0)

    outs = [acc_ref[h] / l_ref[h] for h in range(N_HEADS)]
    y_ref[...] = jnp.concatenate(outs, axis=0).T.astype(BF16)


def _attention(qt, qit, wit, k, ki, vt, batch, seq_len):
    n = ki.shape[0]
    tq = AT_TQ
    nq = seq_len // tq
    n_sel = min(TOPK, seq_len // 4)
    qcol = lambda b, j: (0, b * nq + j)
    qcol_next = lambda b, j: (0, b * nq + jnp.minimum(j + 1, nq - 1))
    return pl.pallas_call(
        functools.partial(_attn_kernel, seq_len=seq_len, n_sel=n_sel),
        grid=(batch, nq),
        in_specs=[
            pl.BlockSpec((D_ATTN, tq), qcol),
            pl.BlockSpec((IDX_HEADS * IDX_DIM, tq), qcol),
            pl.BlockSpec((IDX_HEADS, tq), qcol),
            pl.BlockSpec((IDX_HEADS * IDX_DIM, tq), qcol_next),
            pl.BlockSpec((IDX_HEADS, tq), qcol_next),
            pl.BlockSpec((N_KV_HEADS, seq_len, HEAD_DIM), lambda b, j: (0, b, 0)),
            pl.BlockSpec((seq_len, IDX_DIM), lambda b, j: (b, 0)),
            pl.BlockSpec((D_KV, seq_len), lambda b, j: (0, b)),
        ],
        out_specs=pl.BlockSpec((tq, D_ATTN), lambda b, j: (b * nq + j, 0)),
        out_shape=jax.ShapeDtypeStruct((n, D_ATTN), BF16),
        scratch_shapes=[pltpu.VMEM((2, seq_len, tq), F32),
                        pltpu.VMEM((N_HEADS, 1, tq), F32),
                        pltpu.VMEM((N_HEADS, 1, tq), F32),
                        pltpu.VMEM((N_HEADS, HEAD_DIM, tq), F32),
                        pltpu.VMEM((2, N_HEADS, AT_TK, tq), F32),
                        pltpu.VMEM((1, tq), I32)],
        compiler_params=_cparams(("arbitrary", "arbitrary")),
        name="attn",
    )(qt, qit, wit, qit, wit, k, ki, vt)


MIX_TM = 512


def _split_bf16(a):
    hi = a.astype(BF16)
    return hi, (a - hi.astype(F32)).astype(BF16)


def _mix_kernel(x_ref, ys_ref, ya_ref, gs_ref, ga_ref, wps_ref, wpa_ref, wo_ref, g2_ref,
                wr_hi_ref, wr_lo_ref, br_ref, x1_ref, h_ref, idx_ref, gate_ref):
    ps = jnp.dot(ys_ref[...], wps_ref[...], preferred_element_type=F32)
    pa = jnp.dot(ya_ref[...], wpa_ref[...], preferred_element_type=F32)
    mixed = gs_ref[...].astype(F32) * ps + ga_ref[...].astype(F32) * pa
    x1 = x_ref[...] + jnp.dot(mixed.astype(BF16), wo_ref[...], preferred_element_type=F32)
    h = x1 * lax.rsqrt(jnp.mean(x1 * x1, axis=-1, keepdims=True) + RMS_EPS) * g2_ref[...]
    tm = x1.shape[0]
    for j in range(D_MODEL // LANES):
        x1_ref[pl.ds(j, tm, stride=D_MODEL // LANES), :] = x1[:, j * LANES:(j + 1) * LANES]
        h_ref[pl.ds(j, tm, stride=D_MODEL // LANES), :] = h[:, j * LANES:(j + 1) * LANES]
    h_hi, h_lo = _split_bf16(h)
    logits = (jnp.dot(h_hi, wr_hi_ref[...], preferred_element_type=F32)
              + jnp.dot(h_hi, wr_lo_ref[...], preferred_element_type=F32)
              + jnp.dot(h_lo, wr_hi_ref[...], preferred_element_type=F32)) + br_ref[...]
    lane = lax.broadcasted_iota(I32, logits.shape, 1)
    rest = logits
    firsts, vals = [], []
    for _ in range(TOP_K_EXPERTS):
        m = jnp.max(rest, axis=-1, keepdims=True)
        first = jnp.minimum(jnp.min(jnp.where(rest == m, lane, N_EXPERTS), axis=-1, keepdims=True),
                            N_EXPERTS - 1)
        firsts.append(first)
        vals.append(m)
        rest = jnp.where(lane == first, -jnp.inf, rest)
    es = [jnp.exp(v - vals[0]) for v in vals]
    denom = es[0] + es[1] + es[2] + es[3]
    wide = lax.broadcasted_iota(I32, (logits.shape[0], LANES), 1)
    idx_wide = jnp.zeros(wide.shape, I32)
    gate_wide = jnp.zeros(wide.shape, F32)
    for pick, (first, e) in enumerate(zip(firsts, es)):
        idx_wide = jnp.where(wide == pick, first, idx_wide)
        gate_wide = jnp.where(wide == pick, e / denom, gate_wide)
    idx_ref[...] = idx_wide.T[:SUBLANES]
    gate_ref[...] = gate_wide.T[:SUBLANES]


def _mix(x2, y_ssm, y_att, gs, ga, w_proj_ssm, w_proj_attn, w_out, norm2_g, w_router, b_router):
    n = x2.shape[0]
    tm = min(MIX_TM, n)
    chunks = D_MODEL // LANES
    row = lambda i: (i, 0)
    wr_hi, wr_lo = _split_bf16(w_router)
    consts = [w_proj_ssm.astype(BF16), w_proj_attn.astype(BF16), w_out.astype(BF16),
              norm2_g.reshape(1, D_MODEL), wr_hi, wr_lo, b_router.reshape(1, N_EXPERTS)]
    return pl.pallas_call(
        _mix_kernel,
        grid=(n // tm,),
        in_specs=[pl.BlockSpec((tm, D_MODEL), row), pl.BlockSpec((tm, D_SSM), row),
                  pl.BlockSpec((tm, D_ATTN), row), pl.BlockSpec((tm, D_MODEL), row),
                  pl.BlockSpec((tm, D_MODEL), row)] + [_const_spec(c.shape) for c in consts],
        out_specs=(pl.BlockSpec((tm * chunks, LANES), row), pl.BlockSpec((tm * chunks, LANES), row),
                   pl.BlockSpec((SUBLANES, tm), lambda i: (0, i)), pl.BlockSpec((SUBLANES, tm), lambda i: (0, i))),
        out_shape=(jax.ShapeDtypeStruct((n * chunks, LANES), F32), jax.ShapeDtypeStruct((n * chunks, LANES), F32),
                   jax.ShapeDtypeStruct((SUBLANES, n), I32), jax.ShapeDtypeStruct((SUBLANES, n), F32)),
        compiler_params=_cparams(("arbitrary",)),
        name="mix",
    )(x2, y_ssm, y_att, gs, ga, *consts)


MOE_TT = 2048
MOE_RB = 288
ROW_CHUNKS = D_MODEL // LANES
assert ROW_CHUNKS == SUBLANES
MOE_PITCH = MOE_RB + SUBLANES


def _moe_kernel(cnt_ref, off_ref, tok_ref, gate_ref, h_ref, win_ref, bin_ref, wout_ref, bout_ref,
                out_ref, xa_ref, xb_ref, xx_ref, ya_ref, yb_ref, yx_ref):
    i, e = pl.program_id(0), pl.program_id(1)
    trash = out_ref.shape[0] // SUBLANES - 1
    here = i * N_EXPERTS + e
    n, start = cnt_ref[here], off_ref[here]
    start_next = off_ref[i * N_EXPERTS + jnp.minimum(e + 1, N_EXPERTS - 1)]
    prev = i * N_EXPERTS + jnp.maximum(e - 1, 0)
    start_prev = off_ref[prev]
    n_prev = jnp.where(e > 0, jnp.minimum(cnt_ref[prev], MOE_RB), 0)

    def slab(t):
        return pl.ds(pl.multiple_of(t * SUBLANES, SUBLANES), SUBLANES)

    def tile_rows(r):
        return pl.ds(r, SUBLANES, stride=MOE_PITCH)

    def gather_row(base, r, x_ref):
        x_ref[tile_rows(r), :] = h_ref[slab(tok_ref[0, 0, base + r]), :]

    def gather_loop(base, x_ref):
        def body(r8, c):
            for rr in range(SUBLANES):
                gather_row(base, r8 * SUBLANES + rr, x_ref)
            return c

        lax.fori_loop(0, MOE_RB // SUBLANES, body, 0)

    def expert_mlp(x_ref, y_ref):
        xg = jnp.concatenate([x_ref[j * MOE_PITCH:j * MOE_PITCH + MOE_RB, :] for j in range(ROW_CHUNKS)],
                             axis=1).astype(BF16)
        z = jnp.dot(xg, win_ref[0], preferred_element_type=F32) + bin_ref[0]
        half = MXU_COLS // 2
        groups = range(2 * D_EXPERT // MXU_COLS)
        zg = jnp.concatenate([z[:, c * MXU_COLS:c * MXU_COLS + half] for c in groups], axis=1)
        zl = jnp.concatenate([z[:, c * MXU_COLS + half:(c + 1) * MXU_COLS] for c in groups], axis=1)
        zg = jnp.minimum(zg, SWIGLU_LIMIT)
        zl = jnp.clip(zl, -SWIGLU_LIMIT, SWIGLU_LIMIT)
        act = zg * jax.nn.sigmoid(SWIGLU_ALPHA * zg) * (zl + 1.0)
        y = jnp.dot(act.astype(BF16), wout_ref[0], preferred_element_type=F32) + bout_ref[0]
        for j in range(ROW_CHUNKS):
            y_ref[j * MOE_PITCH:j * MOE_PITCH + MOE_RB, :] = y[:, j * LANES:(j + 1) * LANES]

    def updated(base, r, n_valid, y_ref):
        t = jnp.where(r < n_valid, tok_ref[0, 0, base + r], trash)
        return t, out_ref[slab(t), :] + gate_ref[0, 0, base + r] * y_ref[tile_rows(r), :]

    def scatter_group(base, r0, n_valid, y_ref):
        rows = [updated(base, r0 + rr, n_valid, y_ref) for rr in range(SUBLANES)]
        for t, v in rows:
            out_ref[slab(t), :] = v

    def scatter_loop(base, n_valid, y_ref):
        def body(r8, c):
            scatter_group(base, r8 * SUBLANES, n_valid, y_ref)
            return c

        lax.fori_loop(0, (n_valid + SUBLANES - 1) // SUBLANES, body, 0)

    @pl.when(e == 0)
    def _():
        out_ref[...] = jnp.zeros(out_ref.shape, F32)
        gather_loop(start, xa_ref)

    @pl.when(jnp.logical_and(i == 0, e == 0))
    def _():
        yb_ref[...] = jnp.zeros(yb_ref.shape, F32)

    def pipelined(x_cur, x_next, y_cur, y_prev):
        for r in range(MOE_RB):
            gather_row(start_next, r, x_next)
        expert_mlp(x_cur, y_cur)
        for r0 in range(0, MOE_RB, SUBLANES):
            scatter_group(start_prev, r0, n_prev, y_prev)

    @pl.when(e % 2 == 0)
    def _():
        pipelined(xa_ref, xb_ref, ya_ref, yb_ref)

    @pl.when(e % 2 == 1)
    def _():
        pipelined(xb_ref, xa_ref, yb_ref, ya_ref)

    def extra_block(b, c):
        base = start + b * MOE_RB
        gather_loop(base, xx_ref)
        expert_mlp(xx_ref, yx_ref)
        scatter_loop(base, jnp.minimum(MOE_RB, n - b * MOE_RB), yx_ref)
        return c

    lax.fori_loop(1, (n + MOE_RB - 1) // MOE_RB, extra_block, 0)

    @pl.when(e == N_EXPERTS - 1)
    def _():
        scatter_loop(start, jnp.minimum(n, MOE_RB), yb_ref)


FINAL_TM = 512


def _final_kernel(x1_ref, moe_ref, g_ref, o_ref):
    tm = o_ref.shape[0]
    v = jnp.concatenate([x1_ref[pl.ds(j, tm, stride=ROW_CHUNKS), :] + moe_ref[0, pl.ds(j, tm, stride=ROW_CHUNKS), :]
                         for j in range(ROW_CHUNKS)], axis=1)
    o_ref[...] = v * lax.rsqrt(jnp.mean(v * v, axis=-1, keepdims=True) + RMS_EPS) * g_ref[...]


MXU_COLS = 256


def _expert_weights_kernel(win_ref, wout_ref, perm_ref, winp_ref, woutb_ref):
    w = win_ref[0].astype(BF16)
    for c in range(w.shape[1] // MXU_COLS):
        cols = slice(c * MXU_COLS, (c + 1) * MXU_COLS)
        winp_ref[0, :, cols] = jnp.dot(w[:, cols], perm_ref[...], preferred_element_type=F32).astype(BF16)
    woutb_ref[0] = wout_ref[0].astype(BF16)


def _expert_weights(w_moe_in, w_moe_out):
    half = MXU_COLS // 2
    src = jnp.concatenate([jnp.arange(half) * 2, jnp.arange(half) * 2 + 1])
    perm = (jnp.arange(MXU_COLS)[:, None] == src[None, :]).astype(BF16)
    per_expert = lambda a: pl.BlockSpec((1,) + a.shape[1:], lambda e: (e, 0, 0))
    return pl.pallas_call(
        _expert_weights_kernel,
        grid=(N_EXPERTS,),
        in_specs=[per_expert(w_moe_in), per_expert(w_moe_out), _const_spec(perm.shape)],
        out_specs=(per_expert(w_moe_in), per_expert(w_moe_out)),
        out_shape=(jax.ShapeDtypeStruct(w_moe_in.shape, BF16), jax.ShapeDtypeStruct(w_moe_out.shape, BF16)),
        compiler_params=_cparams(("arbitrary",)),
        name="expert_weights",
    )(w_moe_in, w_moe_out, perm)


def _moe(h, x1, idx4, gate4, w_moe_in, b_moe_in, w_moe_out, b_moe_out, norm_f_g):
    n = h.shape[0] // ROW_CHUNKS
    tt = min(MOE_TT, n)
    n_tiles = n // tt
    pairs = TOP_K_EXPERTS * tt
    cap = pairs + MOE_RB

    def per_tile(a):
        return jnp.transpose(a[:TOP_K_EXPERTS].reshape(TOP_K_EXPERTS, n_tiles, tt), (1, 0, 2)).reshape(n_tiles, pairs)

    local = jnp.arange(n, dtype=I32) % tt
    keys, gate = lax.sort((per_tile(idx4 * tt + local[None, :]), per_tile(gate4)), dimension=1, num_keys=1)
    bounds = jnp.arange(N_EXPERTS + 1, dtype=I32) * tt
    below = jnp.sum(keys[:, :, None] < bounds[None, None, :], axis=1).astype(I32)
    off, cnt = below[:, :-1], below[:, 1:] - below[:, :-1]
    pad = ((0, 0), (0, cap - pairs))
    tok = jnp.pad(keys % tt, pad).reshape(n_tiles, 1, cap)
    gate = jnp.pad(gate, pad).reshape(n_tiles, 1, cap)

    win, wout = _expert_weights(w_moe_in, w_moe_out)
    half = MXU_COLS // 2
    bin_ = jnp.swapaxes(b_moe_in.reshape(N_EXPERTS, -1, half, 2), 2, 3).reshape(N_EXPERTS, 1, 2 * D_EXPERT)
    bout = b_moe_out.reshape(N_EXPERTS, 1, D_MODEL)

    out_rows = (tt + 1) * ROW_CHUNKS
    tile1 = pl.BlockSpec((tt * ROW_CHUNKS, LANES), lambda i, e, *_: (i, 0), pipeline_mode=pl.Buffered(1))
    smem_list = pl.BlockSpec((1, 1, cap), lambda i, e, *_: (i, 0, 0), memory_space=pltpu.SMEM)
    per_expert = lambda shape: pl.BlockSpec((1,) + shape, lambda i, e, *_: (e, 0, 0))
    row_tile = pltpu.VMEM((ROW_CHUNKS * MOE_PITCH, LANES), F32)
    grid_spec = pltpu.PrefetchScalarGridSpec(
        num_scalar_prefetch=2,
        grid=(n_tiles, N_EXPERTS),
        in_specs=[smem_list, smem_list, tile1,
                  per_expert((D_MODEL, 2 * D_EXPERT)), per_expert((1, 2 * D_EXPERT)),
                  per_expert((D_EXPERT, D_MODEL)), per_expert((1, D_MODEL))],
        out_specs=pl.BlockSpec((out_rows, LANES), lambda i, e, *_: (i, 0)),
        scratch_shapes=[row_tile] * 6,
    )
    moe_out = pl.pallas_call(
        _moe_kernel,
        grid_spec=grid_spec,
        out_shape=jax.ShapeDtypeStruct((n_tiles * out_rows, LANES), F32),
        compiler_params=_cparams(("arbitrary", "arbitrary")),
        name="moe",
    )(cnt.reshape(-1), off.reshape(-1).astype(I32), tok, gate, h, win, bin_, wout, bout)

    tm = min(FINAL_TM, tt)
    per_tile = tt // tm
    return pl.pallas_call(
        _final_kernel,
        grid=(n_tiles, per_tile),
        in_specs=[pl.BlockSpec((tm * ROW_CHUNKS, LANES), lambda i, j: (i * per_tile + j, 0)),
                  pl.BlockSpec((1, tm * ROW_CHUNKS, LANES), lambda i, j: (i, j, 0)),
                  _const_spec((1, D_MODEL))],
        out_specs=pl.BlockSpec((tm, D_MODEL), lambda i, j: (i * per_tile + j, 0)),
        out_shape=jax.ShapeDtypeStruct((n, D_MODEL), F32),
        compiler_params=_cparams(("arbitrary", "arbitrary")),
        name="final_norm",
    )(x1, moe_out.reshape(n_tiles, out_rows, LANES), norm_f_g.reshape(1, D_MODEL))


def kernel(x, norm1_g, w_in, b_gate, ssm_lam_re, ssm_lam_im, ssm_log_dt, ssm_b_re, ssm_b_im, ssm_c_re, ssm_c_im, ssm_d, w_glu, b_glu, w_proj_ssm, w_proj_attn, w_out, norm2_g, w_router, b_router, w_moe_in, b_moe_in, w_moe_out, b_moe_out, norm_f_g):
    bsz, seq_len, _ = x.shape
    n = bsz * seq_len
    x2 = x.reshape(n, D_MODEL)
    u, qt, k, vt, qit, ki, wit, gs, ga = _in_proj(x2, norm1_g[0], w_in[0], b_gate[0], seq_len)
    y_ssm = _s5(u, bsz, seq_len, ssm_lam_re[0], ssm_lam_im[0], ssm_log_dt[0], ssm_b_re[0], ssm_b_im[0],
                ssm_c_re[0], ssm_c_im[0], ssm_d[0], w_glu[0], b_glu[0])
    y_att = _attention(qt, qit, wit, k, ki, vt, bsz, seq_len)
    x1, h, idx4, gate4 = _mix(x2, y_ssm, y_att, gs, ga, w_proj_ssm[0], w_proj_attn[0], w_out[0], norm2_g[0],
                              w_router[0], b_router[0])
    out = _moe(h, x1, idx4, gate4, w_moe_in[0], b_moe_in[0], w_moe_out[0], b_moe_out[0], norm_f_g)
    return out.reshape(x.shape)
```

```python
import functools
import math

import jax
import jax.numpy as jnp
from jax import lax
from jax.experimental import pallas as pl
from jax.experimental.pallas import tpu as pltpu

D_MODEL = 1024
D_SSM = 512
SSM_GROUP = 16
N_SSM_GROUPS = 32
SSM_STATE = 64
N_HEADS = 8
N_KV_HEADS = 2
HEAD_DIM = 64
D_ATTN = N_HEADS * HEAD_DIM
D_KV = N_KV_HEADS * HEAD_DIM
IDX_HEADS = 16
IDX_DIM = 64
TOPK = 256
ROPE_THETA = 10000.0
N_EXPERTS = 32
TOP_K_EXPERTS = 4
D_EXPERT = D_MODEL
SWIGLU_LIMIT = 7.0
SWIGLU_ALPHA = 1.702
RMS_EPS = 1e-5
NEG = -1e30

LANES = 128
SUBLANES = 8
VMEM_LIMIT = 56 * 1024 * 1024

F32 = jnp.float32
BF16 = jnp.bfloat16
I32 = jnp.int32


def _cparams(sem):
    return pltpu.CompilerParams(dimension_semantics=sem, vmem_limit_bytes=VMEM_LIMIT)


def _const_spec(shape):
    nd = len(shape)
    return pl.BlockSpec(shape, lambda *_: (0,) * nd)


IN_TM = 1024


def _in_proj_kernel(x_ref, g_ref, cos_ref, sin_ref, bg_ref,
                    wu_ref, wq_ref, wkv_ref, wqi_ref, wkw_ref, wgs_ref, wga_ref,
                    u_ref, qt_ref, k_ref, vt_ref, qit_ref, ki_ref, wit_ref, gs_ref, ga_ref):
    x = x_ref[...]
    xn = x * lax.rsqrt(jnp.mean(x * x, axis=-1, keepdims=True) + RMS_EPS) * g_ref[...]
    xb = xn.astype(BF16)

    def mm(w_ref):
        return jnp.dot(xb, w_ref[...], preferred_element_type=F32)

    cos = cos_ref[...]
    sin = sin_ref[...]
    lane = lax.broadcasted_iota(I32, cos.shape, 1)
    first_half = lane % HEAD_DIM < HEAD_DIM // 2

    def rope(a):
        cols = []
        for c in range(a.shape[1] // LANES):
            blk = a[:, c * LANES:(c + 1) * LANES]
            partner = jnp.where(first_half, pltpu.roll(blk, LANES - HEAD_DIM // 2, 1),
                                pltpu.roll(blk, HEAD_DIM // 2, 1))
            cols.append(blk * cos + partner * sin)
        return jnp.concatenate(cols, axis=1)

    u_ref[...] = mm(wu_ref)
    q = rope(mm(wq_ref)) * (HEAD_DIM ** -0.5 * math.log2(math.e))
    qt_ref[...] = q.T.astype(BF16)
    kv = mm(wkv_ref)
    k = rope(kv[:, :D_KV])
    for g in range(N_KV_HEADS):
        k_ref[g] = k[:, g * HEAD_DIM:(g + 1) * HEAD_DIM].astype(BF16)
    vt_ref[...] = kv[:, D_KV:].T.astype(BF16)
    kw = mm(wkw_ref)
    ki_ref[...] = rope(kw[:, :LANES])[:, :IDX_DIM].astype(BF16)
    qi = rope(mm(wqi_ref)) * (IDX_DIM ** -0.5)
    qit_ref[...] = qi.T.astype(BF16)
    wit_ref[...] = (kw[:, LANES:2 * LANES].T)[:IDX_HEADS] * (IDX_HEADS ** -0.5)
    gs_ref[...] = jax.nn.sigmoid(mm(wgs_ref) + bg_ref[0:1, :]).astype(BF16)
    ga_ref[...] = jax.nn.sigmoid(mm(wga_ref) + bg_ref[1:2, :]).astype(BF16)


def _in_proj(x2, norm1_g, w_in, b_gate, seq_len):
    n = x2.shape[0]
    tm = min(IN_TM, seq_len)
    o = 0
    parts = []
    for width in (D_SSM, D_ATTN, D_KV, D_KV, IDX_HEADS * IDX_DIM, IDX_DIM, IDX_HEADS, D_MODEL, D_MODEL):
        parts.append(w_in[:, o:o + width])
        o += width
    wu, wq, wk, wv, wqi, wki, wwi, wgs, wga = parts
    wkv = jnp.concatenate([wk, wv], axis=1)
    wkw = jnp.concatenate([wki, jnp.zeros((D_MODEL, LANES - IDX_DIM), F32),
                           wwi, jnp.zeros((D_MODEL, LANES - IDX_HEADS), F32)], axis=1)
    weights = [w.astype(BF16) for w in (wu, wq, wkv, wqi, wkw, wgs, wga)]

    half = HEAD_DIM // 2
    inv = ROPE_THETA ** (-jnp.arange(half, dtype=F32) / half)
    ang = jnp.arange(seq_len, dtype=F32)[:, None] * inv[None, :]
    cos = jnp.tile(jnp.cos(ang), (1, 4))
    sin = jnp.tile(jnp.concatenate([-jnp.sin(ang), jnp.sin(ang)], axis=1), (1, 2))

    tiles_per_seq = seq_len // tm
    row = lambda i: (i, 0)
    col = lambda i: (0, i)
    out_shapes = (
        jax.ShapeDtypeStruct((n, D_SSM), F32),
        jax.ShapeDtypeStruct((D_ATTN, n), BF16),
        jax.ShapeDtypeStruct((N_KV_HEADS, n, HEAD_DIM), BF16),
        jax.ShapeDtypeStruct((D_KV, n), BF16),
        jax.ShapeDtypeStruct((IDX_HEADS * IDX_DIM, n), BF16),
        jax.ShapeDtypeStruct((n, IDX_DIM), BF16),
        jax.ShapeDtypeStruct((IDX_HEADS, n), F32),
        jax.ShapeDtypeStruct((n, D_MODEL), BF16),
        jax.ShapeDtypeStruct((n, D_MODEL), BF16),
    )
    out_specs = (
        pl.BlockSpec((tm, D_SSM), row),
        pl.BlockSpec((D_ATTN, tm), col),
        pl.BlockSpec((N_KV_HEADS, tm, HEAD_DIM), lambda i: (0, i, 0)),
        pl.BlockSpec((D_KV, tm), col),
        pl.BlockSpec((IDX_HEADS * IDX_DIM, tm), col),
        pl.BlockSpec((tm, IDX_DIM), row),
        pl.BlockSpec((IDX_HEADS, tm), col),
        pl.BlockSpec((tm, D_MODEL), row),
        pl.BlockSpec((tm, D_MODEL), row),
    )
    in_specs = [
        pl.BlockSpec((tm, D_MODEL), row),
        _const_spec((1, D_MODEL)),
        pl.BlockSpec((tm, LANES), lambda i: (i % tiles_per_seq, 0)),
        pl.BlockSpec((tm, LANES), lambda i: (i % tiles_per_seq, 0)),
        _const_spec((2, D_MODEL)),
    ] + [_const_spec(w.shape) for w in weights]
    return pl.pallas_call(
        _in_proj_kernel,
        grid=(n // tm,),
        in_specs=in_specs,
        out_specs=out_specs,
        out_shape=out_shapes,
        compiler_params=_cparams(("arbitrary",)),
        name="in_proj",
    )(x2, norm1_g.reshape(1, D_MODEL), cos, sin, b_gate, *weights)


S5_T = 1024
S5_CHUNK = 512
N_STATE = N_SSM_GROUPS * SSM_STATE
S5_HALF_IN = D_SSM // 2
S5_HALF_ST = N_STATE // 2


def _cmul(ar, ai, br, bi):
    return ar * br - ai * bi, ar * bi + ai * br


def _s5_kernel(u_ref, lr_ref, li_ref, ldt_ref, bre_ref, bim_ref, cre_ref, cim_ref, d_ref,
               wglu_ref, bglu_ref, y_ref,
               bb_ref, cc_ref, pw_re_ref, pw_im_ref, st_re_ref, st_im_ref, carry_re_ref, carry_im_ref):
    first = jnp.logical_and(pl.program_id(0) == 0, pl.program_id(1) == 0)

    @pl.when(first)
    def _():
        lr, li = lr_ref[...], li_ref[...]
        dt = jnp.exp(ldt_ref[...])
        mag = jnp.exp(lr * dt)
        ar = mag * jnp.cos(li * dt)
        ai = mag * jnp.sin(li * dt)
        den = lr * lr + li * li
        zr = ((ar - 1.0) * lr + ai * li) / den
        zi = (ai * lr - (ar - 1.0) * li) / den
        for h in range(2):
            rows = slice(h * S5_HALF_IN, (h + 1) * S5_HALF_IN)
            cols = slice(h * S5_HALF_ST, (h + 1) * S5_HALF_ST)
            bre, bim = bre_ref[rows, cols], bim_ref[rows, cols]
            bb_ref[h, :, :S5_HALF_ST] = (zr[:, cols] * bre - zi[:, cols] * bim).astype(BF16)
            bb_ref[h, :, S5_HALF_ST:] = (zr[:, cols] * bim + zi[:, cols] * bre).astype(BF16)
            cc_ref[h, :S5_HALF_ST, :] = cre_ref[cols, rows].astype(BF16)
            cc_ref[h, S5_HALF_ST:, :] = (-cim_ref[cols, rows]).astype(BF16)
        pr, pi = ar, ai
        pw_re_ref[0:1, :] = pr
        pw_im_ref[0:1, :] = pi
        for n in range(1, SUBLANES):
            pr, pi = _cmul(pr, pi, ar, ai)
            pw_re_ref[n:n + 1, :] = pr
            pw_im_ref[n:n + 1, :] = pi

    @pl.when(pl.program_id(1) == 0)
    def _():
        carry_re_ref[...] = jnp.zeros_like(carry_re_ref)
        carry_im_ref[...] = jnp.zeros_like(carry_im_ref)

    u = u_ref[...]
    ub = u.astype(BF16)
    for h in range(2):
        bu = jnp.dot(ub[:, h * S5_HALF_IN:(h + 1) * S5_HALF_IN], bb_ref[h], preferred_element_type=F32)
        st_re_ref[:, h * S5_HALF_ST:(h + 1) * S5_HALF_ST] = bu[:, :S5_HALF_ST]
        st_im_ref[:, h * S5_HALF_ST:(h + 1) * S5_HALF_ST] = bu[:, S5_HALF_ST:]

    t_len = u.shape[0]
    row = lax.broadcasted_iota(I32, (SUBLANES, S5_CHUNK), 0)
    for c in range(N_STATE // S5_CHUNK):
        lanes = slice(c * S5_CHUNK, (c + 1) * S5_CHUNK)
        p_re, p_im = pw_re_ref[:, lanes], pw_im_ref[:, lanes]
        steps = []
        for d in (1, 2, 4):
            a_re = jnp.where(row >= d, jnp.broadcast_to(p_re[d - 1:d, :], row.shape), 0.0)
            a_im = jnp.where(row >= d, jnp.broadcast_to(p_im[d - 1:d, :], row.shape), 0.0)
            steps.append((d, a_re, a_im))

        def block(i, carry):
            c_re, c_im = carry
            r0 = pl.multiple_of(i * SUBLANES, SUBLANES)
            xr = st_re_ref[pl.ds(r0, SUBLANES), lanes]
            xi = st_im_ref[pl.ds(r0, SUBLANES), lanes]
            for d, a_re, a_im in steps:
                sr = pltpu.roll(xr, d, 0)
                si = pltpu.roll(xi, d, 0)
                tr, ti = _cmul(a_re, a_im, sr, si)
                xr, xi = xr + tr, xi + ti
            tr, ti = _cmul(p_re, p_im, jnp.broadcast_to(c_re, xr.shape), jnp.broadcast_to(c_im, xi.shape))
            xr, xi = xr + tr, xi + ti
            st_re_ref[pl.ds(r0, SUBLANES), lanes] = xr
            st_im_ref[pl.ds(r0, SUBLANES), lanes] = xi
            return xr[SUBLANES - 1:, :], xi[SUBLANES - 1:, :]

        def two_blocks(j, carry):
            return block(2 * j + 1, block(2 * j, carry))

        c_re, c_im = lax.fori_loop(0, t_len // (2 * SUBLANES), two_blocks,
                                   (carry_re_ref[:, lanes], carry_im_ref[:, lanes]))
        carry_re_ref[:, lanes] = c_re
        carry_im_ref[:, lanes] = c_im

    ys = []
    for h in range(2):
        cols = slice(h * S5_HALF_ST, (h + 1) * S5_HALF_ST)
        xs = jnp.concatenate([st_re_ref[:, cols], st_im_ref[:, cols]], axis=1).astype(BF16)
        ys.append(jnp.dot(xs, cc_ref[h], preferred_element_type=F32))
    y = jnp.concatenate(ys, axis=1) + d_ref[...] * u
    y = jax.nn.gelu(y)
    gate = jnp.dot(y.astype(BF16), wglu_ref[...], preferred_element_type=F32) + bglu_ref[...]
    y_ref[...] = (y * jax.nn.sigmoid(gate)).astype(BF16)


def _s5(u, batch, seq_len, lam_re, lam_im, log_dt, b_re, b_im, c_re, c_im, d_skip, w_glu, b_glu):
    n = u.shape[0]
    t = min(S5_T, seq_len)
    g, p, hh = N_SSM_GROUPS, SSM_STATE, SSM_GROUP
    same_group = (jnp.arange(D_SSM)[:, None] // hh) == (jnp.arange(N_STATE)[None, :] // p)

    def b_blockdiag(b):
        rows = jnp.transpose(b, (0, 2, 1)).reshape(D_SSM, p)
        return jnp.where(same_group, jnp.tile(rows, (1, g)), 0.0)

    def c_blockdiag(c):
        rows = jnp.transpose(c, (0, 2, 1)).reshape(N_STATE, hh)
        return jnp.where(same_group.T, jnp.tile(rows, (1, g)), 0.0)

    bre_bd, bim_bd = b_blockdiag(b_re), b_blockdiag(b_im)
    cre_bd, cim_bd = c_blockdiag(c_re), c_blockdiag(c_im)
    lr = lam_re.reshape(1, N_STATE)
    li = lam_im.reshape(1, N_STATE)
    ldt = jnp.repeat(log_dt, p).reshape(1, N_STATE)
    tiles = seq_len // t
    consts = [lr, li, ldt, bre_bd, bim_bd, cre_bd, cim_bd, d_skip.reshape(1, D_SSM),
              w_glu.astype(BF16), b_glu.reshape(1, D_SSM)]
    return pl.pallas_call(
        _s5_kernel,
        grid=(batch, tiles),
        in_specs=[pl.BlockSpec((t, D_SSM), lambda b, i: (b * tiles + i, 0))]
        + [_const_spec(c.shape) for c in consts],
        out_specs=pl.BlockSpec((t, D_SSM), lambda b, i: (b * tiles + i, 0)),
        out_shape=jax.ShapeDtypeStruct((n, D_SSM), BF16),
        scratch_shapes=[
            pltpu.VMEM((2, S5_HALF_IN, 2 * S5_HALF_ST), BF16),
            pltpu.VMEM((2, 2 * S5_HALF_ST, S5_HALF_IN), BF16),
            pltpu.VMEM((SUBLANES, N_STATE), F32),
            pltpu.VMEM((SUBLANES, N_STATE), F32),
            pltpu.VMEM((t, N_STATE), F32),
            pltpu.VMEM((t, N_STATE), F32),
            pltpu.VMEM((1, N_STATE), F32),
            pltpu.VMEM((1, N_STATE), F32),
        ],
        compiler_params=_cparams(("arbitrary", "arbitrary")),
        name="s5",
    )(u, *consts)


AT_TQ = 256
AT_TK = 128
INT_MIN = -2 ** 31


def _attn_kernel(qt_ref, qit_ref, wit_ref, qit_next_ref, wit_next_ref, k_ref, ki_ref, vt_ref, y_ref,
                 score_ref, m_ref, l_ref, acc_ref, lg_ref, cut_ref, *, seq_len, n_sel):
    j = pl.program_id(1)
    last_block = pl.num_programs(1) - 1
    tq, tk = AT_TQ, AT_TK
    n_diag = tq // tk
    n_full = j * n_diag
    n_tot = n_full + n_diag
    q_pos = j * tq + lax.broadcasted_iota(I32, (tk, tq), 1)
    k_iota = lax.broadcasted_iota(I32, (tk, tq), 0)
    scores = score_ref.at[j % 2]
    scores_next = score_ref.at[(j + 1) % 2]

    def tile_start(kt):
        return pl.multiple_of(kt * tk, tk)

    def score_tile(kt, q_ref, w_ref, dst, dst_pos):
        r0 = tile_start(kt)
        ki_t = ki_ref[pl.ds(r0, tk), :]
        acc = jnp.zeros((tk, tq), F32)
        for h in range(IDX_HEADS):
            s = jnp.dot(ki_t, q_ref[h * IDX_DIM:(h + 1) * IDX_DIM, :], preferred_element_type=F32)
            acc = acc + w_ref[h:h + 1, :] * jnp.maximum(s, 0.0)
        dst[pl.ds(r0, tk), :] = jnp.where(r0 + k_iota <= dst_pos, acc, NEG)

    for dd in range(n_diag):
        score_tile(n_full + dd, qit_ref, wit_ref, scores, q_pos)

    n_beyond = seq_len - (j + 1) * tq

    def candidate(u):
        s = u ^ INT_MIN
        return lax.bitcast_convert_type(jnp.where(s < 0, s ^ 0x7FFFFFFF, s), F32)

    n_acc = 4

    def count_tile(c, cnts, cand):
        r0 = pl.multiple_of(c * tq, tq)
        sc = scores[pl.ds(r0, tq), :]
        cnts = list(cnts)
        for i in range(tq // SUBLANES):
            rows = sc[i * SUBLANES:(i + 1) * SUBLANES]
            cnts[i % n_acc] = jnp.where(rows >= cand, cnts[i % n_acc] + 1, cnts[i % n_acc])
        return tuple(cnts)

    zero = jnp.zeros((SUBLANES, tq), I32)

    def bit_step(b, state):
        thr_u, n_ge = state
        cand_u = thr_u | lax.shift_left(jnp.int32(1), 31 - b)
        cand = candidate(cand_u)
        cnts = lax.fori_loop(0, j + 1, functools.partial(count_tile, cand=cand), (zero,) * n_acc)
        cnt = (cnts[0] + cnts[1]) + (cnts[2] + cnts[3])
        total = jnp.sum(cnt, axis=0, keepdims=True) + jnp.where(cand <= NEG, n_beyond, 0)
        keep = total >= n_sel
        return jnp.where(keep, cand_u, thr_u), jnp.where(keep, total, n_ge)

    thr_u, n_ge = lax.fori_loop(0, 32, bit_step,
                                (jnp.zeros((1, tq), I32), jnp.full((1, tq), seq_len, I32)))
    thr = candidate(thr_u)

    sub_iota = lax.broadcasted_iota(I32, (SUBLANES, tq), 0)

    def rows_of(c):
        r0 = pl.multiple_of(c * tq, tq)
        sc = scores[pl.ds(r0, tq), :]
        return [(sc[i * SUBLANES:(i + 1) * SUBLANES], r0 + i * SUBLANES + sub_iota)
                for i in range(tq // SUBLANES)]

    cut_ref[...] = jnp.full((1, tq), seq_len, I32)

    @pl.when(jnp.max(n_ge) > n_sel)
    def _():
        def above_tile(c, gt):
            for rows, _ in rows_of(c):
                gt = jnp.where(rows > thr, gt + 1, gt)
            return gt

        gt = lax.fori_loop(0, j + 1, above_tile, zero)
        n_gt = jnp.sum(gt, axis=0, keepdims=True) + jnp.where(thr < NEG, n_beyond, 0)
        need = n_sel - n_gt
        n_bits = (seq_len - 1).bit_length()

        def eq_tile(c, cnt, cand):
            for rows, pos in rows_of(c):
                cnt = jnp.where(jnp.logical_and(rows == thr, pos < cand), cnt + 1, cnt)
            return cnt

        def index_step(b, c0):
            cand = c0 | lax.shift_left(jnp.int32(1), n_bits - 1 - b)
            cnt = lax.fori_loop(0, j + 1, functools.partial(eq_tile, cand=cand), zero)
            below = jnp.sum(cnt, axis=0, keepdims=True) + jnp.where(
                thr == NEG, jnp.clip(cand - (j + 1) * tq, 0, n_beyond), 0)
            return jnp.where(below < need, cand, c0)

        cut_ref[...] = lax.fori_loop(0, n_bits, index_step, jnp.zeros((1, tq), I32))

    cut = cut_ref[...]

    m_ref[...] = jnp.full(m_ref.shape, NEG, F32)
    l_ref[...] = jnp.zeros(l_ref.shape, F32)
    acc_ref[...] = jnp.zeros(acc_ref.shape, F32)
    rep = N_HEADS // N_KV_HEADS

    q_row = q_pos[0:1, :]
    tie_row = jnp.minimum(cut, q_row)

    def logits(kt, slot):
        r0 = tile_start(kt)
        sc = scores[pl.ds(r0, tk), :]
        last = jnp.where(sc > thr, q_row, jnp.where(sc == thr, tie_row, -1))
        bias = jnp.where(r0 + k_iota <= last, 0.0, NEG)
        for g in range(N_KV_HEADS):
            k_t = k_ref[g, pl.ds(r0, tk), :]
            for h in range(g * rep, (g + 1) * rep):
                q_h = qt_ref[h * HEAD_DIM:(h + 1) * HEAD_DIM, :]
                lg_ref[slot, h] = jnp.dot(k_t, q_h, preferred_element_type=F32) + bias

    ones_rows = jnp.ones((2 * SUBLANES, tk), BF16)

    def softmax_values(kt, slot):
        r0 = tile_start(kt)
        for g in range(N_KV_HEADS):
            v_t = jnp.concatenate([vt_ref[g * HEAD_DIM:(g + 1) * HEAD_DIM, pl.ds(r0, tk)], ones_rows], axis=0)
            for h in range(g * rep, (g + 1) * rep):
                lg = lg_ref[slot, h]
                m_old = m_ref[h]
                m_new = jnp.maximum(m_old, jnp.max(lg, axis=0, keepdims=True))
                p = jnp.exp2(lg - m_new)
                alpha = jnp.exp2(m_old - m_new)
                pv = jnp.dot(v_t, p.astype(BF16), preferred_element_type=F32)
                l_ref[h] = alpha * l_ref[h] + pv[HEAD_DIM:HEAD_DIM + 1, :]
                acc_ref[h] = alpha * acc_ref[h] + pv[:HEAD_DIM, :]
                m_ref[h] = m_new

    logits(0, 0)

    def tile_pair(i, c, with_next):
        logits(2 * i + 1, 1)
        softmax_values(2 * i, 0)
        logits(jnp.minimum(2 * i + 2, n_tot - 1), 0)
        softmax_values(2 * i + 1, 1)
        if with_next:
            for kt in (2 * i, 2 * i + 1):
                score_tile(kt, qit_next_ref, wit_next_ref, scores_next, q_pos + tq)
        return c

    @pl.when(j < last_block)
    def _():
        lax.fori_loop(0, n_tot // 2, functools.partial(tile_pair, with_next=True), 0)

    @pl.when(j == last_block)
    def _():
        lax.fori_loop(0, n_tot // 2, functools.partial(tile_pair, with_next=False), 0)

    outs = [acc_ref[h] / l_ref[h] for h in range(N_HEADS)]
    y_ref[...] = jnp.concatenate(outs, axis=0).T.astype(BF16)


def _attention(qt, qit, wit, k, ki, vt, batch, seq_len):
    n = ki.shape[0]
    tq = AT_TQ
    nq = seq_len // tq
    n_sel = min(TOPK, seq_len // 4)
    qcol = lambda b, j: (0, b * nq + j)
    qcol_next = lambda b, j: (0, b * nq + jnp.minimum(j + 1, nq - 1))
    return pl.pallas_call(
        functools.partial(_attn_kernel, seq_len=seq_len, n_sel=n_sel),
        grid=(batch, nq),
        in_specs=[
            pl.BlockSpec((D_ATTN, tq), qcol),
            pl.BlockSpec((IDX_HEADS * IDX_DIM, tq), qcol),
            pl.BlockSpec((IDX_HEADS, tq), qcol),
            pl.BlockSpec((IDX_HEADS * IDX_DIM, tq), qcol_next),
            pl.BlockSpec((IDX_HEADS, tq), qcol_next),
            pl.BlockSpec((N_KV_HEADS, seq_len, HEAD_DIM), lambda b, j: (0, b, 0)),
            pl.BlockSpec((seq_len, IDX_DIM), lambda b, j: (b, 0)),
            pl.BlockSpec((D_KV, seq_len), lambda b, j: (0, b)),
        ],
        out_specs=pl.BlockSpec((tq, D_ATTN), lambda b, j: (b * nq + j, 0)),
        out_shape=jax.ShapeDtypeStruct((n, D_ATTN), BF16),
        scratch_shapes=[pltpu.VMEM((2, seq_len, tq), F32),
                        pltpu.VMEM((N_HEADS, 1, tq), F32),
                        pltpu.VMEM((N_HEADS, 1, tq), F32),
                        pltpu.VMEM((N_HEADS, HEAD_DIM, tq), F32),
                        pltpu.VMEM((2, N_HEADS, AT_TK, tq), F32),
                        pltpu.VMEM((1, tq), I32)],
        compiler_params=_cparams(("arbitrary", "arbitrary")),
        name="attn",
    )(qt, qit, wit, qit, wit, k, ki, vt)


MIX_TM = 1024


def _split_bf16(a):
    hi = a.astype(BF16)
    return hi, (a - hi.astype(F32)).astype(BF16)


def _mix_kernel(x_ref, ys_ref, ya_ref, gs_ref, ga_ref, wps_ref, wpa_ref, wo_ref, g2_ref,
                wr_hi_ref, wr_lo_ref, br_ref, x1_ref, h_ref, idx_ref, gate_ref):
    ps = jnp.dot(ys_ref[...], wps_ref[...], preferred_element_type=F32)
    pa = jnp.dot(ya_ref[...], wpa_ref[...], preferred_element_type=F32)
    mixed = gs_ref[...].astype(F32) * ps + ga_ref[...].astype(F32) * pa
    x1 = x_ref[...] + jnp.dot(mixed.astype(BF16), wo_ref[...], preferred_element_type=F32)
    h = x1 * lax.rsqrt(jnp.mean(x1 * x1, axis=-1, keepdims=True) + RMS_EPS) * g2_ref[...]
    tm = x1.shape[0]
    for j in range(D_MODEL // LANES):
        x1_ref[pl.ds(j, tm, stride=D_MODEL // LANES), :] = x1[:, j * LANES:(j + 1) * LANES]
        h_ref[pl.ds(j, tm, stride=D_MODEL // LANES), :] = h[:, j * LANES:(j + 1) * LANES]
    h_hi, h_lo = _split_bf16(h)
    logits = (jnp.dot(h_hi, wr_hi_ref[...], preferred_element_type=F32)
              + jnp.dot(h_hi, wr_lo_ref[...], preferred_element_type=F32)
              + jnp.dot(h_lo, wr_hi_ref[...], preferred_element_type=F32)) + br_ref[...]
    lane = lax.broadcasted_iota(I32, logits.shape, 1)
    rest = logits
    firsts, vals = [], []
    for _ in range(TOP_K_EXPERTS):
        m = jnp.max(rest, axis=-1, keepdims=True)
        first = jnp.minimum(jnp.min(jnp.where(rest == m, lane, N_EXPERTS), axis=-1, keepdims=True),
                            N_EXPERTS - 1)
        firsts.append(first)
        vals.append(m)
        rest = jnp.where(lane == first, -jnp.inf, rest)
    es = [jnp.exp(v - vals[0]) for v in vals]
    denom = es[0] + es[1] + es[2] + es[3]
    wide = lax.broadcasted_iota(I32, (logits.shape[0], LANES), 1)
    idx_wide = jnp.zeros(wide.shape, I32)
    gate_wide = jnp.zeros(wide.shape, F32)
    for pick, (first, e) in enumerate(zip(firsts, es)):
        idx_wide = jnp.where(wide == pick, first, idx_wide)
        gate_wide = jnp.where(wide == pick, e / denom, gate_wide)
    idx_ref[...] = idx_wide.T[:SUBLANES]
    gate_ref[...] = gate_wide.T[:SUBLANES]


def _mix(x2, y_ssm, y_att, gs, ga, w_proj_ssm, w_proj_attn, w_out, norm2_g, w_router, b_router):
    n = x2.shape[0]
    tm = min(MIX_TM, n)
    chunks = D_MODEL // LANES
    row = lambda i: (i, 0)
    wr_hi, wr_lo = _split_bf16(w_router)
    consts = [w_proj_ssm.astype(BF16), w_proj_attn.astype(BF16), w_out.astype(BF16),
              norm2_g.reshape(1, D_MODEL), wr_hi, wr_lo, b_router.reshape(1, N_EXPERTS)]
    return pl.pallas_call(
        _mix_kernel,
        grid=(n // tm,),
        in_specs=[pl.BlockSpec((tm, D_MODEL), row), pl.BlockSpec((tm, D_SSM), row),
                  pl.BlockSpec((tm, D_ATTN), row), pl.BlockSpec((tm, D_MODEL), row),
                  pl.BlockSpec((tm, D_MODEL), row)] + [_const_spec(c.shape) for c in consts],
        out_specs=(pl.BlockSpec((tm * chunks, LANES), row), pl.BlockSpec((tm * chunks, LANES), row),
                   pl.BlockSpec((SUBLANES, tm), lambda i: (0, i)), pl.BlockSpec((SUBLANES, tm), lambda i: (0, i))),
        out_shape=(jax.ShapeDtypeStruct((n * chunks, LANES), F32), jax.ShapeDtypeStruct((n * chunks, LANES), F32),
                   jax.ShapeDtypeStruct((SUBLANES, n), I32), jax.ShapeDtypeStruct((SUBLANES, n), F32)),
        compiler_params=_cparams(("arbitrary",)),
        name="mix",
    )(x2, y_ssm, y_att, gs, ga, *consts)


MOE_TT = 2048
MOE_RB = 288
ROW_CHUNKS = D_MODEL // LANES
assert ROW_CHUNKS == SUBLANES
MOE_PITCH = MOE_RB + SUBLANES


def _moe_kernel(cnt_ref, off_ref, tok_ref, gate_ref, h_ref, win_ref, bin_ref, wout_ref, bout_ref,
                out_ref, xa_ref, xb_ref, xx_ref, ya_ref, yb_ref, yx_ref):
    i, e = pl.program_id(0), pl.program_id(1)
    trash = out_ref.shape[0] // SUBLANES - 1
    here = i * N_EXPERTS + e
    n, start = cnt_ref[here], off_ref[here]
    start_next = off_ref[i * N_EXPERTS + jnp.minimum(e + 1, N_EXPERTS - 1)]
    prev = i * N_EXPERTS + jnp.maximum(e - 1, 0)
    start_prev = off_ref[prev]
    n_prev = jnp.where(e > 0, jnp.minimum(cnt_ref[prev], MOE_RB), 0)

    def slab(t):
        return pl.ds(pl.multiple_of(t * SUBLANES, SUBLANES), SUBLANES)

    def tile_rows(r):
        return pl.ds(r, SUBLANES, stride=MOE_PITCH)

    def gather_row(base, r, x_ref):
        x_ref[tile_rows(r), :] = h_ref[slab(tok_ref[0, 0, base + r]), :]

    def gather_loop(base, x_ref):
        def body(r8, c):
            for rr in range(SUBLANES):
                gather_row(base, r8 * SUBLANES + rr, x_ref)
            return c

        lax.fori_loop(0, MOE_RB // SUBLANES, body, 0)

    def expert_mlp(x_ref, y_ref):
        xg = jnp.concatenate([x_ref[j * MOE_PITCH:j * MOE_PITCH + MOE_RB, :] for j in range(ROW_CHUNKS)],
                             axis=1).astype(BF16)
        z = jnp.dot(xg, win_ref[0], preferred_element_type=F32) + bin_ref[0]
        half = MXU_COLS // 2
        groups = range(2 * D_EXPERT // MXU_COLS)
        zg = jnp.concatenate([z[:, c * MXU_COLS:c * MXU_COLS + half] for c in groups], axis=1)
        zl = jnp.concatenate([z[:, c * MXU_COLS + half:(c + 1) * MXU_COLS] for c in groups], axis=1)
        zg = jnp.minimum(zg, SWIGLU_LIMIT)
        zl = jnp.clip(zl, -SWIGLU_LIMIT, SWIGLU_LIMIT)
        act = zg * jax.nn.sigmoid(SWIGLU_ALPHA * zg) * (zl + 1.0)
        y = jnp.dot(act.astype(BF16), wout_ref[0], preferred_element_type=F32) + bout_ref[0]
        for j in range(ROW_CHUNKS):
            y_ref[j * MOE_PITCH:j * MOE_PITCH + MOE_RB, :] = y[:, j * LANES:(j + 1) * LANES]

    def updated(base, r, n_valid, y_ref):
        t = jnp.where(r < n_valid, tok_ref[0, 0, base + r], trash)
        return t, out_ref[slab(t), :] + gate_ref[0, 0, base + r] * y_ref[tile_rows(r), :]

    def scatter_group(base, r0, n_valid, y_ref):
        rows = [updated(base, r0 + rr, n_valid, y_ref) for rr in range(SUBLANES)]
        for t, v in rows:
            out_ref[slab(t), :] = v

    def scatter_loop(base, n_valid, y_ref):
        def body(r8, c):
            scatter_group(base, r8 * SUBLANES, n_valid, y_ref)
            return c

        lax.fori_loop(0, (n_valid + SUBLANES - 1) // SUBLANES, body, 0)

    @pl.when(e == 0)
    def _():
        out_ref[...] = jnp.zeros(out_ref.shape, F32)
        gather_loop(start, xa_ref)

    @pl.when(jnp.logical_and(i == 0, e == 0))
    def _():
        yb_ref[...] = jnp.zeros(yb_ref.shape, F32)

    def pipelined(x_cur, x_next, y_cur, y_prev):
        for r in range(MOE_RB):
            gather_row(start_next, r, x_next)
        expert_mlp(x_cur, y_cur)
        for r0 in range(0, MOE_RB, SUBLANES):
            scatter_group(start_prev, r0, n_prev, y_prev)

    @pl.when(e % 2 == 0)
    def _():
        pipelined(xa_ref, xb_ref, ya_ref, yb_ref)

    @pl.when(e % 2 == 1)
    def _():
        pipelined(xb_ref, xa_ref, yb_ref, ya_ref)

    def extra_block(b, c):
        base = start + b * MOE_RB
        gather_loop(base, xx_ref)
        expert_mlp(xx_ref, yx_ref)
        scatter_loop(base, jnp.minimum(MOE_RB, n - b * MOE_RB), yx_ref)
        return c

    lax.fori_loop(1, (n + MOE_RB - 1) // MOE_RB, extra_block, 0)

    @pl.when(e == N_EXPERTS - 1)
    def _():
        scatter_loop(start, jnp.minimum(n, MOE_RB), yb_ref)


FINAL_TM = 512


def _final_kernel(x1_ref, moe_ref, g_ref, o_ref):
    tm = o_ref.shape[0]
    v = jnp.concatenate([x1_ref[pl.ds(j, tm, stride=ROW_CHUNKS), :] + moe_ref[0, pl.ds(j, tm, stride=ROW_CHUNKS), :]
                         for j in range(ROW_CHUNKS)], axis=1)
    o_ref[...] = v * lax.rsqrt(jnp.mean(v * v, axis=-1, keepdims=True) + RMS_EPS) * g_ref[...]


MXU_COLS = 256


def _expert_weights_kernel(win_ref, wout_ref, perm_ref, winp_ref, woutb_ref):
    w = win_ref[0].astype(BF16)
    for c in range(w.shape[1] // MXU_COLS):
        cols = slice(c * MXU_COLS, (c + 1) * MXU_COLS)
        winp_ref[0, :, cols] = jnp.dot(w[:, cols], perm_ref[...], preferred_element_type=F32).astype(BF16)
    woutb_ref[0] = wout_ref[0].astype(BF16)


def _expert_weights(w_moe_in, w_moe_out):
    half = MXU_COLS // 2
    src = jnp.concatenate([jnp.arange(half) * 2, jnp.arange(half) * 2 + 1])
    perm = (jnp.arange(MXU_COLS)[:, None] == src[None, :]).astype(BF16)
    per_expert = lambda a: pl.BlockSpec((1,) + a.shape[1:], lambda e: (e, 0, 0))
    return pl.pallas_call(
        _expert_weights_kernel,
        grid=(N_EXPERTS,),
        in_specs=[per_expert(w_moe_in), per_expert(w_moe_out), _const_spec(perm.shape)],
        out_specs=(per_expert(w_moe_in), per_expert(w_moe_out)),
        out_shape=(jax.ShapeDtypeStruct(w_moe_in.shape, BF16), jax.ShapeDtypeStruct(w_moe_out.shape, BF16)),
        compiler_params=_cparams(("arbitrary",)),
        name="expert_weights",
    )(w_moe_in, w_moe_out, perm)


def _moe(h, x1, idx4, gate4, w_moe_in, b_moe_in, w_moe_out, b_moe_out, norm_f_g):
    n = h.shape[0] // ROW_CHUNKS
    tt = min(MOE_TT, n)
    n_tiles = n // tt
    pairs = TOP_K_EXPERTS * tt
    cap = pairs + MOE_RB

    def per_tile(a):
        return jnp.transpose(a[:TOP_K_EXPERTS].reshape(TOP_K_EXPERTS, n_tiles, tt), (1, 0, 2)).reshape(n_tiles, pairs)

    local = jnp.arange(n, dtype=I32) % tt
    keys, gate = lax.sort((per_tile(idx4 * tt + local[None, :]), per_tile(gate4)), dimension=1, num_keys=1)
    bounds = jnp.arange(N_EXPERTS + 1, dtype=I32) * tt
    below = jnp.sum(keys[:, :, None] < bounds[None, None, :], axis=1).astype(I32)
    off, cnt = below[:, :-1], below[:, 1:] - below[:, :-1]
    pad = ((0, 0), (0, cap - pairs))
    tok = jnp.pad(keys % tt, pad).reshape(n_tiles, 1, cap)
    gate = jnp.pad(gate, pad).reshape(n_tiles, 1, cap)

    win, wout = _expert_weights(w_moe_in, w_moe_out)
    half = MXU_COLS // 2
    bin_ = jnp.swapaxes(b_moe_in.reshape(N_EXPERTS, -1, half, 2), 2, 3).reshape(N_EXPERTS, 1, 2 * D_EXPERT)
    bout = b_moe_out.reshape(N_EXPERTS, 1, D_MODEL)

    out_rows = (tt + 1) * ROW_CHUNKS
    tile1 = pl.BlockSpec((tt * ROW_CHUNKS, LANES), lambda i, e, *_: (i, 0), pipeline_mode=pl.Buffered(1))
    smem_list = pl.BlockSpec((1, 1, cap), lambda i, e, *_: (i, 0, 0), memory_space=pltpu.SMEM)
    per_expert = lambda shape: pl.BlockSpec((1,) + shape, lambda i, e, *_: (e, 0, 0))
    row_tile = pltpu.VMEM((ROW_CHUNKS * MOE_PITCH, LANES), F32)
    grid_spec = pltpu.PrefetchScalarGridSpec(
        num_scalar_prefetch=2,
        grid=(n_tiles, N_EXPERTS),
        in_specs=[smem_list, smem_list, tile1,
                  per_expert((D_MODEL, 2 * D_EXPERT)), per_expert((1, 2 * D_EXPERT)),
                  per_expert((D_EXPERT, D_MODEL)), per_expert((1, D_MODEL))],
        out_specs=pl.BlockSpec((out_rows, LANES), lambda i, e, *_: (i, 0)),
        scratch_shapes=[row_tile] * 6,
    )
    moe_out = pl.pallas_call(
        _moe_kernel,
        grid_spec=grid_spec,
        out_shape=jax.ShapeDtypeStruct((n_tiles * out_rows, LANES), F32),
        compiler_params=_cparams(("arbitrary", "arbitrary")),
        name="moe",
    )(cnt.reshape(-1), off.reshape(-1).astype(I32), tok, gate, h, win, bin_, wout, bout)

    tm = min(FINAL_TM, tt)
    per_tile = tt // tm
    return pl.pallas_call(
        _final_kernel,
        grid=(n_tiles, per_tile),
        in_specs=[pl.BlockSpec((tm * ROW_CHUNKS, LANES), lambda i, j: (i * per_tile + j, 0)),
                  pl.BlockSpec((1, tm * ROW_CHUNKS, LANES), lambda i, j: (i, j, 0)),
                  _const_spec((1, D_MODEL))],
        out_specs=pl.BlockSpec((tm, D_MODEL), lambda i, j: (i * per_tile + j, 0)),
        out_shape=jax.ShapeDtypeStruct((n, D_MODEL), F32),
        compiler_params=_cparams(("arbitrary", "arbitrary")),
        name="final_norm",
    )(x1, moe_out.reshape(n_tiles, out_rows, LANES), norm_f_g.reshape(1, D_MODEL))


def kernel(x, norm1_g, w_in, b_gate, ssm_lam_re, ssm_lam_im, ssm_log_dt, ssm_b_re, ssm_b_im, ssm_c_re, ssm_c_im, ssm_d, w_glu, b_glu, w_proj_ssm, w_proj_attn, w_out, norm2_g, w_router, b_router, w_moe_in, b_moe_in, w_moe_out, b_moe_out, norm_f_g):
    bsz, seq_len, _ = x.shape
    n = bsz * seq_len
    x2 = x.reshape(n, D_MODEL)
    u, qt, k, vt, qit, ki, wit, gs, ga = _in_proj(x2, norm1_g[0], w_in[0], b_gate[0], seq_len)
    y_ssm = _s5(u, bsz, seq_len, ssm_lam_re[0], ssm_lam_im[0], ssm_log_dt[0], ssm_b_re[0], ssm_b_im[0],
                ssm_c_re[0], ssm_c_im[0], ssm_d[0], w_glu[0], b_glu[0])
    y_att = _attention(qt, qit, wit, k, ki, vt, bsz, seq_len)
    x1, h, idx4, gate4 = _mix(x2, y_ssm, y_att, gs, ga, w_proj_ssm[0], w_proj_attn[0], w_out[0], norm2_g[0],
                              w_router[0], b_router[0])
    out = _moe(h, x1, idx4, gate4, w_moe_in[0], b_moe_in[0], w_moe_out[0], b_moe_out[0], norm_f_g)
    return out.reshape(x.shape)
```
